```python
import math
import jax, jax.numpy as jnp
from jax import lax
import numpy as np

D_MODEL = 2048
BATCH = 1
SEQ = 16384
DEPTH = 2

CHUNK = 64
Q_BLOCK = 128
N_MIXERS = 2
EPS = 1e-6

A_HEADS = 16
A_NOPE = 128
A_ROPE = 64
A_QK = A_NOPE + A_ROPE
A_V = 128
A_Q_LORA = 512
A_KV_LORA = 512
A_WIDTH = A_HEADS * A_V
A_IN = A_Q_LORA + A_KV_LORA + A_ROPE + A_WIDTH
ROPE_THETA = 10000.0

B_HEADS = 8
B_HD = 128
B_V = 2 * B_HD
B_WIDTH = B_HEADS * B_V
B_QK = B_HEADS * 2 * B_HD
B_IN = 2 * B_QK + B_WIDTH + B_WIDTH

N_A = (DEPTH + 1) // 2
N_B = DEPTH // 2

kernel_name = "hybrid_mla_diffattn_chunk_causal"


def rms_norm(x, gain):
    xf = x.astype(jnp.float32)
    y = xf * lax.rsqrt(jnp.mean(xf * xf, axis=-1, keepdims=True) + EPS)
    return (y * gain.astype(jnp.float32)).astype(x.dtype)


def rope(x, pos):
    r = x.shape[-1]
    inv_freq = ROPE_THETA ** (-jnp.arange(0, r, 2, dtype=jnp.float32) / r)
    ang = pos.astype(jnp.float32)[..., None] * inv_freq
    cos = jnp.cos(ang)[:, :, None, :]
    sin = jnp.sin(ang)[:, :, None, :]
    xf = x.astype(jnp.float32)
    x1, x2 = xf[..., : r // 2], xf[..., r // 2:]
    return jnp.concatenate([x1 * cos - x2 * sin, x1 * sin + x2 * cos], axis=-1).astype(x.dtype)


def chunk_allowed(pos_q, pos_k):
    return (pos_k // CHUNK)[:, None, :] <= (pos_q // CHUNK)[:, :, None]


def mla_attend(q, k, v, pos):
    b, s, h, d = q.shape
    nblk = s // Q_BLOCK
    scale = 1.0 / math.sqrt(d)
    k_t = k.transpose(0, 2, 1, 3)
    v_t = v.transpose(0, 2, 1, 3)
    q_blk = q.reshape(b, nblk, Q_BLOCK, h, d).transpose(1, 0, 3, 2, 4)
    p_blk = pos.reshape(b, nblk, Q_BLOCK).transpose(1, 0, 2)

    def one_block(args):
        qi, pi = args
        sc = jnp.einsum('bhqd,bhkd->bhqk', qi, k_t).astype(jnp.float32) * scale
        sc = jnp.where(chunk_allowed(pi, pos)[:, None], sc, -jnp.inf)
        p = jax.nn.softmax(sc, axis=-1)
        return jnp.einsum('bhqk,bhkd->bhqd', p.astype(v_t.dtype), v_t)

    o = lax.map(one_block, (q_blk, p_blk))
    return o.transpose(1, 0, 3, 2, 4).reshape(b, s, h, v.shape[-1])


def diff_attend(q, k, v, pos, lam):
    b, s, h, _, d = q.shape
    nblk = s // Q_BLOCK
    scale = 1.0 / math.sqrt(d)
    slopes = 2.0 ** (-8.0 * jnp.arange(1, h + 1, dtype=jnp.float32) / h)
    k_t = k.transpose(0, 2, 3, 1, 4)
    v_t = v.transpose(0, 2, 1, 3)
    q_blk = q.reshape(b, nblk, Q_BLOCK, h, 2, d).transpose(1, 0, 3, 4, 2, 5)
    p_blk = pos.reshape(b, nblk, Q_BLOCK).transpose(1, 0, 2)

    def one_block(args):
        qi, pi = args
        sc = jnp.einsum('bhcqd,bhckd->bhcqk', qi, k_t).astype(jnp.float32) * scale
        dist = jnp.abs(pi[:, :, None] - pos[:, None, :]).astype(jnp.float32)
        bias = -slopes[None, :, None, None] * dist[:, None]
        sc = sc + bias[:, :, None]
        sc = jnp.where(chunk_allowed(pi, pos)[:, None, None], sc, -jnp.inf)
        p = jax.nn.softmax(sc, axis=-1)
        p_diff = p[:, :, 0] - lam * p[:, :, 1]
        return jnp.einsum('bhqk,bhkd->bhqd', p_diff.astype(v_t.dtype), v_t)

    o = lax.map(one_block, (q_blk, p_blk))
    return o.transpose(1, 0, 3, 2, 4).reshape(b, s, h, v.shape[-1])


def mla_layer(x, pos, norm_g, w_in, q_norm_g, w_q_up, kv_norm_g, w_kv_up, q_gain, k_gain, w_out):
    b, s, _ = x.shape
    z = rms_norm(x, norm_g) @ w_in
    cq, ckv, k_pe, gate = jnp.split(
        z, [A_Q_LORA, A_Q_LORA + A_KV_LORA, A_Q_LORA + A_KV_LORA + A_ROPE], axis=-1)
    q = (rms_norm(cq, q_norm_g) @ w_q_up).reshape(b, s, A_HEADS, A_QK)
    kv = (rms_norm(ckv, kv_norm_g) @ w_kv_up).reshape(b, s, A_HEADS, A_NOPE + A_V)
    k_nope, v = kv[..., :A_NOPE], kv[..., A_NOPE:]
    k = jnp.concatenate(
        [k_nope, jnp.broadcast_to(k_pe[:, :, None, :], (b, s, A_HEADS, A_ROPE))], axis=-1)
    q = rms_norm(q, q_gain)
    k = rms_norm(k, k_gain)
    q = jnp.concatenate([q[..., :A_NOPE], rope(q[..., A_NOPE:], pos)], axis=-1)
    k = jnp.concatenate([k[..., :A_NOPE], rope(k[..., A_NOPE:], pos)], axis=-1)
    o = mla_attend(q, k, v, pos).reshape(b, s, A_WIDTH)
    return x + (o * jax.nn.silu(gate)) @ w_out


def diff_layer(x, pos, layer_idx, norm_g, w_in, q_gain, k_gain, lq1, lk1, lq2, lk2, subln_g, w_out):
    b, s, _ = x.shape
    z = rms_norm(x, norm_g) @ w_in
    q, k, v, gate = jnp.split(z, [B_QK, 2 * B_QK, 2 * B_QK + B_WIDTH], axis=-1)
    q = rms_norm(q.reshape(b, s, B_HEADS, 2, B_HD), q_gain)
    k = rms_norm(k.reshape(b, s, B_HEADS, 2, B_HD), k_gain)
    v = v.reshape(b, s, B_HEADS, B_V)
    lam_init = 0.8 - 0.6 * math.exp(-0.3 * layer_idx)
    lam = (jnp.exp(jnp.sum(lq1.astype(jnp.float32) * lk1.astype(jnp.float32)))
           - jnp.exp(jnp.sum(lq2.astype(jnp.float32) * lk2.astype(jnp.float32))) + lam_init)
    o = diff_attend(q, k, v, pos, lam)
    o = rms_norm(o, subln_g) * (1.0 - lam_init)
    return x + (o.reshape(b, s, B_WIDTH) * jax.nn.silu(gate)) @ w_out


def setup_inputs(seed: int = 0) -> dict:
    key = jax.random.key(seed)
    ks = jax.random.split(key, 24)
    f32 = jnp.float32

    def nrm(k, shape, scale):
        return jax.random.normal(k, shape, f32) * scale

    def gain(k, shape):
        return 1.0 + 0.02 * jax.random.normal(k, shape, f32)

    x = jax.random.normal(ks[0], (BATCH, SEQ, D_MODEL), f32)
    positions = jnp.broadcast_to(jnp.arange(SEQ, dtype=jnp.int32)[None], (BATCH, SEQ))
    return {
        "x": x,
        "positions": positions,
        "a_norm": gain(ks[1], (N_A, D_MODEL)),
        "a_w_in": nrm(ks[2], (N_A, D_MODEL, A_IN), D_MODEL ** -0.5),
        "a_q_norm": gain(ks[3], (N_A, A_Q_LORA)),
        "a_w_q_up": nrm(ks[4], (N_A, A_Q_LORA, A_HEADS * A_QK), A_Q_LORA ** -0.5),
        "a_kv_norm": gain(ks[5], (N_A, A_KV_LORA)),
        "a_w_kv_up": nrm(ks[6], (N_A, A_KV_LORA, A_HEADS * (A_NOPE + A_V)), A_KV_LORA ** -0.5),
        "a_q_gain": gain(ks[7], (N_A, A_QK)),
        "a_k_gain": gain(ks[8], (N_A, A_QK)),
        "a_w_out": nrm(ks[9], (N_A, A_WIDTH, D_MODEL), A_WIDTH ** -0.5),
        "b_norm": gain(ks[10], (N_B, D_MODEL)),
        "b_w_in": nrm(ks[11], (N_B, D_MODEL, B_IN), D_MODEL ** -0.5),
        "b_q_gain": gain(ks[12], (N_B, B_HD)),
        "b_k_gain": gain(ks[13], (N_B, B_HD)),
        "b_lambda_q1": nrm(ks[14], (N_B, B_HD), 0.1),
        "b_lambda_k1": nrm(ks[15], (N_B, B_HD), 0.1),
        "b_lambda_q2": nrm(ks[16], (N_B, B_HD), 0.1),
        "b_lambda_k2": nrm(ks[17], (N_B, B_HD), 0.1),
        "b_subln": gain(ks[18], (N_B, B_V)),
        "b_w_out": nrm(ks[19], (N_B, B_WIDTH, D_MODEL), B_WIDTH ** -0.5),
    }


def reference(x, positions, a_norm, a_w_in, a_q_norm, a_w_q_up, a_kv_norm, a_w_kv_up,
              a_q_gain, a_k_gain, a_w_out, b_norm, b_w_in, b_q_gain, b_k_gain,
              b_lambda_q1, b_lambda_k1, b_lambda_q2, b_lambda_k2, b_subln, b_w_out):
    for i in range(DEPTH):
        j = i // N_MIXERS
        if i % N_MIXERS == 0:
            x = mla_layer(x, positions, a_norm[j], a_w_in[j], a_q_norm[j], a_w_q_up[j],
                          a_kv_norm[j], a_w_kv_up[j], a_q_gain[j], a_k_gain[j], a_w_out[j])
        else:
            x = diff_layer(x, positions, i, b_norm[j], b_w_in[j], b_q_gain[j], b_k_gain[j],
                           b_lambda_q1[j], b_lambda_k1[j], b_lambda_q2[j], b_lambda_k2[j],
                           b_subln[j], b_w_out[j])
    return x
```

```python
import functools
import math

import jax
import jax.numpy as jnp
from jax import lax
from jax.experimental import pallas as pl
from jax.experimental.pallas import tpu as pltpu

D_MODEL = 2048
CHUNK_SHIFT = 6
EPS = 1e-6
LOG2E = 1.4426950408889634

A_HEADS = 16
A_NOPE = 128
A_ROPE = 64
A_QK = A_NOPE + A_ROPE
A_V = 128
A_Q_LORA = 512
A_KV_LORA = 512
A_WIDTH = A_HEADS * A_V
A_HEAD_PAD = 256
ROPE_THETA = 10000.0

B_HEADS = 8
B_HD = 128
B_V = 2 * B_HD
B_WIDTH = B_HEADS * B_V
B_QK = B_HEADS * 2 * B_HD

LANES = 128
NEG_BIG = -1e30

F32 = jnp.float32
BF16 = jnp.bfloat16


def _const_spec(shape):
    nd = len(shape)
    return pl.BlockSpec(shape, lambda *_: (0,) * nd, pipeline_mode=pl.Buffered(1))


def _rms(x, denom):
    return lax.rsqrt(jnp.sum(x * x, axis=-1, keepdims=True) * (1.0 / denom) + EPS)


def _a_prologue_kernel(x_ref, pos_ref, ng_ref, wa_ref, wg_ref, qng_ref, wq_ref, kvng_ref, wkv_ref,
                       gq_ref, gkn_ref, gkp_ref, invf_ref, sgn_ref,
                       q_ref, k_ref, v_ref, sg_ref):
    x = x_ref[...]
    xn = (x * _rms(x, D_MODEL) * ng_ref[...]).astype(BF16)
    za = jnp.dot(xn, wa_ref[...], preferred_element_type=F32)
    gate = jnp.dot(xn, wg_ref[...], preferred_element_type=F32)
    sg_ref[...] = (gate * jax.nn.sigmoid(gate)).astype(BF16)

    cq = za[:, :A_Q_LORA]
    ckv = za[:, A_Q_LORA:A_Q_LORA + A_KV_LORA]
    kpe = za[:, A_Q_LORA + A_KV_LORA:]
    cqn = (cq * _rms(cq, A_Q_LORA) * qng_ref[...]).astype(BF16)
    ckvn = (ckv * _rms(ckv, A_KV_LORA) * kvng_ref[...]).astype(BF16)
    q = jnp.dot(cqn, wq_ref[...], preferred_element_type=F32)
    kv = jnp.dot(ckvn, wkv_ref[...], preferred_element_type=F32)

    ang = pos_ref[...].astype(F32) * invf_ref[...]
    cos = jnp.cos(ang)
    sin_signed = jnp.sin(ang) * sgn_ref[...]
    lane = lax.broadcasted_iota(jnp.int32, cos.shape, 1)
    first_half = lane < (A_ROPE // 2)

    def rope(t):
        swapped = jnp.where(first_half, pltpu.roll(t, LANES - A_ROPE // 2, 1),
                            pltpu.roll(t, A_ROPE // 2, 1))
        return t * cos + swapped * sin_signed

    gq = gq_ref[...]
    gkn = gkn_ref[...]
    kpe_ss = jnp.sum(kpe * kpe, axis=-1, keepdims=True)
    kpe_roped = rope(kpe * gkp_ref[...])
    for h in range(A_HEADS):
        qh = q[:, h * A_HEAD_PAD:(h + 1) * A_HEAD_PAD]
        qn = qh * _rms(qh, A_QK) * gq
        q_ref[:, h * A_HEAD_PAD:h * A_HEAD_PAD + LANES] = qn[:, :LANES].astype(BF16)
        q_ref[:, h * A_HEAD_PAD + LANES:(h + 1) * A_HEAD_PAD] = rope(qn[:, LANES:]).astype(BF16)

        kn = kv[:, h * A_NOPE:(h + 1) * A_NOPE]
        r = lax.rsqrt((jnp.sum(kn * kn, axis=-1, keepdims=True) + kpe_ss) * (1.0 / A_QK) + EPS)
        k_ref[:, h * A_HEAD_PAD:h * A_HEAD_PAD + LANES] = (kn * r * gkn).astype(BF16)
        k_ref[:, h * A_HEAD_PAD + LANES:(h + 1) * A_HEAD_PAD] = (kpe_roped * r).astype(BF16)
    v_ref[...] = kv[:, A_HEADS * A_NOPE:].astype(BF16)


def _a_prologue(x, pos_col, ng, wa, wg, qng, wq, kvng, wkv, gq, gkn, gkp, invf, sgn, tm):
    s = x.shape[0]
    row = lambda w: pl.BlockSpec((tm, w), lambda i: (i, 0))
    consts = (ng, wa, wg, qng, wq, kvng, wkv, gq, gkn, gkp, invf, sgn)
    return pl.pallas_call(
        _a_prologue_kernel,
        grid=(s // tm,),
        in_specs=[row(D_MODEL), row(1)] + [_const_spec(c.shape) for c in consts],
        out_specs=[row(A_HEADS * A_HEAD_PAD), row(A_HEADS * A_HEAD_PAD), row(A_WIDTH), row(A_WIDTH)],
        out_shape=[jax.ShapeDtypeStruct((s, A_HEADS * A_HEAD_PAD), BF16),
                   jax.ShapeDtypeStruct((s, A_HEADS * A_HEAD_PAD), BF16),
                   jax.ShapeDtypeStruct((s, A_WIDTH), BF16),
                   jax.ShapeDtypeStruct((s, A_WIDTH), BF16)],
        compiler_params=pltpu.CompilerParams(dimension_semantics=("arbitrary",),
                                             vmem_limit_bytes=56 * 1024 * 1024),
        name="a_prologue",
    )(x, pos_col, *consts)


def _b_prologue_kernel(x_ref, ng_ref, w_ref, gq_ref, gk_ref, q_ref, k_ref, v_ref, sg_ref, xn_sc):
    j = pl.program_id(1)

    @pl.when(j == 0)
    def _():
        x = x_ref[...]
        xn_sc[...] = (x * _rms(x, D_MODEL) * ng_ref[...]).astype(BF16)

    z = jnp.dot(xn_sc[...], w_ref[...], preferred_element_type=F32)

    def head_norm(out_ref, g):
        for c in range(B_QK // B_HD):
            t = z[:, c * B_HD:(c + 1) * B_HD]
            out_ref[:, c * B_HD:(c + 1) * B_HD] = (t * _rms(t, B_HD) * g).astype(BF16)

    @pl.when(j == 0)
    def _():
        head_norm(q_ref, gq_ref[...])

    @pl.when(j == 1)
    def _():
        head_norm(k_ref, gk_ref[...])

    @pl.when(j == 2)
    def _():
        v_ref[...] = z.astype(BF16)

    @pl.when(j == 3)
    def _():
        sg_ref[...] = (z * jax.nn.sigmoid(z)).astype(BF16)


def _b_prologue(x, ng, w, gq, gk, tm):
    s = x.shape[0]
    row = lambda: pl.BlockSpec((tm, B_WIDTH), lambda i, j: (i, 0))
    const = lambda a: pl.BlockSpec(a.shape, lambda i, j: (0, 0))
    return pl.pallas_call(
        _b_prologue_kernel,
        grid=(s // tm, 4),
        in_specs=[row(), const(ng), pl.BlockSpec((D_MODEL, B_WIDTH), lambda i, j: (0, j)),
                  const(gq), const(gk)],
        out_specs=[row(), row(), row(), row()],
        out_shape=[jax.ShapeDtypeStruct((s, B_WIDTH), BF16)] * 4,
        scratch_shapes=[pltpu.VMEM((tm, D_MODEL), BF16)],
        compiler_params=pltpu.CompilerParams(dimension_semantics=("arbitrary", "arbitrary"),
                                             vmem_limit_bytes=56 * 1024 * 1024),
        name="b_prologue",
    )(x, ng, w, gq, gk)


def _out_proj_kernel(x_ref, o_ref, sg_ref, w_ref, y_ref):
    a = o_ref[...] * sg_ref[...]
    y_ref[...] = x_ref[...] + jnp.dot(a, w_ref[...], preferred_element_type=F32)


def _out_proj(x, o, sg, w, tm, name):
    s = x.shape[0]
    row = lambda: pl.BlockSpec((tm, D_MODEL), lambda i: (i, 0))
    return pl.pallas_call(
        _out_proj_kernel,
        grid=(s // tm,),
        in_specs=[row(), row(), row(), _const_spec(w.shape)],
        out_specs=row(),
        out_shape=jax.ShapeDtypeStruct((s, D_MODEL), F32),
        compiler_params=pltpu.CompilerParams(dimension_semantics=("arbitrary",),
                                             vmem_limit_bytes=48 * 1024 * 1024),
        name=name,
    )(x, o, sg, w)


def _tile_tables(pos, tq, tk):
    s = pos.shape[0]
    nq, nk = s // tq, s // tk
    ch = jnp.right_shift(pos, CHUNK_SHIFT)
    qch = ch.reshape(nq, tq)
    kch = ch.reshape(nk, tk)
    qmin, qmax = qch.min(axis=1), qch.max(axis=1)
    kmin, kmax = kch.min(axis=1), kch.max(axis=1)
    hidden = kmin[None, :] > qmax[:, None]
    full = kmax[None, :] <= qmin[:, None]
    masked = jnp.logical_not(hidden | full)

    def lists(flag):
        order = jnp.argsort(jnp.logical_not(flag), axis=1, stable=True)
        return order.astype(jnp.int32).reshape(-1), flag.sum(axis=1).astype(jnp.int32)

    full_list, full_cnt = lists(full)
    mask_list, mask_cnt = lists(masked)
    return full_cnt, mask_cnt, full_list, mask_list


def _online_update(s, v, m_ref, l_ref, acc_ref):
    m_prev = m_ref[...]
    m_new = jnp.maximum(m_prev, jnp.max(s, axis=1, keepdims=True))
    alpha = jnp.exp2(m_prev - m_new)
    p = jnp.exp2(s - m_new)
    l_ref[...] = alpha * l_ref[...] + jnp.sum(p, axis=1, keepdims=True)
    acc_ref[...] = alpha * acc_ref[...] + jnp.dot(p.astype(BF16), v, preferred_element_type=F32)
    m_ref[...] = m_new


_NT = (((1,), (1,)), ((), ()))


def _a_attn_kernel(fcnt_ref, mcnt_ref, flist_ref, mlist_ref,
                   q_ref, k_ref, v_ref, pq_ref, pk_ref, o_ref, m_sc, l_sc, acc_sc, *, tk, nk):
    qi = pl.program_id(1)
    q = q_ref[...]
    qchunk = jnp.right_shift(pq_ref[...], CHUNK_SHIFT)
    m_sc[...] = jnp.full_like(m_sc, NEG_BIG)
    l_sc[...] = jnp.zeros_like(l_sc)
    acc_sc[...] = jnp.zeros_like(acc_sc)

    def step(ki, masked):
        start = pl.multiple_of(ki * tk, tk)
        k = k_ref[pl.ds(start, tk), :]
        v = v_ref[pl.ds(start, tk), :]
        s = lax.dot_general(q, k, _NT, preferred_element_type=F32)
        if masked:
            kchunk = jnp.right_shift(pk_ref[ki], CHUNK_SHIFT)
            s = jnp.where(kchunk <= qchunk, s, NEG_BIG)
        _online_update(s, v, m_sc, l_sc, acc_sc)

    def full_body(t, c):
        step(flist_ref[qi * nk + t], False)
        return c

    def mask_body(t, c):
        step(mlist_ref[qi * nk + t], True)
        return c

    lax.fori_loop(0, fcnt_ref[qi], full_body, 0)
    lax.fori_loop(0, mcnt_ref[qi], mask_body, 0)
    o_ref[...] = (acc_sc[...] / l_sc[...]).astype(BF16)


def _a_attn(q, k, v, pos_col, pos_k3, tables, tq, tk):
    s = q.shape[0]
    nq, nk = s // tq, s // tk
    grid_spec = pltpu.PrefetchScalarGridSpec(
        num_scalar_prefetch=4,
        grid=(A_HEADS, nq),
        in_specs=[pl.BlockSpec((tq, A_HEAD_PAD), lambda h, i, *_: (i, h)),
                  pl.BlockSpec((s, A_HEAD_PAD), lambda h, i, *_: (0, h)),
                  pl.BlockSpec((s, A_V), lambda h, i, *_: (0, h)),
                  pl.BlockSpec((tq, 1), lambda h, i, *_: (i, 0)),
                  pl.BlockSpec((nk, 1, tk), lambda h, i, *_: (0, 0, 0))],
        out_specs=pl.BlockSpec((tq, A_V), lambda h, i, *_: (i, h)),
        scratch_shapes=[pltpu.VMEM((tq, 1), F32), pltpu.VMEM((tq, 1), F32),
                        pltpu.VMEM((tq, A_V), F32)],
    )
    return pl.pallas_call(
        functools.partial(_a_attn_kernel, tk=tk, nk=nk),
        grid_spec=grid_spec,
        out_shape=jax.ShapeDtypeStruct((s, A_WIDTH), BF16),
        compiler_params=pltpu.CompilerParams(dimension_semantics=("arbitrary", "arbitrary"),
                                             vmem_limit_bytes=52 * 1024 * 1024),
        name="a_attention",
    )(*tables, q, k, v, pos_col, pos_k3)


def _b_attn_kernel(fcnt_ref, mcnt_ref, flist_ref, mlist_ref,
                   q_ref, k_ref, v_ref, pq_ref, pk_ref, slope_ref, lam_ref, sub_ref, o_ref,
                   m_sc, l_sc, acc_sc, *, tk, nk, out_scale):
    qi = pl.program_id(1)
    q = q_ref[...]
    pq = pq_ref[...]
    qchunk = jnp.right_shift(pq, CHUNK_SHIFT)
    neg_slope = slope_ref[0, :, 0:1]
    m_sc[...] = jnp.full_like(m_sc, NEG_BIG)
    l_sc[...] = jnp.zeros_like(l_sc)
    acc_sc[...] = jnp.zeros_like(acc_sc)

    def step(ki, masked):
        start = pl.multiple_of(ki * tk, tk)
        k = k_ref[pl.ds(start, tk), :]
        v = v_ref[pl.ds(start, tk), :]
        pk = pk_ref[ki]
        bias = jnp.abs(pq - pk).astype(F32) * neg_slope
        if masked:
            visible = jnp.right_shift(pk, CHUNK_SHIFT) <= qchunk
        for c in range(2):
            s = lax.dot_general(q[:, c * B_HD:(c + 1) * B_HD], k[:, c * B_HD:(c + 1) * B_HD],
                                _NT, preferred_element_type=F32) + bias
            if masked:
                s = jnp.where(visible, s, NEG_BIG)
            _online_update(s, v, m_sc.at[c], l_sc.at[c], acc_sc.at[c])

    def full_body(t, c):
        step(flist_ref[qi * nk + t], False)
        return c

    def mask_body(t, c):
        step(mlist_ref[qi * nk + t], True)
        return c

    lax.fori_loop(0, fcnt_ref[qi], full_body, 0)
    lax.fori_loop(0, mcnt_ref[qi], mask_body, 0)

    lam = lam_ref[0:1, 0:1]
    od = acc_sc[0] / l_sc[0] - lam * (acc_sc[1] / l_sc[1])
    o_ref[...] = (od * _rms(od, B_V) * (sub_ref[...] * out_scale)).astype(BF16)


def _b_attn(q, k, v, pos_col, pos_k3, slopes, lam, subln, tables, tq, tk, out_scale):
    s = q.shape[0]
    nq, nk = s // tq, s // tk
    grid_spec = pltpu.PrefetchScalarGridSpec(
        num_scalar_prefetch=4,
        grid=(B_HEADS, nq),
        in_specs=[pl.BlockSpec((tq, B_V), lambda h, i, *_: (i, h)),
                  pl.BlockSpec((s, B_V), lambda h, i, *_: (0, h)),
                  pl.BlockSpec((s, B_V), lambda h, i, *_: (0, h)),
                  pl.BlockSpec((tq, 1), lambda h, i, *_: (i, 0)),
                  pl.BlockSpec((nk, 1, tk), lambda h, i, *_: (0, 0, 0)),
                  pl.BlockSpec((1, 1, LANES), lambda h, i, *_: (h, 0, 0)),
                  pl.BlockSpec((1, LANES), lambda h, i, *_: (0, 0)),
                  pl.BlockSpec((1, B_V), lambda h, i, *_: (0, 0))],
        out_specs=pl.BlockSpec((tq, B_V), lambda h, i, *_: (i, h)),
        scratch_shapes=[pltpu.VMEM((2, tq, 1), F32), pltpu.VMEM((2, tq, 1), F32),
                        pltpu.VMEM((2, tq, B_V), F32)],
    )
    return pl.pallas_call(
        functools.partial(_b_attn_kernel, tk=tk, nk=nk, out_scale=out_scale),
        grid_spec=grid_spec,
        out_shape=jax.ShapeDtypeStruct((s, B_WIDTH), BF16),
        compiler_params=pltpu.CompilerParams(dimension_semantics=("arbitrary", "arbitrary"),
                                             vmem_limit_bytes=56 * 1024 * 1024),
        name="b_attention",
    )(*tables, q, k, v, pos_col, pos_k3, slopes, lam, subln)


def _a_weights(w_in, w_q_up, w_kv_up, q_gain, k_gain):
    n_lat = A_Q_LORA + A_KV_LORA + A_ROPE
    wa = jnp.pad(w_in[:, :n_lat], ((0, 0), (0, LANES - A_ROPE))).astype(BF16)
    wg = w_in[:, n_lat:].astype(BF16)
    wq = w_q_up.reshape(A_Q_LORA, A_HEADS, A_QK)
    wq = jnp.pad(wq, ((0, 0), (0, 0), (0, A_HEAD_PAD - A_QK)))
    wq = wq.reshape(A_Q_LORA, A_HEADS * A_HEAD_PAD).astype(BF16)
    wkv = w_kv_up.reshape(A_KV_LORA, A_HEADS, A_NOPE + A_V)
    wkv = jnp.concatenate([wkv[:, :, :A_NOPE].reshape(A_KV_LORA, -1),
                           wkv[:, :, A_NOPE:].reshape(A_KV_LORA, -1)], axis=1).astype(BF16)
    q_scale = LOG2E / math.sqrt(A_QK)
    gq = jnp.pad(q_gain * q_scale, (0, A_HEAD_PAD - A_QK)).reshape(1, A_HEAD_PAD)
    gkn = k_gain[:A_NOPE].reshape(1, LANES)
    gkp = jnp.pad(k_gain[A_NOPE:], (0, LANES - A_ROPE)).reshape(1, LANES)
    return wa, wg, wq, wkv, gq, gkn, gkp


def _rope_tables():
    half = A_ROPE // 2
    inv_freq = ROPE_THETA ** (-jnp.arange(0, A_ROPE, 2, dtype=F32) / A_ROPE)
    zeros = jnp.zeros((LANES - A_ROPE,), F32)
    invf = jnp.concatenate([inv_freq, inv_freq, zeros]).reshape(1, LANES)
    sgn = jnp.concatenate([-jnp.ones((half,), F32), jnp.ones((half,), F32), zeros]).reshape(1, LANES)
    return invf, sgn


def kernel(x, positions, a_norm, a_w_in, a_q_norm, a_w_q_up, a_kv_norm, a_w_kv_up, a_q_gain, a_k_gain, a_w_out, b_norm, b_w_in, b_q_gain, b_k_gain, b_lambda_q1, b_lambda_k1, b_lambda_q2, b_lambda_k2, b_subln, b_w_out):
    batch, seq, _ = x.shape
    assert batch == 1
    xs = x[0]
    pos = positions[0]
    pos_col = pos.reshape(seq, 1)

    tq_a, tk_a = 512, 1024
    tq_b, tk_b = 512, 1024

    wa, wg, wq, wkv, gq, gkn, gkp = _a_weights(a_w_in[0], a_w_q_up[0], a_w_kv_up[0],
                                               a_q_gain[0], a_k_gain[0])
    invf, sgn = _rope_tables()
    q, k, v, sg = _a_prologue(xs, pos_col, a_norm[0].reshape(1, -1), wa, wg,
                              a_q_norm[0].reshape(1, -1), wq, a_kv_norm[0].reshape(1, -1), wkv,
                              gq, gkn, gkp, invf, sgn, tm=256)
    tables = _tile_tables(pos, tq_a, tk_a)
    o = _a_attn(q, k, v, pos_col, pos.reshape(seq // tk_a, 1, tk_a), tables, tq_a, tk_a)
    xs = _out_proj(xs, o, sg, a_w_out[0].astype(BF16), tm=512, name="a_out_proj")

    layer_idx = 1
    lam_init = 0.8 - 0.6 * math.exp(-0.3 * layer_idx)
    b_scale = LOG2E / math.sqrt(B_HD)
    q, k, v, sg = _b_prologue(xs, b_norm[0].reshape(1, -1), b_w_in[0].astype(BF16),
                              (b_q_gain[0] * b_scale).reshape(1, -1), b_k_gain[0].reshape(1, -1),
                              tm=512)
    slopes = 2.0 ** (-8.0 * jnp.arange(1, B_HEADS + 1, dtype=F32) / B_HEADS)
    neg_slopes = jnp.broadcast_to((-LOG2E * slopes)[:, None, None], (B_HEADS, 1, LANES))
    lam = (jnp.exp(jnp.sum(b_lambda_q1[0] * b_lambda_k1[0]))
           - jnp.exp(jnp.sum(b_lambda_q2[0] * b_lambda_k2[0])) + lam_init)
    lam = jnp.broadcast_to(lam.reshape(1, 1), (1, LANES)).astype(F32)
    tables = _tile_tables(pos, tq_b, tk_b)
    o = _b_attn(q, k, v, pos_col, pos.reshape(seq // tk_b, 1, tk_b), neg_slopes, lam,
                b_subln[0].reshape(1, -1), tables, tq_b, tk_b, 1.0 - lam_init)
    xs = _out_proj(xs, o, sg, b_w_out[0].astype(BF16), tm=512, name="b_out_proj")
    return xs[None]
```

```python
import functools
import math

import jax
import jax.numpy as jnp
from jax import lax
from jax.experimental import pallas as pl
from jax.experimental.pallas import tpu as pltpu

D_MODEL = 2048
CHUNK_SHIFT = 6
EPS = 1e-6
LOG2E = 1.4426950408889634

A_HEADS = 16
A_NOPE = 128
A_ROPE = 64
A_QK = A_NOPE + A_ROPE
A_V = 128
A_Q_LORA = 512
A_KV_LORA = 512
A_WIDTH = A_HEADS * A_V
A_HEAD_PAD = 256
A_V_PAD = 256
A_ONE_COL = A_V + A_ROPE
ROPE_THETA = 10000.0

B_HEADS = 8
B_HD = 128
B_V = 2 * B_HD
B_WIDTH = B_HEADS * B_V
B_QK = B_HEADS * 2 * B_HD

LANES = 128
NEG_BIG = -1e30
MAX_CONST_SHIFT = 50.0
BOUND_MARGIN = 1.01

F32 = jnp.float32
BF16 = jnp.bfloat16
_NT = (((1,), (1,)), ((), ()))


def _const_spec(shape):
    nd = len(shape)
    return pl.BlockSpec(shape, lambda *_: (0,) * nd, pipeline_mode=pl.Buffered(1))


def _rms(x, denom):
    return lax.rsqrt(jnp.sum(x * x, axis=-1, keepdims=True) * (1.0 / denom) + EPS)


def _a_prologue_kernel(x_ref, pos_ref, ng_ref, wa_ref, wg_ref, qng_ref, wq_ref, kvng_ref, wkv_ref,
                       gq_ref, gkn_ref, gkp_ref, invf_ref, sgn_ref, qshift_ref, one_ref,
                       q_ref, k_ref, v_ref, sg_ref):
    x = x_ref[...]
    xn = (x * _rms(x, D_MODEL) * ng_ref[...]).astype(BF16)
    za = jnp.dot(xn, wa_ref[...], preferred_element_type=F32)
    gate = jnp.dot(xn, wg_ref[...], preferred_element_type=F32)
    sg_ref[...] = (gate * jax.nn.sigmoid(gate)).astype(BF16)

    cq = za[:, :A_Q_LORA]
    ckv = za[:, A_Q_LORA:A_Q_LORA + A_KV_LORA]
    kpe = za[:, A_Q_LORA + A_KV_LORA:]
    cqn = (cq * _rms(cq, A_Q_LORA) * qng_ref[...]).astype(BF16)
    ckvn = (ckv * _rms(ckv, A_KV_LORA) * kvng_ref[...]).astype(BF16)
    q = jnp.dot(cqn, wq_ref[...], preferred_element_type=F32)
    kv = jnp.dot(ckvn, wkv_ref[...], preferred_element_type=F32)

    ang = pos_ref[...].astype(F32) * invf_ref[...]
    cos = jnp.cos(ang)
    sin_signed = jnp.sin(ang) * sgn_ref[...]
    lane = lax.broadcasted_iota(jnp.int32, cos.shape, 1)
    first_half = lane < (A_ROPE // 2)

    def rope(t):
        swapped = jnp.where(first_half, pltpu.roll(t, LANES - A_ROPE // 2, 1),
                            pltpu.roll(t, A_ROPE // 2, 1))
        return t * cos + swapped * sin_signed

    gq = gq_ref[...]
    gkn = gkn_ref[...]
    qshift = qshift_ref[...]
    one = one_ref[...]
    v_one = jnp.broadcast_to(one, (x.shape[0], LANES)).astype(BF16)
    kpe_ss = jnp.sum(kpe * kpe, axis=-1, keepdims=True)
    kpe_roped = rope(kpe * gkp_ref[...])
    for h in range(A_HEADS):
        qh = q[:, h * A_HEAD_PAD:(h + 1) * A_HEAD_PAD]
        qn = qh * _rms(qh, A_QK) * gq
        q_ref[:, h * A_HEAD_PAD:h * A_HEAD_PAD + LANES] = qn[:, :LANES].astype(BF16)
        q_ref[:, h * A_HEAD_PAD + LANES:(h + 1) * A_HEAD_PAD] = (
            rope(qn[:, LANES:]) + qshift).astype(BF16)

        kn = kv[:, h * A_NOPE:(h + 1) * A_NOPE]
        r = lax.rsqrt((jnp.sum(kn * kn, axis=-1, keepdims=True) + kpe_ss) * (1.0 / A_QK) + EPS)
        k_ref[:, h * A_HEAD_PAD:h * A_HEAD_PAD + LANES] = (kn * r * gkn).astype(BF16)
        k_ref[:, h * A_HEAD_PAD + LANES:(h + 1) * A_HEAD_PAD] = (kpe_roped * r + one).astype(BF16)

        v_ref[:, h * A_V_PAD:h * A_V_PAD + A_V] = (
            kv[:, A_HEADS * A_NOPE + h * A_V:A_HEADS * A_NOPE + (h + 1) * A_V].astype(BF16))
        v_ref[:, h * A_V_PAD + A_V:(h + 1) * A_V_PAD] = v_one


def _a_prologue(x, pos_col, consts, tm):
    s = x.shape[0]
    row = lambda w: pl.BlockSpec((tm, w), lambda i: (i, 0))
    return pl.pallas_call(
        _a_prologue_kernel,
        grid=(s // tm,),
        in_specs=[row(D_MODEL), row(1)] + [_const_spec(c.shape) for c in consts],
        out_specs=[row(A_HEADS * A_HEAD_PAD), row(A_HEADS * A_HEAD_PAD), row(A_HEADS * A_V_PAD),
                   row(A_WIDTH)],
        out_shape=[jax.ShapeDtypeStruct((s, A_HEADS * A_HEAD_PAD), BF16),
                   jax.ShapeDtypeStruct((s, A_HEADS * A_HEAD_PAD), BF16),
                   jax.ShapeDtypeStruct((s, A_HEADS * A_V_PAD), BF16),
                   jax.ShapeDtypeStruct((s, A_WIDTH), BF16)],
        compiler_params=pltpu.CompilerParams(dimension_semantics=("arbitrary",),
                                             vmem_limit_bytes=58 * 1024 * 1024),
        name="a_prologue",
    )(x, pos_col, *consts)


def _b_prologue_kernel(x_ref, ng_ref, w_ref, gq_ref, gk_ref, q_ref, k_ref, v_ref, sg_ref, xn_sc):
    j = pl.program_id(1)

    @pl.when(j == 0)
    def _():
        x = x_ref[...]
        xn_sc[...] = (x * _rms(x, D_MODEL) * ng_ref[...]).astype(BF16)

    z = jnp.dot(xn_sc[...], w_ref[...], preferred_element_type=F32)

    def head_norm(out_ref, g):
        for c in range(B_QK // B_HD):
            t = z[:, c * B_HD:(c + 1) * B_HD]
            out_ref[:, c * B_HD:(c + 1) * B_HD] = (t * _rms(t, B_HD) * g).astype(BF16)

    @pl.when(j == 0)
    def _():
        head_norm(q_ref, gq_ref[...])

    @pl.when(j == 1)
    def _():
        head_norm(k_ref, gk_ref[...])

    @pl.when(j == 2)
    def _():
        v_ref[...] = z.astype(BF16)

    @pl.when(j == 3)
    def _():
        sg_ref[...] = (z * jax.nn.sigmoid(z)).astype(BF16)


def _b_prologue(x, ng, w, gq, gk, tm):
    s = x.shape[0]
    row = lambda: pl.BlockSpec((tm, B_WIDTH), lambda i, j: (i, 0))
    const = lambda a: pl.BlockSpec(a.shape, lambda i, j: (0, 0))
    return pl.pallas_call(
        _b_prologue_kernel,
        grid=(s // tm, 4),
        in_specs=[row(), const(ng), pl.BlockSpec((D_MODEL, B_WIDTH), lambda i, j: (0, j)),
                  const(gq), const(gk)],
        out_specs=[row(), row(), row(), row()],
        out_shape=[jax.ShapeDtypeStruct((s, B_WIDTH), BF16)] * 4,
        scratch_shapes=[pltpu.VMEM((tm, D_MODEL), BF16)],
        compiler_params=pltpu.CompilerParams(dimension_semantics=("arbitrary", "arbitrary"),
                                             vmem_limit_bytes=56 * 1024 * 1024),
        name="b_prologue",
    )(x, ng, w, gq, gk)


def _out_proj_kernel(x_ref, o_ref, sg_ref, w_ref, y_ref):
    a = o_ref[...] * sg_ref[...]
    y_ref[...] = x_ref[...] + jnp.dot(a, w_ref[...], preferred_element_type=F32)


def _out_proj(x, o, sg, w, tm, name):
    s = x.shape[0]
    row = lambda: pl.BlockSpec((tm, D_MODEL), lambda i: (i, 0))
    return pl.pallas_call(
        _out_proj_kernel,
        grid=(s // tm,),
        in_specs=[row(), row(), row(), _const_spec(w.shape)],
        out_specs=row(),
        out_shape=jax.ShapeDtypeStruct((s, D_MODEL), F32),
        compiler_params=pltpu.CompilerParams(dimension_semantics=("arbitrary",),
                                             vmem_limit_bytes=48 * 1024 * 1024),
        name=name,
    )(x, o, sg, w)


def _tile_tables(pos, tq, tk):
    s = pos.shape[0]
    nq, nk = s // tq, s // tk
    ch = jnp.right_shift(pos, CHUNK_SHIFT)
    qmax = ch.reshape(nq, tq).max(axis=1)
    kmin = ch.reshape(nk, tk).min(axis=1)
    visible = kmin[None, :] <= qmax[:, None]
    order = jnp.argsort(jnp.logical_not(visible), axis=1, stable=True)
    return visible.sum(axis=1).astype(jnp.int32), order.astype(jnp.int32).reshape(-1)


def _pipelined_tiles(n, tile_at, produce, consume):
    produce(tile_at(0), 0)

    def pair(u, c):
        t = 2 * u
        produce(tile_at(t + 1), 1)
        consume(tile_at(t), 0)
        produce(tile_at(t + 2), 0)
        consume(tile_at(t + 1), 1)
        return c

    lax.fori_loop(0, (n - 1) // 2, pair, 0)
    last = n - 1

    @pl.when(last % 2 == 1)
    def _():
        produce(tile_at(last), 1)
        consume(tile_at(last - 1), 0)
        consume(tile_at(last), 1)

    @pl.when(last % 2 == 0)
    def _():
        consume(tile_at(last), 0)


def _lane_group_sum(p):
    out = p[:, :LANES]
    for j in range(1, p.shape[1] // LANES):
        out = out + p[:, j * LANES:(j + 1) * LANES]
    return out


def _a_attn_kernel(cnt_ref, list_ref, q_ref, k_ref, v_ref, pq_ref, pk_ref, o_ref,
                   p_sc, acc_sc, m_sc, alpha_sc, *, tk, nk, online):
    qi = pl.program_id(1)
    q = q_ref[...]
    qchunk = jnp.right_shift(pq_ref[...], CHUNK_SHIFT)
    acc_sc[...] = jnp.zeros_like(acc_sc)
    if online:
        m_sc[...] = jnp.full_like(m_sc, NEG_BIG)

    def produce(ki, slot):
        k = k_ref[pl.ds(pl.multiple_of(ki * tk, tk), tk), :]
        s = lax.dot_general(q, k, _NT, preferred_element_type=F32)
        kchunk = jnp.right_shift(pk_ref[ki], CHUNK_SHIFT)
        s = jnp.where(kchunk <= qchunk, s, NEG_BIG)
        if online:
            m_prev = m_sc[...]
            m_new = jnp.maximum(m_prev, jnp.max(s, axis=1, keepdims=True))
            alpha_sc[slot] = jnp.exp2(m_prev - m_new)
            m_sc[...] = m_new
            s = s - m_new
        p_sc[slot] = jnp.exp2(s).astype(BF16)

    def consume(ki, slot):
        v = v_ref[pl.ds(pl.multiple_of(ki * tk, tk), tk), :]
        pv = jnp.dot(p_sc[slot], v, preferred_element_type=F32)
        if online:
            acc_sc[...] = alpha_sc[slot] * acc_sc[...] + pv
        else:
            acc_sc[...] += pv

    _pipelined_tiles(cnt_ref[qi], lambda t: list_ref[qi * nk + t], produce, consume)
    acc = acc_sc[...]
    o_ref[...] = (acc[:, :A_V] / acc[:, A_ONE_COL:A_ONE_COL + 1]).astype(BF16)


def _a_attn(q, k, v, pos_col, pos_k3, tables, tq, tk, online):
    s = q.shape[0]
    nq, nk = s // tq, s // tk
    grid_spec = pltpu.PrefetchScalarGridSpec(
        num_scalar_prefetch=2,
        grid=(A_HEADS, nq),
        in_specs=[pl.BlockSpec((tq, A_HEAD_PAD), lambda h, i, *_: (i, h)),
                  pl.BlockSpec((s, A_HEAD_PAD), lambda h, i, *_: (0, h)),
                  pl.BlockSpec((s, A_V_PAD), lambda h, i, *_: (0, h)),
                  pl.BlockSpec((tq, 1), lambda h, i, *_: (i, 0)),
                  pl.BlockSpec((nk, 1, tk), lambda h, i, *_: (0, 0, 0))],
        out_specs=pl.BlockSpec((tq, A_V), lambda h, i, *_: (i, h)),
        scratch_shapes=[pltpu.VMEM((2, tq, tk), BF16), pltpu.VMEM((tq, A_V_PAD), F32),
                        pltpu.VMEM((tq, 1), F32), pltpu.VMEM((2, tq, 1), F32)],
    )
    return pl.pallas_call(
        functools.partial(_a_attn_kernel, tk=tk, nk=nk, online=online),
        grid_spec=grid_spec,
        out_shape=jax.ShapeDtypeStruct((s, A_WIDTH), BF16),
        compiler_params=pltpu.CompilerParams(dimension_semantics=("arbitrary", "arbitrary"),
                                             vmem_limit_bytes=56 * 1024 * 1024),
        name="a_attention_online" if online else "a_attention",
    )(*tables, q, k, v, pos_col, pos_k3)


def _b_attn_kernel(cnt_ref, list_ref, q_ref, k_ref, v_ref, pq_ref, pk_ref, slope_ref, shift_ref,
                   lq1_ref, lk1_ref, lq2_ref, lk2_ref, sub_ref, o_ref,
                   p_sc, acc_sc, l_sc, m_sc, alpha_sc, *, tk, nk, lam_init, online):
    qi = pl.program_id(1)
    q = q_ref[...]
    pq = pq_ref[...]
    qchunk = jnp.right_shift(pq, CHUNK_SHIFT)
    neg_slope = slope_ref[0, :, 0:1]
    shift = shift_ref[0:1, 0:1]
    l_sc[...] = jnp.zeros_like(l_sc)
    acc_sc[...] = jnp.zeros_like(acc_sc)
    if online:
        m_sc[...] = jnp.full_like(m_sc, NEG_BIG)

    def produce(ki, slot):
        k = k_ref[pl.ds(pl.multiple_of(ki * tk, tk), tk), :]
        pk = pk_ref[ki]
        bias = jnp.abs(pq - pk).astype(F32) * neg_slope - shift
        bias = jnp.where(jnp.right_shift(pk, CHUNK_SHIFT) <= qchunk, bias, NEG_BIG)
        for c in range(2):
            s = lax.dot_general(q[:, c * B_HD:(c + 1) * B_HD], k[:, c * B_HD:(c + 1) * B_HD],
                                _NT, preferred_element_type=F32) + bias
            if online:
                m_prev = m_sc[c]
                m_new = jnp.maximum(m_prev, jnp.max(s, axis=1, keepdims=True))
                alpha = jnp.exp2(m_prev - m_new)
                alpha_sc[slot, c] = alpha
                m_sc[c] = m_new
                p = jnp.exp2(s - m_new)
                l_sc[c] = alpha * l_sc[c] + _lane_group_sum(p)
            else:
                p = jnp.exp2(s)
                l_sc[c] += _lane_group_sum(p)
            p_sc[slot, c] = p.astype(BF16)

    def consume(ki, slot):
        v = v_ref[pl.ds(pl.multiple_of(ki * tk, tk), tk), :]
        for c in range(2):
            pv = jnp.dot(p_sc[slot, c], v, preferred_element_type=F32)
            if online:
                acc_sc[c] = alpha_sc[slot, c] * acc_sc[c] + pv
            else:
                acc_sc[c] += pv

    _pipelined_tiles(cnt_ref[qi], lambda t: list_ref[qi * nk + t], produce, consume)

    lam = (jnp.exp(jnp.sum(lq1_ref[...] * lk1_ref[...], axis=-1, keepdims=True))
           - jnp.exp(jnp.sum(lq2_ref[...] * lk2_ref[...], axis=-1, keepdims=True)) + lam_init)
    l0 = jnp.sum(l_sc[0], axis=-1, keepdims=True)
    l1 = jnp.sum(l_sc[1], axis=-1, keepdims=True)
    od = acc_sc[0] / l0 - lam * (acc_sc[1] / l1)
    o_ref[...] = (od * _rms(od, B_V) * (sub_ref[...] * (1.0 - lam_init))).astype(BF16)


def _b_attn(q, k, v, pos_col, pos_k3, consts, tables, tq, tk, lam_init, online):
    s = q.shape[0]
    nq, nk = s // tq, s // tk
    slopes = consts[0]
    grid_spec = pltpu.PrefetchScalarGridSpec(
        num_scalar_prefetch=2,
        grid=(B_HEADS, nq),
        in_specs=[pl.BlockSpec((tq, B_V), lambda h, i, *_: (i, h)),
                  pl.BlockSpec((s, B_V), lambda h, i, *_: (0, h)),
                  pl.BlockSpec((s, B_V), lambda h, i, *_: (0, h)),
                  pl.BlockSpec((tq, 1), lambda h, i, *_: (i, 0)),
                  pl.BlockSpec((nk, 1, tk), lambda h, i, *_: (0, 0, 0)),
                  pl.BlockSpec((1,) + slopes.shape[1:], lambda h, i, *_: (h, 0, 0))]
                 + [pl.BlockSpec(c.shape, lambda h, i, *_: (0, 0)) for c in consts[1:]],
        out_specs=pl.BlockSpec((tq, B_V), lambda h, i, *_: (i, h)),
        scratch_shapes=[pltpu.VMEM((2, 2, tq, tk), BF16), pltpu.VMEM((2, tq, B_V), F32),
                        pltpu.VMEM((2, tq, LANES), F32), pltpu.VMEM((2, tq, 1), F32),
                        pltpu.VMEM((2, 2, tq, 1), F32)],
    )
    return pl.pallas_call(
        functools.partial(_b_attn_kernel, tk=tk, nk=nk, lam_init=lam_init, online=online),
        grid_spec=grid_spec,
        out_shape=jax.ShapeDtypeStruct((s, B_WIDTH), BF16),
        compiler_params=pltpu.CompilerParams(dimension_semantics=("arbitrary", "arbitrary"),
                                             vmem_limit_bytes=56 * 1024 * 1024),
        name="b_attention_online" if online else "b_attention",
    )(*tables, q, k, v, pos_col, pos_k3, *consts)


def _attend(fn, shift_bound, *args, **kwargs):
    return lax.cond(shift_bound <= MAX_CONST_SHIFT,
                    lambda: fn(*args, online=False, **kwargs),
                    lambda: fn(*args, online=True, **kwargs))


def _lane_onehot(value, lane):
    return jnp.zeros((1, LANES), F32).at[0, lane].set(value)


def _a_consts(norm_g, w_in, q_norm_g, w_q_up, kv_norm_g, w_kv_up, q_gain, k_gain):
    n_lat = A_Q_LORA + A_KV_LORA + A_ROPE
    wa = jnp.pad(w_in[:, :n_lat], ((0, 0), (0, LANES - A_ROPE))).astype(BF16)
    wg = w_in[:, n_lat:].astype(BF16)
    wq = w_q_up.reshape(A_Q_LORA, A_HEADS, A_QK)
    wq = jnp.pad(wq, ((0, 0), (0, 0), (0, A_HEAD_PAD - A_QK)))
    wq = wq.reshape(A_Q_LORA, A_HEADS * A_HEAD_PAD).astype(BF16)
    wkv = w_kv_up.reshape(A_KV_LORA, A_HEADS, A_NOPE + A_V)
    wkv = jnp.concatenate([wkv[:, :, :A_NOPE].reshape(A_KV_LORA, -1),
                           wkv[:, :, A_NOPE:].reshape(A_KV_LORA, -1)], axis=1).astype(BF16)
    q_scale = LOG2E / math.sqrt(A_QK)
    gq = jnp.pad(q_gain * q_scale, (0, A_HEAD_PAD - A_QK)).reshape(1, A_HEAD_PAD)
    gkn = k_gain[:A_NOPE].reshape(1, LANES)
    gkp = jnp.pad(k_gain[A_NOPE:], (0, LANES - A_ROPE)).reshape(1, LANES)
    half = A_ROPE // 2
    inv_freq = ROPE_THETA ** (-jnp.arange(0, A_ROPE, 2, dtype=F32) / A_ROPE)
    zeros = jnp.zeros((LANES - A_ROPE,), F32)
    invf = jnp.concatenate([inv_freq, inv_freq, zeros]).reshape(1, LANES)
    sgn = jnp.concatenate([-jnp.ones((half,), F32), jnp.ones((half,), F32), zeros]).reshape(1, LANES)
    bound = (BOUND_MARGIN * A_QK * q_scale
             * jnp.max(jnp.abs(q_gain)) * jnp.max(jnp.abs(k_gain)))
    bound = bound.astype(BF16).astype(F32)
    qshift = _lane_onehot(-bound, A_ROPE)
    one = _lane_onehot(1.0, A_ROPE)
    consts = (norm_g.reshape(1, -1), wa, wg, q_norm_g.reshape(1, -1), wq, kv_norm_g.reshape(1, -1),
              wkv, gq, gkn, gkp, invf, sgn, qshift, one)
    return consts, bound


def kernel(x, positions, a_norm, a_w_in, a_q_norm, a_w_q_up, a_kv_norm, a_w_kv_up, a_q_gain, a_k_gain, a_w_out, b_norm, b_w_in, b_q_gain, b_k_gain, b_lambda_q1, b_lambda_k1, b_lambda_q2, b_lambda_k2, b_subln, b_w_out):
    batch, seq, _ = x.shape
    assert batch == 1
    xs = x[0]
    pos = positions[0]
    pos_col = pos.reshape(seq, 1)

    tq_a, tk_a = 512, 1024
    tq_b, tk_b = 512, 512

    consts, a_bound = _a_consts(a_norm[0], a_w_in[0], a_q_norm[0], a_w_q_up[0], a_kv_norm[0],
                                a_w_kv_up[0], a_q_gain[0], a_k_gain[0])
    q, k, v, sg = _a_prologue(xs, pos_col, consts, tm=256)
    tables = _tile_tables(pos, tq_a, tk_a)
    o = _attend(_a_attn, a_bound, q, k, v, pos_col, pos.reshape(seq // tk_a, 1, tk_a), tables,
                tq_a, tk_a)
    xs = _out_proj(xs, o, sg, a_w_out[0].astype(BF16), tm=512, name="a_out_proj")

    layer_idx = 1
    lam_init = 0.8 - 0.6 * math.exp(-0.3 * layer_idx)
    b_scale = LOG2E / math.sqrt(B_HD)
    q, k, v, sg = _b_prologue(xs, b_norm[0].reshape(1, -1), b_w_in[0].astype(BF16),
                              (b_q_gain[0] * b_scale).reshape(1, -1), b_k_gain[0].reshape(1, -1),
                              tm=512)
    slopes = 2.0 ** (-8.0 * jnp.arange(1, B_HEADS + 1, dtype=F32) / B_HEADS)
    neg_slopes = jnp.broadcast_to((-LOG2E * slopes)[:, None, None], (B_HEADS, 1, LANES))
    b_bound = (BOUND_MARGIN * B_HD * b_scale
               * jnp.max(jnp.abs(b_q_gain[0])) * jnp.max(jnp.abs(b_k_gain[0])))
    row = lambda a: a.reshape(1, -1)
    b_consts = (neg_slopes, jnp.broadcast_to(b_bound.reshape(1, 1), (1, LANES)),
                row(b_lambda_q1[0]), row(b_lambda_k1[0]), row(b_lambda_q2[0]), row(b_lambda_k2[0]),
                row(b_subln[0]))
    tables = _tile_tables(pos, tq_b, tk_b)
    o = _attend(_b_attn, b_bound, q, k, v, pos_col, pos.reshape(seq // tk_b, 1, tk_b), b_consts,
                tables, tq_b, tk_b, lam_init)
    xs = _out_proj(xs, o, sg, b_w_out[0].astype(BF16), tm=512, name="b_out_proj")
    return xs[None]
```

```python
import functools
import math

import jax
import jax.numpy as jnp
from jax import lax
from jax.experimental import pallas as pl
from jax.experimental.pallas import tpu as pltpu

D_MODEL = 2048
CHUNK_SHIFT = 6
EPS = 1e-6
LOG2E = 1.4426950408889634

A_HEADS = 16
A_NOPE = 128
A_ROPE = 64
A_QK = A_NOPE + A_ROPE
A_V = 128
A_Q_LORA = 512
A_KV_LORA = 512
A_WIDTH = A_HEADS * A_V
A_HEAD_PAD = 256
A_V_PAD = 256
A_ONE_COL = A_V + A_ROPE
ROPE_THETA = 10000.0

B_HEADS = 8
B_HD = 128
B_V = 2 * B_HD
B_WIDTH = B_HEADS * B_V
B_QK = B_HEADS * 2 * B_HD
B_HEAD_PAD = 3 * B_HD
POS_SPLIT_SHIFT = 7

LANES = 128
NEG_BIG = -1e30
MAX_CONST_SHIFT = 50.0
BOUND_MARGIN = 1.01

F32 = jnp.float32
BF16 = jnp.bfloat16
_NT = (((1,), (1,)), ((), ()))


def _const_spec(shape):
    nd = len(shape)
    return pl.BlockSpec(shape, lambda *_: (0,) * nd, pipeline_mode=pl.Buffered(1))


def _rms(x, denom):
    return lax.rsqrt(jnp.sum(x * x, axis=-1, keepdims=True) * (1.0 / denom) + EPS)


def _a_prologue_kernel(x_ref, pos_ref, ng_ref, wa_ref, wg_ref, qng_ref, wq_ref, kvng_ref, wkv_ref,
                       gq_ref, gkn_ref, gkp_ref, invf_ref, sgn_ref, qshift_ref, one_ref,
                       q_ref, k_ref, v_ref, sg_ref):
    x = x_ref[...]
    xn = (x * _rms(x, D_MODEL) * ng_ref[...]).astype(BF16)
    za = jnp.dot(xn, wa_ref[...], preferred_element_type=F32)
    gate = jnp.dot(xn, wg_ref[...], preferred_element_type=F32)
    sg_ref[...] = (gate * jax.nn.sigmoid(gate)).astype(BF16)

    cq = za[:, :A_Q_LORA]
    ckv = za[:, A_Q_LORA:A_Q_LORA + A_KV_LORA]
    kpe = za[:, A_Q_LORA + A_KV_LORA:]
    cqn = (cq * _rms(cq, A_Q_LORA) * qng_ref[...]).astype(BF16)
    ckvn = (ckv * _rms(ckv, A_KV_LORA) * kvng_ref[...]).astype(BF16)
    q = jnp.dot(cqn, wq_ref[...], preferred_element_type=F32)
    kv = jnp.dot(ckvn, wkv_ref[...], preferred_element_type=F32)

    ang = pos_ref[...].astype(F32) * invf_ref[...]
    cos = jnp.cos(ang)
    sin_signed = jnp.sin(ang) * sgn_ref[...]
    lane = lax.broadcasted_iota(jnp.int32, cos.shape, 1)
    first_half = lane < (A_ROPE // 2)

    def rope(t):
        swapped = jnp.where(first_half, pltpu.roll(t, LANES - A_ROPE // 2, 1),
                            pltpu.roll(t, A_ROPE // 2, 1))
        return t * cos + swapped * sin_signed

    gq = gq_ref[...]
    gkn = gkn_ref[...]
    qshift = qshift_ref[...]
    one = one_ref[...]
    v_one = jnp.broadcast_to(one, (x.shape[0], LANES)).astype(BF16)
    kpe_ss = jnp.sum(kpe * kpe, axis=-1, keepdims=True)
    kpe_roped = rope(kpe * gkp_ref[...])
    for h in range(A_HEADS):
        qh = q[:, h * A_HEAD_PAD:(h + 1) * A_HEAD_PAD]
        qn = qh * _rms(qh, A_QK) * gq
        q_ref[:, h * A_HEAD_PAD:h * A_HEAD_PAD + LANES] = qn[:, :LANES].astype(BF16)
        q_ref[:, h * A_HEAD_PAD + LANES:(h + 1) * A_HEAD_PAD] = (
            rope(qn[:, LANES:]) + qshift).astype(BF16)

        kn = kv[:, h * A_NOPE:(h + 1) * A_NOPE]
        r = lax.rsqrt((jnp.sum(kn * kn, axis=-1, keepdims=True) + kpe_ss) * (1.0 / A_QK) + EPS)
        k_ref[:, h * A_HEAD_PAD:h * A_HEAD_PAD + LANES] = (kn * r * gkn).astype(BF16)
        k_ref[:, h * A_HEAD_PAD + LANES:(h + 1) * A_HEAD_PAD] = (kpe_roped * r + one).astype(BF16)

        v_ref[:, h * A_V_PAD:h * A_V_PAD + A_V] = (
            kv[:, A_HEADS * A_NOPE + h * A_V:A_HEADS * A_NOPE + (h + 1) * A_V].astype(BF16))
        v_ref[:, h * A_V_PAD + A_V:(h + 1) * A_V_PAD] = v_one


def _a_prologue(x, pos_col, consts, tm):
    s = x.shape[0]
    row = lambda w: pl.BlockSpec((tm, w), lambda i: (i, 0))
    return pl.pallas_call(
        _a_prologue_kernel,
        grid=(s // tm,),
        in_specs=[row(D_MODEL), row(1)] + [_const_spec(c.shape) for c in consts],
        out_specs=[row(A_HEADS * A_HEAD_PAD), row(A_HEADS * A_HEAD_PAD), row(A_HEADS * A_V_PAD),
                   row(A_WIDTH)],
        out_shape=[jax.ShapeDtypeStruct((s, A_HEADS * A_HEAD_PAD), BF16),
                   jax.ShapeDtypeStruct((s, A_HEADS * A_HEAD_PAD), BF16),
                   jax.ShapeDtypeStruct((s, A_HEADS * A_V_PAD), BF16),
                   jax.ShapeDtypeStruct((s, A_WIDTH), BF16)],
        compiler_params=pltpu.CompilerParams(dimension_semantics=("arbitrary",),
                                             vmem_limit_bytes=58 * 1024 * 1024),
        name="a_prologue",
    )(x, pos_col, *consts)


def _b_qk_kernel(x_ref, pos_ref, ng_ref, w_ref, g_ref, extc_ref, ea_ref, eb_ref, out_ref):
    x = x_ref[...]
    xn = (x * _rms(x, D_MODEL) * ng_ref[...]).astype(BF16)
    z = jnp.dot(xn, w_ref[...], preferred_element_type=F32)
    pos = pos_ref[...]
    a = jnp.right_shift(pos, POS_SPLIT_SHIFT).astype(F32)
    b = jnp.bitwise_and(pos, (1 << POS_SPLIT_SHIFT) - 1).astype(F32)
    ext_ab = a * ea_ref[...] + b * eb_ref[...]
    g = g_ref[...]
    for h in range(B_HEADS):
        for c in range(2):
            t = z[:, (2 * h + c) * B_HD:(2 * h + c + 1) * B_HD]
            out_ref[:, h * B_HEAD_PAD + c * B_HD:h * B_HEAD_PAD + (c + 1) * B_HD] = (
                t * _rms(t, B_HD) * g).astype(BF16)
        out_ref[:, h * B_HEAD_PAD + 2 * B_HD:(h + 1) * B_HEAD_PAD] = (
            extc_ref[h:h + 1, :] + ext_ab).astype(BF16)


def _b_qk_proj(x, pos_col, ng, w, g, extc, ea, eb, tm, name):
    s = x.shape[0]
    row = lambda wd: pl.BlockSpec((tm, wd), lambda i: (i, 0))
    consts = (ng, w, g, extc, ea, eb)
    return pl.pallas_call(
        _b_qk_kernel,
        grid=(s // tm,),
        in_specs=[row(D_MODEL), row(1)] + [_const_spec(c.shape) for c in consts],
        out_specs=row(B_HEADS * B_HEAD_PAD),
        out_shape=jax.ShapeDtypeStruct((s, B_HEADS * B_HEAD_PAD), BF16),
        compiler_params=pltpu.CompilerParams(dimension_semantics=("arbitrary",),
                                             vmem_limit_bytes=48 * 1024 * 1024),
        name=name,
    )(x, pos_col, *consts)


def _b_vg_kernel(x_ref, ng_ref, wv_ref, wg_ref, v_ref, sg_ref):
    x = x_ref[...]
    xn = (x * _rms(x, D_MODEL) * ng_ref[...]).astype(BF16)
    v_ref[...] = jnp.dot(xn, wv_ref[...], preferred_element_type=F32).astype(BF16)
    gate = jnp.dot(xn, wg_ref[...], preferred_element_type=F32)
    sg_ref[...] = (gate * jax.nn.sigmoid(gate)).astype(BF16)


def _b_vg_proj(x, ng, wv, wg, tm):
    s = x.shape[0]
    row = lambda: pl.BlockSpec((tm, B_WIDTH), lambda i: (i, 0))
    return pl.pallas_call(
        _b_vg_kernel,
        grid=(s // tm,),
        in_specs=[row(), _const_spec(ng.shape), _const_spec(wv.shape), _const_spec(wg.shape)],
        out_specs=[row(), row()],
        out_shape=[jax.ShapeDtypeStruct((s, B_WIDTH), BF16)] * 2,
        compiler_params=pltpu.CompilerParams(dimension_semantics=("arbitrary",),
                                             vmem_limit_bytes=56 * 1024 * 1024),
        name="b_vg_proj",
    )(x, ng, wv, wg)


def _out_proj_kernel(x_ref, o_ref, sg_ref, w_ref, y_ref):
    a = o_ref[...] * sg_ref[...]
    y_ref[...] = x_ref[...] + jnp.dot(a, w_ref[...], preferred_element_type=F32)


def _out_proj(x, o, sg, w, tm, name):
    s = x.shape[0]
    row = lambda: pl.BlockSpec((tm, D_MODEL), lambda i: (i, 0))
    return pl.pallas_call(
        _out_proj_kernel,
        grid=(s // tm,),
        in_specs=[row(), row(), row(), _const_spec(w.shape)],
        out_specs=row(),
        out_shape=jax.ShapeDtypeStruct((s, D_MODEL), F32),
        compiler_params=pltpu.CompilerParams(dimension_semantics=("arbitrary",),
                                             vmem_limit_bytes=48 * 1024 * 1024),
        name=name,
    )(x, o, sg, w)


def _tile_tables(pos, tq, tk):
    s = pos.shape[0]
    nq, nk = s // tq, s // tk
    ch = jnp.right_shift(pos, CHUNK_SHIFT)
    qmax_ch = ch.reshape(nq, tq).max(axis=1)
    kmin_ch = ch.reshape(nk, tk).min(axis=1)
    qmin = pos.reshape(nq, tq).min(axis=1)
    kmax = pos.reshape(nk, tk).max(axis=1)
    in_range = (pos.min() >= 0) & (pos.max() < (1 << (POS_SPLIT_SHIFT + 8)))
    visible = kmin_ch[None, :] <= qmax_ch[:, None]
    linear = visible & (kmax[None, :] <= qmin[:, None]) & in_range
    general = visible & jnp.logical_not(linear)
    rank = jnp.where(general, 0, jnp.where(linear, 1, 2))
    order = jnp.argsort(rank, axis=1, stable=True)
    return (general.sum(axis=1).astype(jnp.int32), visible.sum(axis=1).astype(jnp.int32),
            order.astype(jnp.int32).reshape(-1))


def _pipelined_tiles(n, tile_at, produce, consume, first_produced=False):
    if not first_produced:
        produce(tile_at(0), 0)

    def pair(u, c):
        t = 2 * u
        produce(tile_at(t + 1), 1)
        consume(tile_at(t), 0)
        produce(tile_at(t + 2), 0)
        consume(tile_at(t + 1), 1)
        return c

    lax.fori_loop(0, (n - 1) // 2, pair, 0)
    last = n - 1

    @pl.when(last % 2 == 1)
    def _():
        produce(tile_at(last), 1)
        consume(tile_at(last - 1), 0)
        consume(tile_at(last), 1)

    @pl.when(last % 2 == 0)
    def _():
        consume(tile_at(last), 0)


def _lane_group_sum(p):
    out = p[:, :LANES]
    for j in range(1, p.shape[1] // LANES):
        out = out + p[:, j * LANES:(j + 1) * LANES]
    return out


def _a_attn_kernel(cnt_ref, list_ref, q_ref, k_ref, v_ref, pq_ref, pk_ref, o_ref,
                   p_sc, acc_sc, m_sc, alpha_sc, *, tk, nk, online):
    qi = pl.program_id(1)
    q = q_ref[...]
    qchunk = jnp.right_shift(pq_ref[...], CHUNK_SHIFT)
    acc_sc[...] = jnp.zeros_like(acc_sc)
    if online:
        m_sc[...] = jnp.full_like(m_sc, NEG_BIG)

    def produce(ki, slot):
        k = k_ref[pl.ds(pl.multiple_of(ki * tk, tk), tk), :]
        s = lax.dot_general(q, k, _NT, preferred_element_type=F32)
        kchunk = jnp.right_shift(pk_ref[ki], CHUNK_SHIFT)
        s = jnp.where(kchunk <= qchunk, s, NEG_BIG)
        if online:
            m_prev = m_sc[...]
            m_new = jnp.maximum(m_prev, jnp.max(s, axis=1, keepdims=True))
            alpha_sc[slot] = jnp.exp2(m_prev - m_new)
            m_sc[...] = m_new
            s = s - m_new
        p_sc[slot] = jnp.exp2(s).astype(BF16)

    def consume(ki, slot):
        v = v_ref[pl.ds(pl.multiple_of(ki * tk, tk), tk), :]
        pv = jnp.dot(p_sc[slot], v, preferred_element_type=F32)
        if online:
            acc_sc[...] = alpha_sc[slot] * acc_sc[...] + pv
        else:
            acc_sc[...] += pv

    _pipelined_tiles(cnt_ref[qi], lambda t: list_ref[qi * nk + t], produce, consume)
    acc = acc_sc[...]
    o_ref[...] = (acc[:, :A_V] / acc[:, A_ONE_COL:A_ONE_COL + 1]).astype(BF16)


def _a_attn(q, k, v, pos_col, pos_k3, tables, tq, tk, online):
    s = q.shape[0]
    nq, nk = s // tq, s // tk
    grid_spec = pltpu.PrefetchScalarGridSpec(
        num_scalar_prefetch=2,
        grid=(A_HEADS, nq),
        in_specs=[pl.BlockSpec((tq, A_HEAD_PAD), lambda h, i, *_: (i, h)),
                  pl.BlockSpec((s, A_HEAD_PAD), lambda h, i, *_: (0, h)),
                  pl.BlockSpec((s, A_V_PAD), lambda h, i, *_: (0, h)),
                  pl.BlockSpec((tq, 1), lambda h, i, *_: (i, 0)),
                  pl.BlockSpec((nk, 1, tk), lambda h, i, *_: (0, 0, 0))],
        out_specs=pl.BlockSpec((tq, A_V), lambda h, i, *_: (i, h)),
        scratch_shapes=[pltpu.VMEM((2, tq, tk), BF16), pltpu.VMEM((tq, A_V_PAD), F32),
                        pltpu.VMEM((tq, 1), F32), pltpu.VMEM((2, tq, 1), F32)],
    )
    return pl.pallas_call(
        functools.partial(_a_attn_kernel, tk=tk, nk=nk, online=online),
        grid_spec=grid_spec,
        out_shape=jax.ShapeDtypeStruct((s, A_WIDTH), BF16),
        compiler_params=pltpu.CompilerParams(dimension_semantics=("arbitrary", "arbitrary"),
                                             vmem_limit_bytes=56 * 1024 * 1024),
        name="a_attention_online" if online else "a_attention",
    )(*tables, q, k, v, pos_col, pos_k3)


def _b_attn_kernel(gcnt_ref, cnt_ref, list_ref, q_ref, k_ref, v_ref, pq_ref, pk_ref, slope_ref,
                   shift_ref, lq1_ref, lk1_ref, lq2_ref, lk2_ref, sub_ref, o_ref,
                   p_sc, acc_sc, l_sc, m_sc, alpha_sc, *, tk, nk, lam_init, online):
    qi = pl.program_id(1)
    q = q_ref[...]
    q_ext = q[:, 2 * B_HD:]
    q_full = [jnp.concatenate([q[:, c * B_HD:(c + 1) * B_HD], q_ext], axis=1) for c in range(2)]
    pq = pq_ref[...]
    qchunk = jnp.right_shift(pq, CHUNK_SHIFT)
    neg_slope = slope_ref[0, :, 0:1]
    shift = shift_ref[0:1, 0:1]
    l_sc[...] = jnp.zeros_like(l_sc)
    acc_sc[...] = jnp.zeros_like(acc_sc)
    if online:
        m_sc[...] = jnp.full_like(m_sc, NEG_BIG)

    def finish(s, c, slot):
        if online:
            m_prev = m_sc[c]
            m_new = jnp.maximum(m_prev, jnp.max(s, axis=1, keepdims=True))
            alpha = jnp.exp2(m_prev - m_new)
            alpha_sc[slot, c] = alpha
            m_sc[c] = m_new
            p = jnp.exp2(s - m_new)
            l_sc[c] = alpha * l_sc[c] + _lane_group_sum(p)
        else:
            p = jnp.exp2(s)
            l_sc[c] += _lane_group_sum(p)
        p_sc[slot, c] = p.astype(BF16)

    def produce_linear(ki, slot):
        k = k_ref[pl.ds(pl.multiple_of(ki * tk, tk), tk), :]
        k_ext = k[:, 2 * B_HD:]
        for c in range(2):
            k_full = jnp.concatenate([k[:, c * B_HD:(c + 1) * B_HD], k_ext], axis=1)
            finish(lax.dot_general(q_full[c], k_full, _NT, preferred_element_type=F32), c, slot)

    def produce_general(ki, slot):
        k = k_ref[pl.ds(pl.multiple_of(ki * tk, tk), tk), :]
        pk = pk_ref[ki]
        bias = jnp.abs(pq - pk).astype(F32) * neg_slope - shift
        bias = jnp.where(jnp.right_shift(pk, CHUNK_SHIFT) <= qchunk, bias, NEG_BIG)
        for c in range(2):
            s = lax.dot_general(q[:, c * B_HD:(c + 1) * B_HD], k[:, c * B_HD:(c + 1) * B_HD],
                                _NT, preferred_element_type=F32) + bias
            finish(s, c, slot)

    def consume(ki, slot):
        v = v_ref[pl.ds(pl.multiple_of(ki * tk, tk), tk), :]
        for c in range(2):
            pv = jnp.dot(p_sc[slot, c], v, preferred_element_type=F32)
            if online:
                acc_sc[c] = alpha_sc[slot, c] * acc_sc[c] + pv
            else:
                acc_sc[c] += pv

    n_general = gcnt_ref[qi]
    tile_at = lambda t: list_ref[qi * nk + t]

    def general_body(t, c):
        produce_general(tile_at(t), 0)
        consume(tile_at(t), 0)
        return c

    first = jnp.maximum(n_general - 1, 0)
    lax.fori_loop(0, first, general_body, 0)

    @pl.when(n_general > 0)
    def _():
        produce_general(tile_at(first), 0)

    @pl.when(n_general == 0)
    def _():
        produce_linear(tile_at(first), 0)

    _pipelined_tiles(cnt_ref[qi] - first, lambda t: tile_at(first + t), produce_linear, consume,
                     first_produced=True)

    lam = (jnp.exp(jnp.sum(lq1_ref[...] * lk1_ref[...], axis=-1, keepdims=True))
           - jnp.exp(jnp.sum(lq2_ref[...] * lk2_ref[...], axis=-1, keepdims=True)) + lam_init)
    l0 = jnp.sum(l_sc[0], axis=-1, keepdims=True)
    l1 = jnp.sum(l_sc[1], axis=-1, keepdims=True)
    od = acc_sc[0] / l0 - lam * (acc_sc[1] / l1)
    o_ref[...] = (od * _rms(od, B_V) * (sub_ref[...] * (1.0 - lam_init))).astype(BF16)


def _b_attn(q, k, v, pos_col, pos_k3, consts, tables, tq, tk, lam_init, online):
    s = q.shape[0]
    nq, nk = s // tq, s // tk
    slopes = consts[0]
    grid_spec = pltpu.PrefetchScalarGridSpec(
        num_scalar_prefetch=3,
        grid=(B_HEADS, nq),
        in_specs=[pl.BlockSpec((tq, B_HEAD_PAD), lambda h, i, *_: (i, h)),
                  pl.BlockSpec((s, B_HEAD_PAD), lambda h, i, *_: (0, h),
                               pipeline_mode=pl.Buffered(1)),
                  pl.BlockSpec((s, B_V), lambda h, i, *_: (0, h)),
                  pl.BlockSpec((tq, 1), lambda h, i, *_: (i, 0)),
                  pl.BlockSpec((nk, 1, tk), lambda h, i, *_: (0, 0, 0)),
                  pl.BlockSpec((1,) + slopes.shape[1:], lambda h, i, *_: (h, 0, 0))]
                 + [pl.BlockSpec(c.shape, lambda h, i, *_: (0, 0)) for c in consts[1:]],
        out_specs=pl.BlockSpec((tq, B_V), lambda h, i, *_: (i, h)),
        scratch_shapes=[pltpu.VMEM((2, 2, tq, tk), BF16), pltpu.VMEM((2, tq, B_V), F32),
                        pltpu.VMEM((2, tq, LANES), F32), pltpu.VMEM((2, tq, 1), F32),
                        pltpu.VMEM((2, 2, tq, 1), F32)],
    )
    return pl.pallas_call(
        functools.partial(_b_attn_kernel, tk=tk, nk=nk, lam_init=lam_init, online=online),
        grid_spec=grid_spec,
        out_shape=jax.ShapeDtypeStruct((s, B_WIDTH), BF16),
        compiler_params=pltpu.CompilerParams(dimension_semantics=("arbitrary", "arbitrary"),
                                             vmem_limit_bytes=56 * 1024 * 1024),
        name="b_attention_online" if online else "b_attention",
    )(*tables, q, k, v, pos_col, pos_k3, *consts)


def _attend(fn, shift_bound, *args, **kwargs):
    return lax.cond(shift_bound <= MAX_CONST_SHIFT,
                    lambda: fn(*args, online=False, **kwargs),
                    lambda: fn(*args, online=True, **kwargs))


def _lane_onehot(value, lane):
    return jnp.zeros((1, LANES), F32).at[0, lane].set(value)


def _a_consts(norm_g, w_in, q_norm_g, w_q_up, kv_norm_g, w_kv_up, q_gain, k_gain):
    n_lat = A_Q_LORA + A_KV_LORA + A_ROPE
    wa = jnp.pad(w_in[:, :n_lat], ((0, 0), (0, LANES - A_ROPE))).astype(BF16)
    wg = w_in[:, n_lat:].astype(BF16)
    wq = w_q_up.reshape(A_Q_LORA, A_HEADS, A_QK)
    wq = jnp.pad(wq, ((0, 0), (0, 0), (0, A_HEAD_PAD - A_QK)))
    wq = wq.reshape(A_Q_LORA, A_HEADS * A_HEAD_PAD).astype(BF16)
    wkv = w_kv_up.reshape(A_KV_LORA, A_HEADS, A_NOPE + A_V)
    wkv = jnp.concatenate([wkv[:, :, :A_NOPE].reshape(A_KV_LORA, -1),
                           wkv[:, :, A_NOPE:].reshape(A_KV_LORA, -1)], axis=1).astype(BF16)
    q_scale = LOG2E / math.sqrt(A_QK)
    gq = jnp.pad(q_gain * q_scale, (0, A_HEAD_PAD - A_QK)).reshape(1, A_HEAD_PAD)
    gkn = k_gain[:A_NOPE].reshape(1, LANES)
    gkp = jnp.pad(k_gain[A_NOPE:], (0, LANES - A_ROPE)).reshape(1, LANES)
    half = A_ROPE // 2
    inv_freq = ROPE_THETA ** (-jnp.arange(0, A_ROPE, 2, dtype=F32) / A_ROPE)
    zeros = jnp.zeros((LANES - A_ROPE,), F32)
    invf = jnp.concatenate([inv_freq, inv_freq, zeros]).reshape(1, LANES)
    sgn = jnp.concatenate([-jnp.ones((half,), F32), jnp.ones((half,), F32), zeros]).reshape(1, LANES)
    bound = (BOUND_MARGIN * A_QK * q_scale
             * jnp.max(jnp.abs(q_gain)) * jnp.max(jnp.abs(k_gain)))
    bound = bound.astype(BF16).astype(F32)
    qshift = _lane_onehot(-bound, A_ROPE)
    one = _lane_onehot(1.0, A_ROPE)
    consts = (norm_g.reshape(1, -1), wa, wg, q_norm_g.reshape(1, -1), wq, kv_norm_g.reshape(1, -1),
              wkv, gq, gkn, gkp, invf, sgn, qshift, one)
    return consts, bound


def _b_consts(q_gain, k_gain):
    b_scale = LOG2E / math.sqrt(B_HD)
    bound = (BOUND_MARGIN * B_HD * b_scale * jnp.max(jnp.abs(q_gain)) * jnp.max(jnp.abs(k_gain)))
    bound = bound.astype(BF16).astype(F32)
    slopes = 2.0 ** (-8.0 * jnp.arange(1, B_HEADS + 1, dtype=F32) / B_HEADS)
    sigma = LOG2E * slopes
    sig_hi = sigma.astype(BF16).astype(F32)
    sig_lo = (sigma - sig_hi).astype(BF16).astype(F32)
    big = float(1 << POS_SPLIT_SHIFT)
    sig_cols = jnp.stack([sig_hi * big, sig_hi, sig_lo * big, sig_lo], axis=1)
    zeros = lambda n: jnp.zeros((B_HEADS, n), F32)
    extc_q = jnp.concatenate([jnp.broadcast_to(-bound, (B_HEADS, 1)), sig_cols, zeros(LANES - 5)],
                             axis=1)
    extc_k = jnp.concatenate([jnp.ones((B_HEADS, 1), F32), zeros(4), sig_cols, zeros(LANES - 9)],
                             axis=1)
    lanes = jnp.arange(LANES)
    pick = lambda ids, val: jnp.where(jnp.isin(lanes, jnp.array(ids)), val, 0.0).reshape(1, LANES)
    ea_q, eb_q = pick([5, 7], -1.0), pick([6, 8], -1.0)
    ea_k, eb_k = pick([1, 3], 1.0), pick([2, 4], 1.0)
    neg_slopes = jnp.broadcast_to(-(sig_hi + sig_lo)[:, None, None], (B_HEADS, 1, LANES))
    return bound, b_scale, neg_slopes, (extc_q, ea_q, eb_q), (extc_k, ea_k, eb_k)


def kernel(x, positions, a_norm, a_w_in, a_q_norm, a_w_q_up, a_kv_norm, a_w_kv_up, a_q_gain, a_k_gain, a_w_out, b_norm, b_w_in, b_q_gain, b_k_gain, b_lambda_q1, b_lambda_k1, b_lambda_q2, b_lambda_k2, b_subln, b_w_out):
    batch, seq, _ = x.shape
    assert batch == 1
    xs = x[0]
    pos = positions[0]
    pos_col = pos.reshape(seq, 1)
    row = lambda a: a.reshape(1, -1)

    tq_a, tk_a = 1024, 512
    tq_b, tk_b = 1024, 512

    consts, a_bound = _a_consts(a_norm[0], a_w_in[0], a_q_norm[0], a_w_q_up[0], a_kv_norm[0],
                                a_w_kv_up[0], a_q_gain[0], a_k_gain[0])
    q, k, v, sg = _a_prologue(xs, pos_col, consts, tm=256)
    _, cnt, order = _tile_tables(pos, tq_a, tk_a)
    o = _attend(_a_attn, a_bound, q, k, v, pos_col, pos.reshape(seq // tk_a, 1, tk_a),
                (cnt, order), tq_a, tk_a)
    xs = _out_proj(xs, o, sg, a_w_out[0].astype(BF16), tm=512, name="a_out_proj")

    layer_idx = 1
    lam_init = 0.8 - 0.6 * math.exp(-0.3 * layer_idx)
    b_bound, b_scale, neg_slopes, q_ext, k_ext = _b_consts(b_q_gain[0], b_k_gain[0])
    w_in = b_w_in[0].astype(BF16)
    ng = row(b_norm[0])
    q = _b_qk_proj(xs, pos_col, ng, w_in[:, :B_QK], row(b_q_gain[0] * b_scale), *q_ext,
                   tm=512, name="b_q_proj")
    k = _b_qk_proj(xs, pos_col, ng, w_in[:, B_QK:2 * B_QK], row(b_k_gain[0]), *k_ext,
                   tm=512, name="b_k_proj")
    v, sg = _b_vg_proj(xs, ng, w_in[:, 2 * B_QK:2 * B_QK + B_WIDTH], w_in[:, 2 * B_QK + B_WIDTH:],
                       tm=512)
    b_consts = (neg_slopes, jnp.broadcast_to(b_bound.reshape(1, 1), (1, LANES)),
                row(b_lambda_q1[0]), row(b_lambda_k1[0]), row(b_lambda_q2[0]), row(b_lambda_k2[0]),
                row(b_subln[0]))
    tables = _tile_tables(pos, tq_b, tk_b)
    o = _attend(_b_attn, b_bound, q, k, v, pos_col, pos.reshape(seq // tk_b, 1, tk_b), b_consts,
                tables, tq_b, tk_b, lam_init)
    xs = _out_proj(xs, o, sg, b_w_out[0].astype(BF16), tm=512, name="b_out_proj")
    return xs[None]
```

```python
import functools
import math

import jax
import jax.numpy as jnp
from jax import lax
from jax.experimental import pallas as pl
from jax.experimental.pallas import tpu as pltpu

D_MODEL = 2048
CHUNK_SHIFT = 6
EPS = 1e-6
LOG2E = 1.4426950408889634

A_HEADS = 16
A_NOPE = 128
A_ROPE = 64
A_QK = A_NOPE + A_ROPE
A_V = 128
A_Q_LORA = 512
A_KV_LORA = 512
A_WIDTH = A_HEADS * A_V
A_HEAD_PAD = 256
A_V_PAD = 256
A_ONE_COL = A_V + A_ROPE
ROPE_THETA = 10000.0

B_HEADS = 8
B_HD = 128
B_V = 2 * B_HD
B_WIDTH = B_HEADS * B_V
B_QK = B_HEADS * 2 * B_HD
B_HEAD_PAD = 3 * B_HD
POS_SPLIT_SHIFT = 7

LANES = 128
NEG_BIG = -1e30
MAX_CONST_SHIFT = 50.0
BOUND_MARGIN = 1.01

F32 = jnp.float32
BF16 = jnp.bfloat16
_NT = (((1,), (1,)), ((), ()))


def _const_spec(shape):
    nd = len(shape)
    return pl.BlockSpec(shape, lambda *_: (0,) * nd, pipeline_mode=pl.Buffered(1))


def _rms(x, denom):
    return lax.rsqrt(jnp.sum(x * x, axis=-1, keepdims=True) * (1.0 / denom) + EPS)


def _a_prologue_kernel(x_ref, pos_ref, ng_ref, wa_ref, wg_ref, qng_ref, wq_ref, kvng_ref, wkv_ref,
                       gq_ref, gkn_ref, gkp_ref, invf_ref, sgn_ref, qshift_ref, one_ref,
                       q_ref, k_ref, v_ref, sg_ref):
    x = x_ref[...]
    xn = (x * _rms(x, D_MODEL) * ng_ref[...]).astype(BF16)
    za = jnp.dot(xn, wa_ref[...], preferred_element_type=F32)
    gate = jnp.dot(xn, wg_ref[...], preferred_element_type=F32)
    sg_ref[...] = (gate * jax.nn.sigmoid(gate)).astype(BF16)

    cq = za[:, :A_Q_LORA]
    ckv = za[:, A_Q_LORA:A_Q_LORA + A_KV_LORA]
    kpe = za[:, A_Q_LORA + A_KV_LORA:]
    cqn = (cq * _rms(cq, A_Q_LORA) * qng_ref[...]).astype(BF16)
    ckvn = (ckv * _rms(ckv, A_KV_LORA) * kvng_ref[...]).astype(BF16)
    q = jnp.dot(cqn, wq_ref[...], preferred_element_type=F32)
    kv = jnp.dot(ckvn, wkv_ref[...], preferred_element_type=F32)

    ang = pos_ref[...].astype(F32) * invf_ref[...]
    cos = jnp.cos(ang)
    sin_signed = jnp.sin(ang) * sgn_ref[...]
    lane = lax.broadcasted_iota(jnp.int32, cos.shape, 1)
    first_half = lane < (A_ROPE // 2)

    def rope(t):
        swapped = jnp.where(first_half, pltpu.roll(t, LANES - A_ROPE // 2, 1),
                            pltpu.roll(t, A_ROPE // 2, 1))
        return t * cos + swapped * sin_signed

    gq = gq_ref[...]
    gkn = gkn_ref[...]
    qshift = qshift_ref[...]
    one = one_ref[...]
    v_one = jnp.broadcast_to(one, (x.shape[0], LANES)).astype(BF16)
    kpe_ss = jnp.sum(kpe * kpe, axis=-1, keepdims=True)
    kpe_roped = rope(kpe * gkp_ref[...])
    for h in range(A_HEADS):
        qh = q[:, h * A_HEAD_PAD:(h + 1) * A_HEAD_PAD]
        qn = qh * _rms(qh, A_QK) * gq
        q_ref[:, h * A_HEAD_PAD:h * A_HEAD_PAD + LANES] = qn[:, :LANES].astype(BF16)
        q_ref[:, h * A_HEAD_PAD + LANES:(h + 1) * A_HEAD_PAD] = (
            rope(qn[:, LANES:]) + qshift).astype(BF16)

        kn = kv[:, h * A_NOPE:(h + 1) * A_NOPE]
        r = lax.rsqrt((jnp.sum(kn * kn, axis=-1, keepdims=True) + kpe_ss) * (1.0 / A_QK) + EPS)
        k_ref[:, h * A_HEAD_PAD:h * A_HEAD_PAD + LANES] = (kn * r * gkn).astype(BF16)
        k_ref[:, h * A_HEAD_PAD + LANES:(h + 1) * A_HEAD_PAD] = (kpe_roped * r + one).astype(BF16)

        v_ref[:, h * A_V_PAD:h * A_V_PAD + A_V] = (
            kv[:, A_HEADS * A_NOPE + h * A_V:A_HEADS * A_NOPE + (h + 1) * A_V].astype(BF16))
        v_ref[:, h * A_V_PAD + A_V:(h + 1) * A_V_PAD] = v_one


def _a_prologue(x, pos_col, consts, tm):
    s = x.shape[0]
    row = lambda w: pl.BlockSpec((tm, w), lambda i: (i, 0))
    return pl.pallas_call(
        _a_prologue_kernel,
        grid=(s // tm,),
        in_specs=[row(D_MODEL), row(1)] + [_const_spec(c.shape) for c in consts],
        out_specs=[row(A_HEADS * A_HEAD_PAD), row(A_HEADS * A_HEAD_PAD), row(A_HEADS * A_V_PAD),
                   row(A_WIDTH)],
        out_shape=[jax.ShapeDtypeStruct((s, A_HEADS * A_HEAD_PAD), BF16),
                   jax.ShapeDtypeStruct((s, A_HEADS * A_HEAD_PAD), BF16),
                   jax.ShapeDtypeStruct((s, A_HEADS * A_V_PAD), BF16),
                   jax.ShapeDtypeStruct((s, A_WIDTH), BF16)],
        compiler_params=pltpu.CompilerParams(dimension_semantics=("arbitrary",),
                                             vmem_limit_bytes=58 * 1024 * 1024),
        name="a_prologue",
    )(x, pos_col, *consts)


def _b_qk_kernel(x_ref, pos_ref, ng_ref, w_ref, g_ref, extc_ref, ea_ref, eb_ref, out_ref):
    x = x_ref[...]
    xn = (x * _rms(x, D_MODEL) * ng_ref[...]).astype(BF16)
    z = jnp.dot(xn, w_ref[...], preferred_element_type=F32)
    pos = pos_ref[...]
    a = jnp.right_shift(pos, POS_SPLIT_SHIFT).astype(F32)
    b = jnp.bitwise_and(pos, (1 << POS_SPLIT_SHIFT) - 1).astype(F32)
    ext_ab = a * ea_ref[...] + b * eb_ref[...]
    g = g_ref[...]
    for h in range(B_HEADS):
        for c in range(2):
            t = z[:, (2 * h + c) * B_HD:(2 * h + c + 1) * B_HD]
            out_ref[:, h * B_HEAD_PAD + c * B_HD:h * B_HEAD_PAD + (c + 1) * B_HD] = (
                t * _rms(t, B_HD) * g).astype(BF16)
        out_ref[:, h * B_HEAD_PAD + 2 * B_HD:(h + 1) * B_HEAD_PAD] = (
            extc_ref[h:h + 1, :] + ext_ab).astype(BF16)


def _b_qk_proj(x, pos_col, ng, w, g, extc, ea, eb, tm, name):
    s = x.shape[0]
    row = lambda wd: pl.BlockSpec((tm, wd), lambda i: (i, 0))
    consts = (ng, w, g, extc, ea, eb)
    return pl.pallas_call(
        _b_qk_kernel,
        grid=(s // tm,),
        in_specs=[row(D_MODEL), row(1)] + [_const_spec(c.shape) for c in consts],
        out_specs=row(B_HEADS * B_HEAD_PAD),
        out_shape=jax.ShapeDtypeStruct((s, B_HEADS * B_HEAD_PAD), BF16),
        compiler_params=pltpu.CompilerParams(dimension_semantics=("arbitrary",),
                                             vmem_limit_bytes=48 * 1024 * 1024),
        name=name,
    )(x, pos_col, *consts)


def _b_vg_kernel(x_ref, ng_ref, wv_ref, wg_ref, v_ref, sg_ref):
    x = x_ref[...]
    xn = (x * _rms(x, D_MODEL) * ng_ref[...]).astype(BF16)
    v_ref[...] = jnp.dot(xn, wv_ref[...], preferred_element_type=F32).astype(BF16)
    gate = jnp.dot(xn, wg_ref[...], preferred_element_type=F32)
    sg_ref[...] = (gate * jax.nn.sigmoid(gate)).astype(BF16)


def _b_vg_proj(x, ng, wv, wg, tm):
    s = x.shape[0]
    row = lambda: pl.BlockSpec((tm, B_WIDTH), lambda i: (i, 0))
    return pl.pallas_call(
        _b_vg_kernel,
        grid=(s // tm,),
        in_specs=[row(), _const_spec(ng.shape), _const_spec(wv.shape), _const_spec(wg.shape)],
        out_specs=[row(), row()],
        out_shape=[jax.ShapeDtypeStruct((s, B_WIDTH), BF16)] * 2,
        compiler_params=pltpu.CompilerParams(dimension_semantics=("arbitrary",),
                                             vmem_limit_bytes=56 * 1024 * 1024),
        name="b_vg_proj",
    )(x, ng, wv, wg)


def _out_proj_kernel(x_ref, o_ref, sg_ref, w_ref, y_ref):
    a = o_ref[...] * sg_ref[...]
    y_ref[...] = x_ref[...] + jnp.dot(a, w_ref[...], preferred_element_type=F32)


def _out_proj(x, o, sg, w, tm, name):
    s = x.shape[0]
    row = lambda: pl.BlockSpec((tm, D_MODEL), lambda i: (i, 0))
    return pl.pallas_call(
        _out_proj_kernel,
        grid=(s // tm,),
        in_specs=[row(), row(), row(), _const_spec(w.shape)],
        out_specs=row(),
        out_shape=jax.ShapeDtypeStruct((s, D_MODEL), F32),
        compiler_params=pltpu.CompilerParams(dimension_semantics=("arbitrary",),
                                             vmem_limit_bytes=48 * 1024 * 1024),
        name=name,
    )(x, o, sg, w)


def _tile_tables(pos, tq, tk):
    s = pos.shape[0]
    nq, nk = s // tq, s // tk
    ch = jnp.right_shift(pos, CHUNK_SHIFT)
    qmax_ch = ch.reshape(nq, tq).max(axis=1)
    kmin_ch = ch.reshape(nk, tk).min(axis=1)
    qmin = pos.reshape(nq, tq).min(axis=1)
    kmax = pos.reshape(nk, tk).max(axis=1)
    in_range = (pos.min() >= 0) & (pos.max() < (1 << (POS_SPLIT_SHIFT + 8)))
    visible = kmin_ch[None, :] <= qmax_ch[:, None]
    linear = visible & (kmax[None, :] <= qmin[:, None]) & in_range
    general = visible & jnp.logical_not(linear)
    rank = jnp.where(general, 0, jnp.where(linear, 1, 2))
    order = jnp.argsort(rank, axis=1, stable=True)
    return (general.sum(axis=1).astype(jnp.int32), visible.sum(axis=1).astype(jnp.int32),
            order.astype(jnp.int32).reshape(-1))


def _pipelined_tiles(n, tile_at, produce, consume, first_produced=False):
    if not first_produced:
        produce(tile_at(0), 0)

    def pair(u, c):
        t = 2 * u
        produce(tile_at(t + 1), 1)
        consume(tile_at(t), 0)
        produce(tile_at(t + 2), 0)
        consume(tile_at(t + 1), 1)
        return c

    lax.fori_loop(0, (n - 1) // 2, pair, 0)
    last = n - 1

    @pl.when(last % 2 == 1)
    def _():
        produce(tile_at(last), 1)
        consume(tile_at(last - 1), 0)
        consume(tile_at(last), 1)

    @pl.when(last % 2 == 0)
    def _():
        consume(tile_at(last), 0)


def _lane_group_sum(p):
    out = p[:, :LANES]
    for j in range(1, p.shape[1] // LANES):
        out = out + p[:, j * LANES:(j + 1) * LANES]
    return out


def _a_attn_kernel(cnt_ref, list_ref, q_ref, k_ref, v_ref, pq_ref, pk_ref, o_ref,
                   p_sc, acc_sc, m_sc, alpha_sc, *, tk, nk, online):
    qi = pl.program_id(1)
    q = q_ref[...]
    qchunk = jnp.right_shift(pq_ref[...], CHUNK_SHIFT)
    acc_sc[...] = jnp.zeros_like(acc_sc)
    if online:
        m_sc[...] = jnp.full_like(m_sc, NEG_BIG)

    def produce(ki, slot):
        k = k_ref[pl.ds(pl.multiple_of(ki * tk, tk), tk), :]
        s = lax.dot_general(q, k, _NT, preferred_element_type=F32)
        kchunk = jnp.right_shift(pk_ref[ki], CHUNK_SHIFT)
        s = jnp.where(kchunk <= qchunk, s, NEG_BIG)
        if online:
            m_prev = m_sc[...]
            m_new = jnp.maximum(m_prev, jnp.max(s, axis=1, keepdims=True))
            alpha_sc[slot] = jnp.exp2(m_prev - m_new)
            m_sc[...] = m_new
            s = s - m_new
        p_sc[slot] = jnp.exp2(s).astype(BF16)

    def consume(ki, slot):
        v = v_ref[pl.ds(pl.multiple_of(ki * tk, tk), tk), :]
        pv = jnp.dot(p_sc[slot], v, preferred_element_type=F32)
        if online:
            acc_sc[...] = alpha_sc[slot] * acc_sc[...] + pv
        else:
            acc_sc[...] += pv

    _pipelined_tiles(cnt_ref[qi], lambda t: list_ref[qi * nk + t], produce, consume)
    acc = acc_sc[...]
    o_ref[...] = (acc[:, :A_V] / acc[:, A_ONE_COL:A_ONE_COL + 1]).astype(BF16)


def _a_attn(q, k, v, pos_col, pos_k3, tables, tq, tk, online):
    s = q.shape[0]
    nq, nk = s // tq, s // tk
    grid_spec = pltpu.PrefetchScalarGridSpec(
        num_scalar_prefetch=2,
        grid=(A_HEADS, nq),
        in_specs=[pl.BlockSpec((tq, A_HEAD_PAD), lambda h, i, *_: (i, h)),
                  pl.BlockSpec((s, A_HEAD_PAD), lambda h, i, *_: (0, h)),
                  pl.BlockSpec((s, A_V_PAD), lambda h, i, *_: (0, h)),
                  pl.BlockSpec((tq, 1), lambda h, i, *_: (i, 0)),
                  pl.BlockSpec((nk, 1, tk), lambda h, i, *_: (0, 0, 0))],
        out_specs=pl.BlockSpec((tq, A_V), lambda h, i, *_: (i, h)),
        scratch_shapes=[pltpu.VMEM((2, tq, tk), BF16), pltpu.VMEM((tq, A_V_PAD), F32),
                        pltpu.VMEM((tq, 1), F32), pltpu.VMEM((2, tq, 1), F32)],
    )
    return pl.pallas_call(
        functools.partial(_a_attn_kernel, tk=tk, nk=nk, online=online),
        grid_spec=grid_spec,
        out_shape=jax.ShapeDtypeStruct((s, A_WIDTH), BF16),
        compiler_params=pltpu.CompilerParams(dimension_semantics=("arbitrary", "arbitrary"),
                                             vmem_limit_bytes=56 * 1024 * 1024),
        name="a_attention_online" if online else "a_attention",
    )(*tables, q, k, v, pos_col, pos_k3)


def _b_attn_kernel(gcnt_ref, cnt_ref, list_ref, q_ref, k_ref, v_ref, pq_ref, pk_ref, slope_ref,
                   shift_ref, lq1_ref, lk1_ref, lq2_ref, lk2_ref, sub_ref, o_ref,
                   p_sc, acc_sc, l_sc, m_sc, alpha_sc, *, tk, nk, lam_init, online):
    qi = pl.program_id(1)
    q = q_ref[...]
    q_ext = q[:, 2 * B_HD:]
    q_full = [jnp.concatenate([q[:, c * B_HD:(c + 1) * B_HD], q_ext], axis=1) for c in range(2)]
    pq = pq_ref[...]
    qchunk = jnp.right_shift(pq, CHUNK_SHIFT)
    neg_slope = slope_ref[0, :, 0:1]
    shift = shift_ref[0:1, 0:1]
    l_sc[...] = jnp.zeros_like(l_sc)
    acc_sc[...] = jnp.zeros_like(acc_sc)
    if online:
        m_sc[...] = jnp.full_like(m_sc, NEG_BIG)

    def finish(s, c, slot):
        if online:
            m_prev = m_sc[c]
            m_new = jnp.maximum(m_prev, jnp.max(s, axis=1, keepdims=True))
            alpha = jnp.exp2(m_prev - m_new)
            alpha_sc[slot, c] = alpha
            m_sc[c] = m_new
            p = jnp.exp2(s - m_new)
            l_sc[c] = alpha * l_sc[c] + _lane_group_sum(p)
        else:
            p = jnp.exp2(s)
            l_sc[c] += _lane_group_sum(p)
        p_sc[slot, c] = p.astype(BF16)

    def produce_linear(ki, slot):
        k = k_ref[pl.ds(pl.multiple_of(ki * tk, tk), tk), :]
        k_ext = k[:, 2 * B_HD:]
        for c in range(2):
            k_full = jnp.concatenate([k[:, c * B_HD:(c + 1) * B_HD], k_ext], axis=1)
            finish(lax.dot_general(q_full[c], k_full, _NT, preferred_element_type=F32), c, slot)

    def produce_general(ki, slot):
        k = k_ref[pl.ds(pl.multiple_of(ki * tk, tk), tk), :]
        pk = pk_ref[ki]
        bias = jnp.abs(pq - pk).astype(F32) * neg_slope - shift
        bias = jnp.where(jnp.right_shift(pk, CHUNK_SHIFT) <= qchunk, bias, NEG_BIG)
        for c in range(2):
            s = lax.dot_general(q[:, c * B_HD:(c + 1) * B_HD], k[:, c * B_HD:(c + 1) * B_HD],
                                _NT, preferred_element_type=F32) + bias
            finish(s, c, slot)

    def consume(ki, slot):
        v = v_ref[pl.ds(pl.multiple_of(ki * tk, tk), tk), :]
        for c in range(2):
            pv = jnp.dot(p_sc[slot, c], v, preferred_element_type=F32)
            if online:
                acc_sc[c] = alpha_sc[slot, c] * acc_sc[c] + pv
            else:
                acc_sc[c] += pv

    n_general = gcnt_ref[qi]
    tile_at = lambda t: list_ref[qi * nk + t]

    def general_body(t, c):
        produce_general(tile_at(t), 0)
        consume(tile_at(t), 0)
        return c

    first = jnp.maximum(n_general - 1, 0)
    lax.fori_loop(0, first, general_body, 0)

    @pl.when(n_general > 0)
    def _():
        produce_general(tile_at(first), 0)

    @pl.when(n_general == 0)
    def _():
        produce_linear(tile_at(first), 0)

    _pipelined_tiles(cnt_ref[qi] - first, lambda t: tile_at(first + t), produce_linear, consume,
                     first_produced=True)

    lam = (jnp.exp(jnp.sum(lq1_ref[...] * lk1_ref[...], axis=-1, keepdims=True))
           - jnp.exp(jnp.sum(lq2_ref[...] * lk2_ref[...], axis=-1, keepdims=True)) + lam_init)
    l0 = jnp.sum(l_sc[0], axis=-1, keepdims=True)
    l1 = jnp.sum(l_sc[1], axis=-1, keepdims=True)
    od = acc_sc[0] / l0 - lam * (acc_sc[1] / l1)
    o_ref[...] = (od * _rms(od, B_V) * (sub_ref[...] * (1.0 - lam_init))).astype(BF16)


def _b_attn(q, k, v, pos_col, pos_k3, consts, tables, tq, tk, lam_init, online):
    s = q.shape[0]
    nq, nk = s // tq, s // tk
    slopes = consts[0]
    grid_spec = pltpu.PrefetchScalarGridSpec(
        num_scalar_prefetch=3,
        grid=(B_HEADS, nq),
        in_specs=[pl.BlockSpec((tq, B_HEAD_PAD), lambda h, i, *_: (i, h)),
                  pl.BlockSpec((s, B_HEAD_PAD), lambda h, i, *_: (0, h),
                               pipeline_mode=pl.Buffered(1)),
                  pl.BlockSpec((s, B_V), lambda h, i, *_: (0, h), pipeline_mode=pl.Buffered(1)),
                  pl.BlockSpec((tq, 1), lambda h, i, *_: (i, 0)),
                  pl.BlockSpec((nk, 1, tk), lambda h, i, *_: (0, 0, 0)),
                  pl.BlockSpec((1,) + slopes.shape[1:], lambda h, i, *_: (h, 0, 0))]
                 + [pl.BlockSpec(c.shape, lambda h, i, *_: (0, 0)) for c in consts[1:]],
        out_specs=pl.BlockSpec((tq, B_V), lambda h, i, *_: (i, h)),
        scratch_shapes=[pltpu.VMEM((2, 2, tq, tk), BF16), pltpu.VMEM((2, tq, B_V), F32),
                        pltpu.VMEM((2, tq, LANES), F32), pltpu.VMEM((2, tq, 1), F32),
                        pltpu.VMEM((2, 2, tq, 1), F32)],
    )
    return pl.pallas_call(
        functools.partial(_b_attn_kernel, tk=tk, nk=nk, lam_init=lam_init, online=online),
        grid_spec=grid_spec,
        out_shape=jax.ShapeDtypeStruct((s, B_WIDTH), BF16),
        compiler_params=pltpu.CompilerParams(dimension_semantics=("arbitrary", "arbitrary"),
                                             vmem_limit_bytes=56 * 1024 * 1024),
        name="b_attention_online" if online else "b_attention",
    )(*tables, q, k, v, pos_col, pos_k3, *consts)


def _attend(fn, shift_bound, *args, **kwargs):
    return lax.cond(shift_bound <= MAX_CONST_SHIFT,
                    lambda: fn(*args, online=False, **kwargs),
                    lambda: fn(*args, online=True, **kwargs))


def _lane_onehot(value, lane):
    return jnp.zeros((1, LANES), F32).at[0, lane].set(value)


def _a_consts(norm_g, w_in, q_norm_g, w_q_up, kv_norm_g, w_kv_up, q_gain, k_gain):
    n_lat = A_Q_LORA + A_KV_LORA + A_ROPE
    wa = jnp.pad(w_in[:, :n_lat], ((0, 0), (0, LANES - A_ROPE))).astype(BF16)
    wg = w_in[:, n_lat:].astype(BF16)
    wq = w_q_up.reshape(A_Q_LORA, A_HEADS, A_QK)
    wq = jnp.pad(wq, ((0, 0), (0, 0), (0, A_HEAD_PAD - A_QK)))
    wq = wq.reshape(A_Q_LORA, A_HEADS * A_HEAD_PAD).astype(BF16)
    wkv = w_kv_up.reshape(A_KV_LORA, A_HEADS, A_NOPE + A_V)
    wkv = jnp.concatenate([wkv[:, :, :A_NOPE].reshape(A_KV_LORA, -1),
                           wkv[:, :, A_NOPE:].reshape(A_KV_LORA, -1)], axis=1).astype(BF16)
    q_scale = LOG2E / math.sqrt(A_QK)
    gq = jnp.pad(q_gain * q_scale, (0, A_HEAD_PAD - A_QK)).reshape(1, A_HEAD_PAD)
    gkn = k_gain[:A_NOPE].reshape(1, LANES)
    gkp = jnp.pad(k_gain[A_NOPE:], (0, LANES - A_ROPE)).reshape(1, LANES)
    half = A_ROPE // 2
    inv_freq = ROPE_THETA ** (-jnp.arange(0, A_ROPE, 2, dtype=F32) / A_ROPE)
    zeros = jnp.zeros((LANES - A_ROPE,), F32)
    invf = jnp.concatenate([inv_freq, inv_freq, zeros]).reshape(1, LANES)
    sgn = jnp.concatenate([-jnp.ones((half,), F32), jnp.ones((half,), F32), zeros]).reshape(1, LANES)
    bound = (BOUND_MARGIN * A_QK * q_scale
             * jnp.max(jnp.abs(q_gain)) * jnp.max(jnp.abs(k_gain)))
    bound = bound.astype(BF16).astype(F32)
    qshift = _lane_onehot(-bound, A_ROPE)
    one = _lane_onehot(1.0, A_ROPE)
    consts = (norm_g.reshape(1, -1), wa, wg, q_norm_g.reshape(1, -1), wq, kv_norm_g.reshape(1, -1),
              wkv, gq, gkn, gkp, invf, sgn, qshift, one)
    return consts, bound


def _b_consts(q_gain, k_gain):
    b_scale = LOG2E / math.sqrt(B_HD)
    bound = (BOUND_MARGIN * B_HD * b_scale * jnp.max(jnp.abs(q_gain)) * jnp.max(jnp.abs(k_gain)))
    bound = bound.astype(BF16).astype(F32)
    slopes = 2.0 ** (-8.0 * jnp.arange(1, B_HEADS + 1, dtype=F32) / B_HEADS)
    sigma = LOG2E * slopes
    sig_hi = sigma.astype(BF16).astype(F32)
    sig_lo = (sigma - sig_hi).astype(BF16).astype(F32)
    big = float(1 << POS_SPLIT_SHIFT)
    sig_cols = jnp.stack([sig_hi * big, sig_hi, sig_lo * big, sig_lo], axis=1)
    zeros = lambda n: jnp.zeros((B_HEADS, n), F32)
    extc_q = jnp.concatenate([jnp.broadcast_to(-bound, (B_HEADS, 1)), sig_cols, zeros(LANES - 5)],
                             axis=1)
    extc_k = jnp.concatenate([jnp.ones((B_HEADS, 1), F32), zeros(4), sig_cols, zeros(LANES - 9)],
                             axis=1)
    lanes = jnp.arange(LANES)
    pick = lambda ids, val: jnp.where(jnp.isin(lanes, jnp.array(ids)), val, 0.0).reshape(1, LANES)
    ea_q, eb_q = pick([5, 7], -1.0), pick([6, 8], -1.0)
    ea_k, eb_k = pick([1, 3], 1.0), pick([2, 4], 1.0)
    neg_slopes = jnp.broadcast_to(-(sig_hi + sig_lo)[:, None, None], (B_HEADS, 1, LANES))
    return bound, b_scale, neg_slopes, (extc_q, ea_q, eb_q), (extc_k, ea_k, eb_k)


def kernel(x, positions, a_norm, a_w_in, a_q_norm, a_w_q_up, a_kv_norm, a_w_kv_up, a_q_gain, a_k_gain, a_w_out, b_norm, b_w_in, b_q_gain, b_k_gain, b_lambda_q1, b_lambda_k1, b_lambda_q2, b_lambda_k2, b_subln, b_w_out):
    batch, seq, _ = x.shape
    assert batch == 1
    xs = x[0]
    pos = positions[0]
    pos_col = pos.reshape(seq, 1)
    row = lambda a: a.reshape(1, -1)

    tq_a, tk_a = 1024, 1024
    tq_b, tk_b = 1024, 1024

    consts, a_bound = _a_consts(a_norm[0], a_w_in[0], a_q_norm[0], a_w_q_up[0], a_kv_norm[0],
                                a_w_kv_up[0], a_q_gain[0], a_k_gain[0])
    q, k, v, sg = _a_prologue(xs, pos_col, consts, tm=256)
    _, cnt, order = _tile_tables(pos, tq_a, tk_a)
    o = _attend(_a_attn, a_bound, q, k, v, pos_col, pos.reshape(seq // tk_a, 1, tk_a),
                (cnt, order), tq_a, tk_a)
    xs = _out_proj(xs, o, sg, a_w_out[0].astype(BF16), tm=512, name="a_out_proj")

    layer_idx = 1
    lam_init = 0.8 - 0.6 * math.exp(-0.3 * layer_idx)
    b_bound, b_scale, neg_slopes, q_ext, k_ext = _b_consts(b_q_gain[0], b_k_gain[0])
    w_in = b_w_in[0].astype(BF16)
    ng = row(b_norm[0])
    q = _b_qk_proj(xs, pos_col, ng, w_in[:, :B_QK], row(b_q_gain[0] * b_scale), *q_ext,
                   tm=512, name="b_q_proj")
    k = _b_qk_proj(xs, pos_col, ng, w_in[:, B_QK:2 * B_QK], row(b_k_gain[0]), *k_ext,
                   tm=512, name="b_k_proj")
    v, sg = _b_vg_proj(xs, ng, w_in[:, 2 * B_QK:2 * B_QK + B_WIDTH], w_in[:, 2 * B_QK + B_WIDTH:],
                       tm=512)
    b_consts = (neg_slopes, jnp.broadcast_to(b_bound.reshape(1, 1), (1, LANES)),
                row(b_lambda_q1[0]), row(b_lambda_k1[0]), row(b_lambda_q2[0]), row(b_lambda_k2[0]),
                row(b_subln[0]))
    tables = _tile_tables(pos, tq_b, tk_b)
    o = _attend(_b_attn, b_bound, q, k, v, pos_col, pos.reshape(seq // tk_b, 1, tk_b), b_consts,
                tables, tq_b, tk_b, lam_init)
    xs = _out_proj(xs, o, sg, b_w_out[0].astype(BF16), tm=512, name="b_out_proj")
    return xs[None]
```

```python
import functools
import math

import jax
import jax.numpy as jnp
from jax import lax
from jax.experimental import pallas as pl
from jax.experimental.pallas import tpu as pltpu

D_MODEL = 2048
CHUNK_SHIFT = 6
EPS = 1e-6
LOG2E = 1.4426950408889634

A_HEADS = 16
A_NOPE = 128
A_ROPE = 64
A_QK = A_NOPE + A_ROPE
A_V = 128
A_Q_LORA = 512
A_KV_LORA = 512
A_WIDTH = A_HEADS * A_V
A_HEAD_PAD = 256
A_V_PAD = 256
A_ONE_COL = A_V + A_ROPE
ROPE_THETA = 10000.0

B_HEADS = 8
B_HD = 128
B_V = 2 * B_HD
B_WIDTH = B_HEADS * B_V
B_QK = B_HEADS * 2 * B_HD
B_HEAD_PAD = 3 * B_HD
POS_SPLIT_SHIFT = 7

LANES = 128
NEG_BIG = -1e30
MAX_CONST_SHIFT = 50.0
BOUND_MARGIN = 1.01

F32 = jnp.float32
BF16 = jnp.bfloat16
_NT = (((1,), (1,)), ((), ()))


def _const_spec(a, block_shape=None, block_index=None):
    shape = a.shape if block_shape is None else block_shape
    index = (0,) * len(shape) if block_index is None else block_index
    return pl.BlockSpec(shape, lambda *_: index, pipeline_mode=pl.Buffered(1))


def _rms(x, denom):
    return lax.rsqrt(jnp.sum(x * x, axis=-1, keepdims=True) * (1.0 / denom) + EPS)


def _a_prologue_kernel(x_ref, pos_ref, ng_ref, wa_ref, wg_ref, qng_ref, wq_ref, kvng_ref, wkv_ref,
                       gq_ref, gkn_ref, gkp_ref, invf_ref, sgn_ref, qshift_ref, one_ref,
                       q_ref, k_ref, v_ref, sg_ref):
    x = x_ref[...]
    xn = (x * _rms(x, D_MODEL) * ng_ref[...]).astype(BF16)
    za = jnp.dot(xn, wa_ref[...], preferred_element_type=F32)
    gate = jnp.dot(xn, wg_ref[...], preferred_element_type=F32)
    sg_ref[...] = (gate * jax.nn.sigmoid(gate)).astype(BF16)
    rope = _rope_fn(pos_ref, invf_ref, sgn_ref)
    _a_q_rows(za[:, :A_Q_LORA], rope, qng_ref, wq_ref, gq_ref, qshift_ref, q_ref)
    _a_kv_rows(za[:, A_Q_LORA:A_Q_LORA + A_KV_LORA], za[:, A_Q_LORA + A_KV_LORA:], rope,
               kvng_ref, wkv_ref, gkn_ref, gkp_ref, one_ref, k_ref, v_ref)


def _rope_fn(pos_ref, invf_ref, sgn_ref):
    ang = pos_ref[...].astype(F32) * invf_ref[...]
    cos = jnp.cos(ang)
    sin_signed = jnp.sin(ang) * sgn_ref[...]
    lane = lax.broadcasted_iota(jnp.int32, cos.shape, 1)
    first_half = lane < (A_ROPE // 2)

    def rope(t):
        swapped = jnp.where(first_half, pltpu.roll(t, LANES - A_ROPE // 2, 1),
                            pltpu.roll(t, A_ROPE // 2, 1))
        return t * cos + swapped * sin_signed

    return rope


def _a_q_rows(cq, rope, qng_ref, wq_ref, gq_ref, qshift_ref, q_ref):
    cqn = (cq * _rms(cq, A_Q_LORA) * qng_ref[...]).astype(BF16)
    q = jnp.dot(cqn, wq_ref[...], preferred_element_type=F32)
    gq = gq_ref[...]
    qshift = qshift_ref[...]
    for h in range(A_HEADS):
        qh = q[:, h * A_HEAD_PAD:(h + 1) * A_HEAD_PAD]
        qn = qh * _rms(qh, A_QK) * gq
        q_ref[:, h * A_HEAD_PAD:h * A_HEAD_PAD + LANES] = qn[:, :LANES].astype(BF16)
        q_ref[:, h * A_HEAD_PAD + LANES:(h + 1) * A_HEAD_PAD] = (
            rope(qn[:, LANES:]) + qshift).astype(BF16)


def _a_kv_rows(ckv, kpe, rope, kvng_ref, wkv_ref, gkn_ref, gkp_ref, one_ref, k_ref, v_ref):
    ckvn = (ckv * _rms(ckv, A_KV_LORA) * kvng_ref[...]).astype(BF16)
    kv = jnp.dot(ckvn, wkv_ref[...], preferred_element_type=F32)
    gkn = gkn_ref[...]
    one = one_ref[...]
    v_one = jnp.broadcast_to(one, kpe.shape).astype(BF16)
    kpe_ss = jnp.sum(kpe * kpe, axis=-1, keepdims=True)
    kpe_roped = rope(kpe * gkp_ref[...])
    for h in range(A_HEADS):
        kn = kv[:, h * A_NOPE:(h + 1) * A_NOPE]
        r = lax.rsqrt((jnp.sum(kn * kn, axis=-1, keepdims=True) + kpe_ss) * (1.0 / A_QK) + EPS)
        k_ref[:, h * A_HEAD_PAD:h * A_HEAD_PAD + LANES] = (kn * r * gkn).astype(BF16)
        k_ref[:, h * A_HEAD_PAD + LANES:(h + 1) * A_HEAD_PAD] = (kpe_roped * r + one).astype(BF16)

        v_ref[:, h * A_V_PAD:h * A_V_PAD + A_V] = (
            kv[:, A_HEADS * A_NOPE + h * A_V:A_HEADS * A_NOPE + (h + 1) * A_V].astype(BF16))
        v_ref[:, h * A_V_PAD + A_V:(h + 1) * A_V_PAD] = v_one


def _rows_call(kernel_fn, row_inputs, consts, out_widths, out_dtypes, tm, vmem_mib, name):
    s = row_inputs[0][0].shape[0]
    in_specs = [pl.BlockSpec((tm, w), lambda i, cb=cb: (i, cb)) for _, w, cb in row_inputs]
    in_specs += [_const_spec(*c) if isinstance(c, tuple) else _const_spec(c) for c in consts]
    out_specs = [pl.BlockSpec((tm, w), lambda i: (i, 0)) for w in out_widths]
    out_shape = [jax.ShapeDtypeStruct((s, w), dt) for w, dt in zip(out_widths, out_dtypes)]
    return pl.pallas_call(
        kernel_fn,
        grid=(s // tm,),
        in_specs=in_specs,
        out_specs=out_specs,
        out_shape=out_shape,
        compiler_params=pltpu.CompilerParams(dimension_semantics=("arbitrary",),
                                             vmem_limit_bytes=vmem_mib * 1024 * 1024),
        name=name,
    )(*[a for a, _, _ in row_inputs], *[c[0] if isinstance(c, tuple) else c for c in consts])


def _a_prologue(x, pos_col, consts):
    wide = A_HEADS * A_HEAD_PAD
    return _rows_call(_a_prologue_kernel, [(x, D_MODEL, 0), (pos_col, 1, 0)], consts,
                      (wide, wide, A_HEADS * A_V_PAD, A_WIDTH), (BF16, BF16, BF16, BF16),
                      tm=256, vmem_mib=58, name="a_prologue")


def _b_qk_kernel(x_ref, pos_ref, ng_ref, w_ref, g_ref, extc_ref, ea_ref, eb_ref, out_ref):
    x = x_ref[...]
    xn = (x * _rms(x, D_MODEL) * ng_ref[...]).astype(BF16)
    z = jnp.dot(xn, w_ref[...], preferred_element_type=F32)
    pos = pos_ref[...]
    a = jnp.right_shift(pos, POS_SPLIT_SHIFT).astype(F32)
    b = jnp.bitwise_and(pos, (1 << POS_SPLIT_SHIFT) - 1).astype(F32)
    ext_ab = a * ea_ref[...] + b * eb_ref[...]
    g = g_ref[...]
    for h in range(B_HEADS):
        for c in range(2):
            t = z[:, (2 * h + c) * B_HD:(2 * h + c + 1) * B_HD]
            out_ref[:, h * B_HEAD_PAD + c * B_HD:h * B_HEAD_PAD + (c + 1) * B_HD] = (
                t * _rms(t, B_HD) * g).astype(BF16)
        out_ref[:, h * B_HEAD_PAD + 2 * B_HD:(h + 1) * B_HEAD_PAD] = (
            extc_ref[h:h + 1, :] + ext_ab).astype(BF16)


def _b_vg_kernel(x_ref, ng_ref, wv_ref, wg_ref, v_ref, sg_ref):
    x = x_ref[...]
    xn = (x * _rms(x, D_MODEL) * ng_ref[...]).astype(BF16)
    v_ref[...] = jnp.dot(xn, wv_ref[...], preferred_element_type=F32).astype(BF16)
    gate = jnp.dot(xn, wg_ref[...], preferred_element_type=F32)
    sg_ref[...] = (gate * jax.nn.sigmoid(gate)).astype(BF16)


def _out_proj_kernel(x_ref, o_ref, sg_ref, w_ref, y_ref):
    a = o_ref[...] * sg_ref[...]
    y_ref[...] = x_ref[...] + jnp.dot(a, w_ref[...], preferred_element_type=F32)


def _out_proj(x, o, sg, w, name):
    (y,) = _rows_call(_out_proj_kernel, [(x, D_MODEL, 0), (o, D_MODEL, 0), (sg, D_MODEL, 0)], (w,),
                      (D_MODEL,), (F32,), tm=512, vmem_mib=48, name=name)
    return y


def _tile_tables(pos, tq, tk):
    s = pos.shape[0]
    nq, nk = s // tq, s // tk
    ch = jnp.right_shift(pos, CHUNK_SHIFT)
    qmax_ch = ch.reshape(nq, tq).max(axis=1)
    kmin_ch = ch.reshape(nk, tk).min(axis=1)
    qmin = pos.reshape(nq, tq).min(axis=1)
    kmax = pos.reshape(nk, tk).max(axis=1)
    in_range = (pos.min() >= 0) & (pos.max() < (1 << (POS_SPLIT_SHIFT + 8)))
    visible = kmin_ch[None, :] <= qmax_ch[:, None]
    linear = visible & (kmax[None, :] <= qmin[:, None]) & in_range
    general = visible & jnp.logical_not(linear)
    rank = jnp.where(general, 0, jnp.where(linear, 1, 2))
    order = jnp.argsort(rank, axis=1, stable=True)
    return (general.sum(axis=1).astype(jnp.int32), visible.sum(axis=1).astype(jnp.int32),
            order.astype(jnp.int32).reshape(-1))


def _pipelined_tiles(n, tile_at, produce, consume, first_produced=False):
    if not first_produced:
        produce(tile_at(0), 0)

    def pair(u, c):
        t = 2 * u
        produce(tile_at(t + 1), 1)
        consume(tile_at(t), 0)
        produce(tile_at(t + 2), 0)
        consume(tile_at(t + 1), 1)
        return c

    lax.fori_loop(0, (n - 1) // 2, pair, 0)
    last = n - 1

    @pl.when(last % 2 == 1)
    def _():
        produce(tile_at(last), 1)
        consume(tile_at(last - 1), 0)
        consume(tile_at(last), 1)

    @pl.when(last % 2 == 0)
    def _():
        consume(tile_at(last), 0)


def _lane_group_sum(p):
    out = p[:, :LANES]
    for j in range(1, p.shape[1] // LANES):
        out = out + p[:, j * LANES:(j + 1) * LANES]
    return out


def _a_attn_kernel(cnt_ref, list_ref, q_ref, k_ref, v_ref, pq_ref, pk_ref, o_ref,
                   p_sc, acc_sc, m_sc, alpha_sc, *, tk, nk, online):
    qi = pl.program_id(1)
    q = q_ref[...]
    qchunk = jnp.right_shift(pq_ref[...], CHUNK_SHIFT)
    acc_sc[...] = jnp.zeros_like(acc_sc)
    if online:
        m_sc[...] = jnp.full_like(m_sc, NEG_BIG)

    def produce(ki, slot):
        k = k_ref[pl.ds(pl.multiple_of(ki * tk, tk), tk), :]
        s = lax.dot_general(q, k, _NT, preferred_element_type=F32)
        kchunk = jnp.right_shift(pk_ref[ki], CHUNK_SHIFT)
        s = jnp.where(kchunk <= qchunk, s, NEG_BIG)
        if online:
            m_prev = m_sc[...]
            m_new = jnp.maximum(m_prev, jnp.max(s, axis=1, keepdims=True))
            alpha_sc[slot] = jnp.exp2(m_prev - m_new)
            m_sc[...] = m_new
            s = s - m_new
        p_sc[slot] = jnp.exp2(s).astype(BF16)

    def consume(ki, slot):
        v = v_ref[pl.ds(pl.multiple_of(ki * tk, tk), tk), :]
        pv = jnp.dot(p_sc[slot], v, preferred_element_type=F32)
        if online:
            acc_sc[...] = alpha_sc[slot] * acc_sc[...] + pv
        else:
            acc_sc[...] += pv

    _pipelined_tiles(cnt_ref[qi], lambda t: list_ref[qi * nk + t], produce, consume)
    acc = acc_sc[...]
    o_ref[...] = (acc[:, :A_V] / acc[:, A_ONE_COL:A_ONE_COL + 1]).astype(BF16)


def _a_attn(q, k, v, pos_col, pos_k3, tables, tq, tk, online):
    s = q.shape[0]
    nq, nk = s // tq, s // tk
    grid_spec = pltpu.PrefetchScalarGridSpec(
        num_scalar_prefetch=2,
        grid=(A_HEADS, nq),
        in_specs=[pl.BlockSpec((tq, A_HEAD_PAD), lambda h, i, *_: (i, h)),
                  pl.BlockSpec((s, A_HEAD_PAD), lambda h, i, *_: (0, h)),
                  pl.BlockSpec((s, A_V_PAD), lambda h, i, *_: (0, h)),
                  pl.BlockSpec((tq, 1), lambda h, i, *_: (i, 0)),
                  pl.BlockSpec((nk, 1, tk), lambda h, i, *_: (0, 0, 0))],
        out_specs=pl.BlockSpec((tq, A_V), lambda h, i, *_: (i, h)),
        scratch_shapes=[pltpu.VMEM((2, tq, tk), BF16), pltpu.VMEM((tq, A_V_PAD), F32),
                        pltpu.VMEM((tq, 1), F32), pltpu.VMEM((2, tq, 1), F32)],
    )
    return pl.pallas_call(
        functools.partial(_a_attn_kernel, tk=tk, nk=nk, online=online),
        grid_spec=grid_spec,
        out_shape=jax.ShapeDtypeStruct((s, A_WIDTH), BF16),
        compiler_params=pltpu.CompilerParams(dimension_semantics=("arbitrary", "arbitrary"),
                                             vmem_limit_bytes=56 * 1024 * 1024),
        name="a_attention_online" if online else "a_attention",
    )(*tables, q, k, v, pos_col, pos_k3)


def _b_attn_kernel(gcnt_ref, cnt_ref, list_ref, q_ref, k_ref, v_ref, pq_ref, pk_ref, slope_ref,
                   shift_ref, lq1_ref, lk1_ref, lq2_ref, lk2_ref, sub_ref, o_ref,
                   p_sc, acc_sc, l_sc, m_sc, alpha_sc, *, tk, nk, lam_init, online):
    qi = pl.program_id(1)
    q = q_ref[...]
    q_ext = q[:, 2 * B_HD:]
    q_full = [jnp.concatenate([q[:, c * B_HD:(c + 1) * B_HD], q_ext], axis=1) for c in range(2)]
    pq = pq_ref[...]
    qchunk = jnp.right_shift(pq, CHUNK_SHIFT)
    neg_slope = slope_ref[0, :, 0:1]
    shift = shift_ref[0:1, 0:1]
    l_sc[...] = jnp.zeros_like(l_sc)
    acc_sc[...] = jnp.zeros_like(acc_sc)
    if online:
        m_sc[...] = jnp.full_like(m_sc, NEG_BIG)

    def finish(s, c, slot):
        if online:
            m_prev = m_sc[c]
            m_new = jnp.maximum(m_prev, jnp.max(s, axis=1, keepdims=True))
            alpha = jnp.exp2(m_prev - m_new)
            alpha_sc[slot, c] = alpha
            m_sc[c] = m_new
            p = jnp.exp2(s - m_new)
            l_sc[c] = alpha * l_sc[c] + _lane_group_sum(p)
        else:
            p = jnp.exp2(s)
            l_sc[c] += _lane_group_sum(p)
        p_sc[slot, c] = p.astype(BF16)

    def produce_linear(ki, slot):
        k = k_ref[pl.ds(pl.multiple_of(ki * tk, tk), tk), :]
        k_ext = k[:, 2 * B_HD:]
        for c in range(2):
            k_full = jnp.concatenate([k[:, c * B_HD:(c + 1) * B_HD], k_ext], axis=1)
            finish(lax.dot_general(q_full[c], k_full, _NT, preferred_element_type=F32), c, slot)

    def produce_general(ki, slot):
        k = k_ref[pl.ds(pl.multiple_of(ki * tk, tk), tk), :]
        pk = pk_ref[ki]
        bias = jnp.abs(pq - pk).astype(F32) * neg_slope - shift
        bias = jnp.where(jnp.right_shift(pk, CHUNK_SHIFT) <= qchunk, bias, NEG_BIG)
        for c in range(2):
            s = lax.dot_general(q[:, c * B_HD:(c + 1) * B_HD], k[:, c * B_HD:(c + 1) * B_HD],
                                _NT, preferred_element_type=F32) + bias
            finish(s, c, slot)

    def consume(ki, slot):
        v = v_ref[pl.ds(pl.multiple_of(ki * tk, tk), tk), :]
        for c in range(2):
            pv = jnp.dot(p_sc[slot, c], v, preferred_element_type=F32)
            if online:
                acc_sc[c] = alpha_sc[slot, c] * acc_sc[c] + pv
            else:
                acc_sc[c] += pv

    n_general = gcnt_ref[qi]
    tile_at = lambda t: list_ref[qi * nk + t]

    def general_body(t, c):
        produce_general(tile_at(t), 0)
        consume(tile_at(t), 0)
        return c

    first = jnp.maximum(n_general - 1, 0)
    lax.fori_loop(0, first, general_body, 0)

    @pl.when(n_general > 0)
    def _():
        produce_general(tile_at(first), 0)

    @pl.when(n_general == 0)
    def _():
        produce_linear(tile_at(first), 0)

    _pipelined_tiles(cnt_ref[qi] - first, lambda t: tile_at(first + t), produce_linear, consume,
                     first_produced=True)

    lam = (jnp.exp(jnp.sum(lq1_ref[...] * lk1_ref[...], axis=-1, keepdims=True))
           - jnp.exp(jnp.sum(lq2_ref[...] * lk2_ref[...], axis=-1, keepdims=True)) + lam_init)
    l0 = jnp.sum(l_sc[0], axis=-1, keepdims=True)
    l1 = jnp.sum(l_sc[1], axis=-1, keepdims=True)
    od = acc_sc[0] / l0 - lam * (acc_sc[1] / l1)
    o_ref[...] = (od * _rms(od, B_V) * (sub_ref[...] * (1.0 - lam_init))).astype(BF16)


def _b_attn(q, k, v, pos_col, pos_k3, consts, tables, tq, tk, lam_init, online):
    s = q.shape[0]
    nq, nk = s // tq, s // tk
    slopes = consts[0]
    grid_spec = pltpu.PrefetchScalarGridSpec(
        num_scalar_prefetch=3,
        grid=(B_HEADS, nq),
        in_specs=[pl.BlockSpec((tq, B_HEAD_PAD), lambda h, i, *_: (i, h)),
                  pl.BlockSpec((s, B_HEAD_PAD), lambda h, i, *_: (0, h),
                               pipeline_mode=pl.Buffered(1)),
                  pl.BlockSpec((s, B_V), lambda h, i, *_: (0, h), pipeline_mode=pl.Buffered(1)),
                  pl.BlockSpec((tq, 1), lambda h, i, *_: (i, 0)),
                  pl.BlockSpec((nk, 1, tk), lambda h, i, *_: (0, 0, 0)),
                  pl.BlockSpec((1,) + slopes.shape[1:], lambda h, i, *_: (h, 0, 0))]
                 + [pl.BlockSpec(c.shape, lambda h, i, *_: (0, 0)) for c in consts[1:]],
        out_specs=pl.BlockSpec((tq, B_V), lambda h, i, *_: (i, h)),
        scratch_shapes=[pltpu.VMEM((2, 2, tq, tk), BF16), pltpu.VMEM((2, tq, B_V), F32),
                        pltpu.VMEM((2, tq, LANES), F32), pltpu.VMEM((2, tq, 1), F32),
                        pltpu.VMEM((2, 2, tq, 1), F32)],
    )
    return pl.pallas_call(
        functools.partial(_b_attn_kernel, tk=tk, nk=nk, lam_init=lam_init, online=online),
        grid_spec=grid_spec,
        out_shape=jax.ShapeDtypeStruct((s, B_WIDTH), BF16),
        compiler_params=pltpu.CompilerParams(dimension_semantics=("arbitrary", "arbitrary"),
                                             vmem_limit_bytes=56 * 1024 * 1024),
        name="b_attention_online" if online else "b_attention",
    )(*tables, q, k, v, pos_col, pos_k3, *consts)


def _attend(fn, shift_bound, *args, **kwargs):
    return lax.cond(shift_bound <= MAX_CONST_SHIFT,
                    lambda: fn(*args, online=False, **kwargs),
                    lambda: fn(*args, online=True, **kwargs))


def _lane_onehot(value, lane):
    return jnp.zeros((1, LANES), F32).at[0, lane].set(value)


def _a_consts(norm_g, w_in, q_norm_g, w_q_up, kv_norm_g, w_kv_up, q_gain, k_gain):
    n_lat = A_Q_LORA + A_KV_LORA + A_ROPE
    wa = jnp.pad(w_in[:, :n_lat], ((0, 0), (0, LANES - A_ROPE))).astype(BF16)
    wg = w_in[:, n_lat:].astype(BF16)
    wq = w_q_up.reshape(A_Q_LORA, A_HEADS, A_QK)
    wq = jnp.pad(wq, ((0, 0), (0, 0), (0, A_HEAD_PAD - A_QK)))
    wq = wq.reshape(A_Q_LORA, A_HEADS * A_HEAD_PAD).astype(BF16)
    wkv = w_kv_up.reshape(A_KV_LORA, A_HEADS, A_NOPE + A_V)
    wkv = jnp.concatenate([wkv[:, :, :A_NOPE].reshape(A_KV_LORA, -1),
                           wkv[:, :, A_NOPE:].reshape(A_KV_LORA, -1)], axis=1).astype(BF16)
    q_scale = LOG2E / math.sqrt(A_QK)
    gq = jnp.pad(q_gain * q_scale, (0, A_HEAD_PAD - A_QK)).reshape(1, A_HEAD_PAD)
    gkn = k_gain[:A_NOPE].reshape(1, LANES)
    gkp = jnp.pad(k_gain[A_NOPE:], (0, LANES - A_ROPE)).reshape(1, LANES)
    half = A_ROPE // 2
    inv_freq = ROPE_THETA ** (-jnp.arange(0, A_ROPE, 2, dtype=F32) / A_ROPE)
    zeros = jnp.zeros((LANES - A_ROPE,), F32)
    invf = jnp.concatenate([inv_freq, inv_freq, zeros]).reshape(1, LANES)
    sgn = jnp.concatenate([-jnp.ones((half,), F32), jnp.ones((half,), F32), zeros]).reshape(1, LANES)
    bound = (BOUND_MARGIN * A_QK * q_scale
             * jnp.max(jnp.abs(q_gain)) * jnp.max(jnp.abs(k_gain)))
    bound = bound.astype(BF16).astype(F32)
    qshift = _lane_onehot(-bound, A_ROPE)
    one = _lane_onehot(1.0, A_ROPE)
    consts = (norm_g.reshape(1, -1), wa, wg, q_norm_g.reshape(1, -1), wq, kv_norm_g.reshape(1, -1),
              wkv, gq, gkn, gkp, invf, sgn, qshift, one)
    return consts, bound


def _b_consts(q_gain, k_gain):
    b_scale = LOG2E / math.sqrt(B_HD)
    bound = (BOUND_MARGIN * B_HD * b_scale * jnp.max(jnp.abs(q_gain)) * jnp.max(jnp.abs(k_gain)))
    bound = bound.astype(BF16).astype(F32)
    slopes = 2.0 ** (-8.0 * jnp.arange(1, B_HEADS + 1, dtype=F32) / B_HEADS)
    sigma = LOG2E * slopes
    sig_hi = sigma.astype(BF16).astype(F32)
    sig_lo = (sigma - sig_hi).astype(BF16).astype(F32)
    big = float(1 << POS_SPLIT_SHIFT)
    sig_cols = jnp.stack([sig_hi * big, sig_hi, sig_lo * big, sig_lo], axis=1)
    zeros = lambda n: jnp.zeros((B_HEADS, n), F32)
    extc_q = jnp.concatenate([jnp.broadcast_to(-bound, (B_HEADS, 1)), sig_cols, zeros(LANES - 5)],
                             axis=1)
    extc_k = jnp.concatenate([jnp.ones((B_HEADS, 1), F32), zeros(4), sig_cols, zeros(LANES - 9)],
                             axis=1)
    lanes = jnp.arange(LANES)
    pick = lambda ids, val: jnp.where(jnp.isin(lanes, jnp.array(ids)), val, 0.0).reshape(1, LANES)
    ea_q, eb_q = pick([5, 7], -1.0), pick([6, 8], -1.0)
    ea_k, eb_k = pick([1, 3], 1.0), pick([2, 4], 1.0)
    neg_slopes = jnp.broadcast_to(-(sig_hi + sig_lo)[:, None, None], (B_HEADS, 1, LANES))
    return bound, b_scale, neg_slopes, (extc_q, ea_q, eb_q), (extc_k, ea_k, eb_k)


def kernel(x, positions, a_norm, a_w_in, a_q_norm, a_w_q_up, a_kv_norm, a_w_kv_up, a_q_gain, a_k_gain, a_w_out, b_norm, b_w_in, b_q_gain, b_k_gain, b_lambda_q1, b_lambda_k1, b_lambda_q2, b_lambda_k2, b_subln, b_w_out):
    batch, seq, _ = x.shape
    assert batch == 1
    xs = x[0]
    pos = positions[0]
    pos_col = pos.reshape(seq, 1)
    row = lambda a: a.reshape(1, -1)

    tq_a, tk_a = 1024, 1024
    tq_b, tk_b = 1024, 1024

    consts, a_bound = _a_consts(a_norm[0], a_w_in[0], a_q_norm[0], a_w_q_up[0], a_kv_norm[0],
                                a_w_kv_up[0], a_q_gain[0], a_k_gain[0])
    q, k, v, sg = _a_prologue(xs, pos_col, consts)
    _, cnt, order = _tile_tables(pos, tq_a, tk_a)
    o = _attend(_a_attn, a_bound, q, k, v, pos_col, pos.reshape(seq // tk_a, 1, tk_a),
                (cnt, order), tq_a, tk_a)
    xs = _out_proj(xs, o, sg, a_w_out[0].astype(BF16), name="a_out_proj")

    layer_idx = 1
    lam_init = 0.8 - 0.6 * math.exp(-0.3 * layer_idx)
    b_bound, b_scale, neg_slopes, q_ext, k_ext = _b_consts(b_q_gain[0], b_k_gain[0])
    w_in = b_w_in[0].astype(BF16)
    ng = row(b_norm[0])
    section = lambda j: (w_in, (D_MODEL, B_WIDTH), (0, j))
    rows = [(xs, D_MODEL, 0), (pos_col, 1, 0)]
    (q,) = _rows_call(_b_qk_kernel, rows, (ng, section(0), row(b_q_gain[0] * b_scale), *q_ext),
                      (B_HEADS * B_HEAD_PAD,), (BF16,), tm=512, vmem_mib=48, name="b_q_proj")
    (k,) = _rows_call(_b_qk_kernel, rows, (ng, section(1), row(b_k_gain[0]), *k_ext),
                      (B_HEADS * B_HEAD_PAD,), (BF16,), tm=512, vmem_mib=48, name="b_k_proj")
    v, sg = _rows_call(_b_vg_kernel, rows[:1], (ng, section(2), section(3)),
                       (B_WIDTH, B_WIDTH), (BF16, BF16), tm=512, vmem_mib=56, name="b_vg_proj")
    b_consts = (neg_slopes, jnp.broadcast_to(b_bound.reshape(1, 1), (1, LANES)),
                row(b_lambda_q1[0]), row(b_lambda_k1[0]), row(b_lambda_q2[0]), row(b_lambda_k2[0]),
                row(b_subln[0]))
    tables = _tile_tables(pos, tq_b, tk_b)
    o = _attend(_b_attn, b_bound, q, k, v, pos_col, pos.reshape(seq // tk_b, 1, tk_b), b_consts,
                tables, tq_b, tk_b, lam_init)
    xs = _out_proj(xs, o, sg, b_w_out[0].astype(BF16), name="b_out_proj")
    return xs[None]
```

```python
import functools
import math

import jax
import jax.numpy as jnp
from jax import lax
from jax.experimental import pallas as pl
from jax.experimental.pallas import tpu as pltpu

D_MODEL = 2048
CHUNK_SHIFT = 6
EPS = 1e-6
LOG2E = 1.4426950408889634

A_HEADS = 16
A_NOPE = 128
A_ROPE = 64
A_QK = A_NOPE + A_ROPE
A_V = 128
A_Q_LORA = 512
A_KV_LORA = 512
A_WIDTH = A_HEADS * A_V
A_HEAD_PAD = 256
A_V_PAD = 256
A_ONE_COL = A_V + A_ROPE
ROPE_THETA = 10000.0

B_HEADS = 8
B_HD = 128
B_V = 2 * B_HD
B_WIDTH = B_HEADS * B_V
B_QK = B_HEADS * 2 * B_HD
B_HEAD_PAD = 3 * B_HD
POS_SPLIT_SHIFT = 7

LANES = 128
NEG_BIG = -1e30
MAX_CONST_SHIFT = 50.0
BOUND_MARGIN = 1.01
ATTN_TILES = (1024, 1024)
ATTN_TILES_ONLINE = (512, 1024)

F32 = jnp.float32
BF16 = jnp.bfloat16
_NT = (((1,), (1,)), ((), ()))


def _const_spec(a, block_shape=None, block_index=None):
    shape = a.shape if block_shape is None else block_shape
    index = (0,) * len(shape) if block_index is None else block_index
    return pl.BlockSpec(shape, lambda *_: index, pipeline_mode=pl.Buffered(1))


def _rms(x, denom):
    return lax.rsqrt(jnp.sum(x * x, axis=-1, keepdims=True) * (1.0 / denom) + EPS)


def _a_prologue_kernel(x_ref, pos_ref, ng_ref, wa_ref, wg_ref, qng_ref, wq_ref, kvng_ref, wkv_ref,
                       gq_ref, gkn_ref, gkp_ref, invf_ref, sgn_ref, qshift_ref, one_ref,
                       q_ref, k_ref, v_ref, sg_ref):
    x = x_ref[...]
    xn = (x * _rms(x, D_MODEL) * ng_ref[...]).astype(BF16)
    za = jnp.dot(xn, wa_ref[...], preferred_element_type=F32)
    gate = jnp.dot(xn, wg_ref[...], preferred_element_type=F32)
    sg_ref[...] = (gate * jax.nn.sigmoid(gate)).astype(BF16)
    rope = _rope_fn(pos_ref, invf_ref, sgn_ref)
    _a_q_rows(za[:, :A_Q_LORA], rope, qng_ref, wq_ref, gq_ref, qshift_ref, q_ref)
    _a_kv_rows(za[:, A_Q_LORA:A_Q_LORA + A_KV_LORA], za[:, A_Q_LORA + A_KV_LORA:], rope,
               kvng_ref, wkv_ref, gkn_ref, gkp_ref, one_ref, k_ref, v_ref)


def _rope_fn(pos_ref, invf_ref, sgn_ref):
    ang = pos_ref[...].astype(F32) * invf_ref[...]
    cos = jnp.cos(ang)
    sin_signed = jnp.sin(ang) * sgn_ref[...]
    lane = lax.broadcasted_iota(jnp.int32, cos.shape, 1)
    first_half = lane < (A_ROPE // 2)

    def rope(t):
        swapped = jnp.where(first_half, pltpu.roll(t, LANES - A_ROPE // 2, 1),
                            pltpu.roll(t, A_ROPE // 2, 1))
        return t * cos + swapped * sin_signed

    return rope


def _a_q_rows(cq, rope, qng_ref, wq_ref, gq_ref, qshift_ref, q_ref):
    cqn = (cq * _rms(cq, A_Q_LORA) * qng_ref[...]).astype(BF16)
    q = jnp.dot(cqn, wq_ref[...], preferred_element_type=F32)
    gq = gq_ref[...]
    qshift = qshift_ref[...]
    for h in range(A_HEADS):
        qh = q[:, h * A_HEAD_PAD:(h + 1) * A_HEAD_PAD]
        qn = qh * _rms(qh, A_QK) * gq
        q_ref[:, h * A_HEAD_PAD:h * A_HEAD_PAD + LANES] = qn[:, :LANES].astype(BF16)
        q_ref[:, h * A_HEAD_PAD + LANES:(h + 1) * A_HEAD_PAD] = (
            rope(qn[:, LANES:]) + qshift).astype(BF16)


def _a_kv_rows(ckv, kpe, rope, kvng_ref, wkv_ref, gkn_ref, gkp_ref, one_ref, k_ref, v_ref):
    ckvn = (ckv * _rms(ckv, A_KV_LORA) * kvng_ref[...]).astype(BF16)
    kv = jnp.dot(ckvn, wkv_ref[...], preferred_element_type=F32)
    gkn = gkn_ref[...]
    one = one_ref[...]
    v_one = jnp.broadcast_to(one, kpe.shape).astype(BF16)
    kpe_ss = jnp.sum(kpe * kpe, axis=-1, keepdims=True)
    kpe_roped = rope(kpe * gkp_ref[...])
    for h in range(A_HEADS):
        kn = kv[:, h * A_NOPE:(h + 1) * A_NOPE]
        r = lax.rsqrt((jnp.sum(kn * kn, axis=-1, keepdims=True) + kpe_ss) * (1.0 / A_QK) + EPS)
        k_ref[:, h * A_HEAD_PAD:h * A_HEAD_PAD + LANES] = (kn * r * gkn).astype(BF16)
        k_ref[:, h * A_HEAD_PAD + LANES:(h + 1) * A_HEAD_PAD] = (kpe_roped * r + one).astype(BF16)

        v_ref[:, h * A_V_PAD:h * A_V_PAD + A_V] = (
            kv[:, A_HEADS * A_NOPE + h * A_V:A_HEADS * A_NOPE + (h + 1) * A_V].astype(BF16))
        v_ref[:, h * A_V_PAD + A_V:(h + 1) * A_V_PAD] = v_one


def _rows_call(kernel_fn, row_inputs, consts, out_widths, out_dtypes, tm, vmem_mib, name):
    s = row_inputs[0][0].shape[0]
    in_specs = [pl.BlockSpec((tm, w), lambda i, cb=cb: (i, cb)) for _, w, cb in row_inputs]
    in_specs += [_const_spec(*c) if isinstance(c, tuple) else _const_spec(c) for c in consts]
    out_specs = [pl.BlockSpec((tm, w), lambda i: (i, 0)) for w in out_widths]
    out_shape = [jax.ShapeDtypeStruct((s, w), dt) for w, dt in zip(out_widths, out_dtypes)]
    return pl.pallas_call(
        kernel_fn,
        grid=(s // tm,),
        in_specs=in_specs,
        out_specs=out_specs,
        out_shape=out_shape,
        compiler_params=pltpu.CompilerParams(dimension_semantics=("arbitrary",),
                                             vmem_limit_bytes=vmem_mib * 1024 * 1024),
        name=name,
    )(*[a for a, _, _ in row_inputs], *[c[0] if isinstance(c, tuple) else c for c in consts])


def _a_prologue(x, pos_col, consts):
    wide = A_HEADS * A_HEAD_PAD
    return _rows_call(_a_prologue_kernel, [(x, D_MODEL, 0), (pos_col, 1, 0)], consts,
                      (wide, wide, A_HEADS * A_V_PAD, A_WIDTH), (BF16, BF16, BF16, BF16),
                      tm=256, vmem_mib=58, name="a_prologue")


def _b_qk_kernel(x_ref, pos_ref, ng_ref, w_ref, g_ref, extc_ref, ea_ref, eb_ref, out_ref):
    x = x_ref[...]
    xn = (x * _rms(x, D_MODEL) * ng_ref[...]).astype(BF16)
    z = jnp.dot(xn, w_ref[...], preferred_element_type=F32)
    pos = pos_ref[...]
    a = jnp.right_shift(pos, POS_SPLIT_SHIFT).astype(F32)
    b = jnp.bitwise_and(pos, (1 << POS_SPLIT_SHIFT) - 1).astype(F32)
    ext_ab = a * ea_ref[...] + b * eb_ref[...]
    g = g_ref[...]
    for h in range(B_HEADS):
        for c in range(2):
            t = z[:, (2 * h + c) * B_HD:(2 * h + c + 1) * B_HD]
            out_ref[:, h * B_HEAD_PAD + c * B_HD:h * B_HEAD_PAD + (c + 1) * B_HD] = (
                t * _rms(t, B_HD) * g).astype(BF16)
        out_ref[:, h * B_HEAD_PAD + 2 * B_HD:(h + 1) * B_HEAD_PAD] = (
            extc_ref[h:h + 1, :] + ext_ab).astype(BF16)


def _b_vg_kernel(x_ref, ng_ref, wv_ref, wg_ref, v_ref, sg_ref):
    x = x_ref[...]
    xn = (x * _rms(x, D_MODEL) * ng_ref[...]).astype(BF16)
    v_ref[...] = jnp.dot(xn, wv_ref[...], preferred_element_type=F32).astype(BF16)
    gate = jnp.dot(xn, wg_ref[...], preferred_element_type=F32)
    sg_ref[...] = (gate * jax.nn.sigmoid(gate)).astype(BF16)


def _out_proj_kernel(x_ref, o_ref, sg_ref, w_ref, y_ref):
    a = o_ref[...] * sg_ref[...]
    y_ref[...] = x_ref[...] + jnp.dot(a, w_ref[...], preferred_element_type=F32)


def _out_proj(x, o, sg, w, name):
    (y,) = _rows_call(_out_proj_kernel, [(x, D_MODEL, 0), (o, D_MODEL, 0), (sg, D_MODEL, 0)], (w,),
                      (D_MODEL,), (F32,), tm=512, vmem_mib=48, name=name)
    return y


def _tile_tables(pos, tq, tk):
    s = pos.shape[0]
    nq, nk = s // tq, s // tk
    ch = jnp.right_shift(pos, CHUNK_SHIFT)
    qmax_ch = ch.reshape(nq, tq).max(axis=1)
    kmin_ch = ch.reshape(nk, tk).min(axis=1)
    qmin = pos.reshape(nq, tq).min(axis=1)
    kmax = pos.reshape(nk, tk).max(axis=1)
    in_range = (pos.min() >= 0) & (pos.max() < (1 << (POS_SPLIT_SHIFT + 8)))
    visible = kmin_ch[None, :] <= qmax_ch[:, None]
    linear = visible & (kmax[None, :] <= qmin[:, None]) & in_range
    general = visible & jnp.logical_not(linear)
    rank = jnp.where(general, 0, jnp.where(linear, 1, 2))
    order = jnp.argsort(rank, axis=1, stable=True)
    return (general.sum(axis=1).astype(jnp.int32), visible.sum(axis=1).astype(jnp.int32),
            order.astype(jnp.int32).reshape(-1))


def _pipelined_tiles(n, tile_at, produce, consume, first_produced=False):
    if not first_produced:
        produce(tile_at(0), 0)

    def two_steps(t):
        produce(tile_at(t + 1), 1)
        consume(tile_at(t), 0)
        produce(tile_at(t + 2), 0)
        consume(tile_at(t + 1), 1)

    def quad(u, c):
        two_steps(4 * u)
        two_steps(4 * u + 2)
        return c

    n_quads = (n - 1) // 4
    lax.fori_loop(0, n_quads, quad, 0)
    done = 4 * n_quads

    def pair(u, c):
        two_steps(done + 2 * u)
        return c

    lax.fori_loop(0, (n - 1 - done) // 2, pair, 0)
    last = n - 1

    @pl.when(last % 2 == 1)
    def _():
        produce(tile_at(last), 1)
        consume(tile_at(last - 1), 0)
        consume(tile_at(last), 1)

    @pl.when(last % 2 == 0)
    def _():
        consume(tile_at(last), 0)


def _lane_group_sum(p):
    out = p[:, :LANES]
    for j in range(1, p.shape[1] // LANES):
        out = out + p[:, j * LANES:(j + 1) * LANES]
    return out


def _a_attn_kernel(cnt_ref, list_ref, q_ref, k_ref, v_ref, pq_ref, pk_ref, o_ref,
                   p_sc, acc_sc, m_sc, alpha_sc, *, tk, nk, online):
    qi = pl.program_id(1)
    q = q_ref[...]
    qchunk = jnp.right_shift(pq_ref[...], CHUNK_SHIFT)
    acc_sc[...] = jnp.zeros_like(acc_sc)
    if online:
        m_sc[...] = jnp.full_like(m_sc, NEG_BIG)

    def produce(ki, slot):
        k = k_ref[pl.ds(pl.multiple_of(ki * tk, tk), tk), :]
        s = lax.dot_general(q, k, _NT, preferred_element_type=F32)
        kchunk = jnp.right_shift(pk_ref[ki], CHUNK_SHIFT)
        s = jnp.where(kchunk <= qchunk, s, NEG_BIG)
        if online:
            m_prev = m_sc[...]
            m_new = jnp.maximum(m_prev, jnp.max(s, axis=1, keepdims=True))
            alpha_sc[slot] = jnp.exp2(m_prev - m_new)
            m_sc[...] = m_new
            s = s - m_new
        p_sc[slot] = jnp.exp2(s).astype(BF16)

    def consume(ki, slot):
        v = v_ref[pl.ds(pl.multiple_of(ki * tk, tk), tk), :]
        pv = jnp.dot(p_sc[slot], v, preferred_element_type=F32)
        if online:
            acc_sc[...] = alpha_sc[slot] * acc_sc[...] + pv
        else:
            acc_sc[...] += pv

    _pipelined_tiles(cnt_ref[qi], lambda t: list_ref[qi * nk + t], produce, consume)
    acc = acc_sc[...]
    o_ref[...] = (acc[:, :A_V] / acc[:, A_ONE_COL:A_ONE_COL + 1]).astype(BF16)


def _a_attn(q, k, v, pos, online):
    s = q.shape[0]
    tq, tk = ATTN_TILES_ONLINE if online else ATTN_TILES
    nq, nk = s // tq, s // tk
    tables = _tile_tables(pos, tq, tk)[1:]
    pos_col, pos_k3 = pos.reshape(s, 1), pos.reshape(nk, 1, tk)
    grid_spec = pltpu.PrefetchScalarGridSpec(
        num_scalar_prefetch=2,
        grid=(A_HEADS, nq),
        in_specs=[pl.BlockSpec((tq, A_HEAD_PAD), lambda h, i, *_: (i, h)),
                  pl.BlockSpec((s, A_HEAD_PAD), lambda h, i, *_: (0, h)),
                  pl.BlockSpec((s, A_V_PAD), lambda h, i, *_: (0, h)),
                  pl.BlockSpec((tq, 1), lambda h, i, *_: (i, 0)),
                  pl.BlockSpec((nk, 1, tk), lambda h, i, *_: (0, 0, 0))],
        out_specs=pl.BlockSpec((tq, A_V), lambda h, i, *_: (i, h)),
        scratch_shapes=[pltpu.VMEM((2, tq, tk), BF16), pltpu.VMEM((tq, A_V_PAD), F32),
                        pltpu.VMEM((tq, 1), F32), pltpu.VMEM((2, tq, 1), F32)],
    )
    return pl.pallas_call(
        functools.partial(_a_attn_kernel, tk=tk, nk=nk, online=online),
        grid_spec=grid_spec,
        out_shape=jax.ShapeDtypeStruct((s, A_WIDTH), BF16),
        compiler_params=pltpu.CompilerParams(dimension_semantics=("arbitrary", "arbitrary"),
                                             vmem_limit_bytes=56 * 1024 * 1024),
        name="a_attention_online" if online else "a_attention",
    )(*tables, q, k, v, pos_col, pos_k3)


def _b_attn_kernel(gcnt_ref, cnt_ref, list_ref, q_ref, k_ref, v_ref, pq_ref, pk_ref, slope_ref,
                   shift_ref, lq1_ref, lk1_ref, lq2_ref, lk2_ref, sub_ref, o_ref,
                   p_sc, acc_sc, l_sc, m_sc, alpha_sc, *, tk, nk, lam_init, online):
    qi = pl.program_id(1)
    q = q_ref[...]
    q_ext = q[:, 2 * B_HD:]
    q_full = [jnp.concatenate([q[:, c * B_HD:(c + 1) * B_HD], q_ext], axis=1) for c in range(2)]
    pq = pq_ref[...]
    qchunk = jnp.right_shift(pq, CHUNK_SHIFT)
    neg_slope = slope_ref[0, :, 0:1]
    shift = shift_ref[0:1, 0:1]
    l_sc[...] = jnp.zeros_like(l_sc)
    acc_sc[...] = jnp.zeros_like(acc_sc)
    if online:
        m_sc[...] = jnp.full_like(m_sc, NEG_BIG)

    def finish(s, c, slot):
        if online:
            m_prev = m_sc[c]
            m_new = jnp.maximum(m_prev, jnp.max(s, axis=1, keepdims=True))
            alpha = jnp.exp2(m_prev - m_new)
            alpha_sc[slot, c] = alpha
            m_sc[c] = m_new
            p = jnp.exp2(s - m_new)
            l_sc[c] = alpha * l_sc[c] + _lane_group_sum(p)
        else:
            p = jnp.exp2(s)
            l_sc[c] += _lane_group_sum(p)
        p_sc[slot, c] = p.astype(BF16)

    def produce_linear(ki, slot):
        k = k_ref[pl.ds(pl.multiple_of(ki * tk, tk), tk), :]
        k_ext = k[:, 2 * B_HD:]
        for c in range(2):
            k_full = jnp.concatenate([k[:, c * B_HD:(c + 1) * B_HD], k_ext], axis=1)
            finish(lax.dot_general(q_full[c], k_full, _NT, preferred_element_type=F32), c, slot)

    def produce_general(ki, slot):
        k = k_ref[pl.ds(pl.multiple_of(ki * tk, tk), tk), :]
        pk = pk_ref[ki]
        bias = jnp.abs(pq - pk).astype(F32) * neg_slope - shift
        bias = jnp.where(jnp.right_shift(pk, CHUNK_SHIFT) <= qchunk, bias, NEG_BIG)
        for c in range(2):
            s = lax.dot_general(q[:, c * B_HD:(c + 1) * B_HD], k[:, c * B_HD:(c + 1) * B_HD],
                                _NT, preferred_element_type=F32) + bias
            finish(s, c, slot)

    def consume(ki, slot):
        v = v_ref[pl.ds(pl.multiple_of(ki * tk, tk), tk), :]
        for c in range(2):
            pv = jnp.dot(p_sc[slot, c], v, preferred_element_type=F32)
            if online:
                acc_sc[c] = alpha_sc[slot, c] * acc_sc[c] + pv
            else:
                acc_sc[c] += pv

    n_general = gcnt_ref[qi]
    tile_at = lambda t: list_ref[qi * nk + t]

    def general_body(t, c):
        produce_general(tile_at(t), 0)
        consume(tile_at(t), 0)
        return c

    first = jnp.maximum(n_general - 1, 0)
    lax.fori_loop(0, first, general_body, 0)

    @pl.when(n_general > 0)
    def _():
        produce_general(tile_at(first), 0)

    @pl.when(n_general == 0)
    def _():
        produce_linear(tile_at(first), 0)

    _pipelined_tiles(cnt_ref[qi] - first, lambda t: tile_at(first + t), produce_linear, consume,
                     first_produced=True)

    lam = (jnp.exp(jnp.sum(lq1_ref[...] * lk1_ref[...], axis=-1, keepdims=True))
           - jnp.exp(jnp.sum(lq2_ref[...] * lk2_ref[...], axis=-1, keepdims=True)) + lam_init)
    l0 = jnp.sum(l_sc[0], axis=-1, keepdims=True)
    l1 = jnp.sum(l_sc[1], axis=-1, keepdims=True)
    od = acc_sc[0] / l0 - lam * (acc_sc[1] / l1)
    o_ref[...] = (od * _rms(od, B_V) * (sub_ref[...] * (1.0 - lam_init))).astype(BF16)


def _b_attn(q, k, v, pos, consts, lam_init, online):
    s = q.shape[0]
    tq, tk = ATTN_TILES_ONLINE if online else ATTN_TILES
    nq, nk = s // tq, s // tk
    tables = _tile_tables(pos, tq, tk)
    pos_col, pos_k3 = pos.reshape(s, 1), pos.reshape(nk, 1, tk)
    slopes = consts[0]
    grid_spec = pltpu.PrefetchScalarGridSpec(
        num_scalar_prefetch=3,
        grid=(B_HEADS, nq),
        in_specs=[pl.BlockSpec((tq, B_HEAD_PAD), lambda h, i, *_: (i, h)),
                  pl.BlockSpec((s, B_HEAD_PAD), lambda h, i, *_: (0, h),
                               pipeline_mode=pl.Buffered(1)),
                  pl.BlockSpec((s, B_V), lambda h, i, *_: (0, h), pipeline_mode=pl.Buffered(1)),
                  pl.BlockSpec((tq, 1), lambda h, i, *_: (i, 0)),
                  pl.BlockSpec((nk, 1, tk), lambda h, i, *_: (0, 0, 0)),
                  pl.BlockSpec((1,) + slopes.shape[1:], lambda h, i, *_: (h, 0, 0))]
                 + [pl.BlockSpec(c.shape, lambda h, i, *_: (0, 0)) for c in consts[1:]],
        out_specs=pl.BlockSpec((tq, B_V), lambda h, i, *_: (i, h)),
        scratch_shapes=[pltpu.VMEM((2, 2, tq, tk), BF16), pltpu.VMEM((2, tq, B_V), F32),
                        pltpu.VMEM((2, tq, LANES), F32), pltpu.VMEM((2, tq, 1), F32),
                        pltpu.VMEM((2, 2, tq, 1), F32)],
    )
    return pl.pallas_call(
        functools.partial(_b_attn_kernel, tk=tk, nk=nk, lam_init=lam_init, online=online),
        grid_spec=grid_spec,
        out_shape=jax.ShapeDtypeStruct((s, B_WIDTH), BF16),
        compiler_params=pltpu.CompilerParams(dimension_semantics=("arbitrary", "arbitrary"),
                                             vmem_limit_bytes=56 * 1024 * 1024),
        name="b_attention_online" if online else "b_attention",
    )(*tables, q, k, v, pos_col, pos_k3, *consts)


def _attend(fn, shift_bound, *args, **kwargs):
    return lax.cond(shift_bound <= MAX_CONST_SHIFT,
                    lambda: fn(*args, online=False, **kwargs),
                    lambda: fn(*args, online=True, **kwargs))


def _lane_onehot(value, lane):
    return jnp.zeros((1, LANES), F32).at[0, lane].set(value)


def _a_consts(norm_g, w_in, q_norm_g, w_q_up, kv_norm_g, w_kv_up, q_gain, k_gain):
    n_lat = A_Q_LORA + A_KV_LORA + A_ROPE
    wa = jnp.pad(w_in[:, :n_lat], ((0, 0), (0, LANES - A_ROPE))).astype(BF16)
    wg = w_in[:, n_lat:].astype(BF16)
    wq = w_q_up.reshape(A_Q_LORA, A_HEADS, A_QK)
    wq = jnp.pad(wq, ((0, 0), (0, 0), (0, A_HEAD_PAD - A_QK)))
    wq = wq.reshape(A_Q_LORA, A_HEADS * A_HEAD_PAD).astype(BF16)
    wkv = w_kv_up.reshape(A_KV_LORA, A_HEADS, A_NOPE + A_V)
    wkv = jnp.concatenate([wkv[:, :, :A_NOPE].reshape(A_KV_LORA, -1),
                           wkv[:, :, A_NOPE:].reshape(A_KV_LORA, -1)], axis=1).astype(BF16)
    q_scale = LOG2E / math.sqrt(A_QK)
    gq = jnp.pad(q_gain * q_scale, (0, A_HEAD_PAD - A_QK)).reshape(1, A_HEAD_PAD)
    gkn = k_gain[:A_NOPE].reshape(1, LANES)
    gkp = jnp.pad(k_gain[A_NOPE:], (0, LANES - A_ROPE)).reshape(1, LANES)
    half = A_ROPE // 2
    inv_freq = ROPE_THETA ** (-jnp.arange(0, A_ROPE, 2, dtype=F32) / A_ROPE)
    zeros = jnp.zeros((LANES - A_ROPE,), F32)
    invf = jnp.concatenate([inv_freq, inv_freq, zeros]).reshape(1, LANES)
    sgn = jnp.concatenate([-jnp.ones((half,), F32), jnp.ones((half,), F32), zeros]).reshape(1, LANES)
    bound = (BOUND_MARGIN * A_QK * q_scale
             * jnp.max(jnp.abs(q_gain)) * jnp.max(jnp.abs(k_gain)))
    bound = bound.astype(BF16).astype(F32)
    qshift = _lane_onehot(-bound, A_ROPE)
    one = _lane_onehot(1.0, A_ROPE)
    consts = (norm_g.reshape(1, -1), wa, wg, q_norm_g.reshape(1, -1), wq, kv_norm_g.reshape(1, -1),
              wkv, gq, gkn, gkp, invf, sgn, qshift, one)
    return consts, bound


def _b_consts(q_gain, k_gain):
    b_scale = LOG2E / math.sqrt(B_HD)
    bound = (BOUND_MARGIN * B_HD * b_scale * jnp.max(jnp.abs(q_gain)) * jnp.max(jnp.abs(k_gain)))
    bound = bound.astype(BF16).astype(F32)
    slopes = 2.0 ** (-8.0 * jnp.arange(1, B_HEADS + 1, dtype=F32) / B_HEADS)
    sigma = LOG2E * slopes
    sig_hi = sigma.astype(BF16).astype(F32)
    sig_lo = (sigma - sig_hi).astype(BF16).astype(F32)
    big = float(1 << POS_SPLIT_SHIFT)
    sig_cols = jnp.stack([sig_hi * big, sig_hi, sig_lo * big, sig_lo], axis=1)
    zeros = lambda n: jnp.zeros((B_HEADS, n), F32)
    extc_q = jnp.concatenate([jnp.broadcast_to(-bound, (B_HEADS, 1)), sig_cols, zeros(LANES - 5)],
                             axis=1)
    extc_k = jnp.concatenate([jnp.ones((B_HEADS, 1), F32), zeros(4), sig_cols, zeros(LANES - 9)],
                             axis=1)
    lanes = jnp.arange(LANES)
    pick = lambda ids, val: jnp.where(jnp.isin(lanes, jnp.array(ids)), val, 0.0).reshape(1, LANES)
    ea_q, eb_q = pick([5, 7], -1.0), pick([6, 8], -1.0)
    ea_k, eb_k = pick([1, 3], 1.0), pick([2, 4], 1.0)
    neg_slopes = jnp.broadcast_to(-(sig_hi + sig_lo)[:, None, None], (B_HEADS, 1, LANES))
    return bound, b_scale, neg_slopes, (extc_q, ea_q, eb_q), (extc_k, ea_k, eb_k)


def kernel(x, positions, a_norm, a_w_in, a_q_norm, a_w_q_up, a_kv_norm, a_w_kv_up, a_q_gain, a_k_gain, a_w_out, b_norm, b_w_in, b_q_gain, b_k_gain, b_lambda_q1, b_lambda_k1, b_lambda_q2, b_lambda_k2, b_subln, b_w_out):
    batch, seq, _ = x.shape
    assert batch == 1
    xs = x[0]
    pos = positions[0]
    pos_col = pos.reshape(seq, 1)
    row = lambda a: a.reshape(1, -1)

    consts, a_bound = _a_consts(a_norm[0], a_w_in[0], a_q_norm[0], a_w_q_up[0], a_kv_norm[0],
                                a_w_kv_up[0], a_q_gain[0], a_k_gain[0])
    q, k, v, sg = _a_prologue(xs, pos_col, consts)
    o = _attend(_a_attn, a_bound, q, k, v, pos)
    xs = _out_proj(xs, o, sg, a_w_out[0].astype(BF16), name="a_out_proj")

    layer_idx = 1
    lam_init = 0.8 - 0.6 * math.exp(-0.3 * layer_idx)
    b_bound, b_scale, neg_slopes, q_ext, k_ext = _b_consts(b_q_gain[0], b_k_gain[0])
    w_in = b_w_in[0].astype(BF16)
    ng = row(b_norm[0])
    section = lambda j: (w_in, (D_MODEL, B_WIDTH), (0, j))
    rows = [(xs, D_MODEL, 0), (pos_col, 1, 0)]
    (q,) = _rows_call(_b_qk_kernel, rows, (ng, section(0), row(b_q_gain[0] * b_scale), *q_ext),
                      (B_HEADS * B_HEAD_PAD,), (BF16,), tm=512, vmem_mib=48, name="b_q_proj")
    (k,) = _rows_call(_b_qk_kernel, rows, (ng, section(1), row(b_k_gain[0]), *k_ext),
                      (B_HEADS * B_HEAD_PAD,), (BF16,), tm=512, vmem_mib=48, name="b_k_proj")
    v, sg = _rows_call(_b_vg_kernel, rows[:1], (ng, section(2), section(3)),
                       (B_WIDTH, B_WIDTH), (BF16, BF16), tm=512, vmem_mib=56, name="b_vg_proj")
    b_consts = (neg_slopes, jnp.broadcast_to(b_bound.reshape(1, 1), (1, LANES)),
                row(b_lambda_q1[0]), row(b_lambda_k1[0]), row(b_lambda_q2[0]), row(b_lambda_k2[0]),
                row(b_subln[0]))
    o = _attend(_b_attn, b_bound, q, k, v, pos, b_consts, lam_init)
    xs = _out_proj(xs, o, sg, b_w_out[0].astype(BF16), name="b_out_proj")
    return xs[None]
```

```python
import functools
import math

import jax
import jax.numpy as jnp
from jax import lax
from jax.experimental import pallas as pl
from jax.experimental.pallas import tpu as pltpu

D_MODEL = 2048
CHUNK_SHIFT = 6
EPS = 1e-6
LOG2E = 1.4426950408889634

A_HEADS = 16
A_NOPE = 128
A_ROPE = 64
A_QK = A_NOPE + A_ROPE
A_V = 128
A_Q_LORA = 512
A_KV_LORA = 512
A_WIDTH = A_HEADS * A_V
A_HEAD_PAD = 256
A_V_PAD = 256
A_ONE_COL = A_V + A_ROPE
ROPE_THETA = 10000.0

B_HEADS = 8
B_HD = 128
B_V = 2 * B_HD
B_WIDTH = B_HEADS * B_V
B_QK = B_HEADS * 2 * B_HD
B_HEAD_PAD = 3 * B_HD
POS_SPLIT_SHIFT = 7

LANES = 128
NEG_BIG = -1e30
MAX_CONST_SHIFT = 50.0
BOUND_MARGIN = 1.01
ATTN_TILES = (1024, 1024)
ATTN_TILES_ONLINE = (512, 1024)

F32 = jnp.float32
BF16 = jnp.bfloat16
_NT = (((1,), (1,)), ((), ()))


def _const_spec(a, block_shape=None, block_index=None):
    shape = a.shape if block_shape is None else block_shape
    index = (0,) * len(shape) if block_index is None else block_index
    return pl.BlockSpec(shape, lambda *_: index, pipeline_mode=pl.Buffered(1))


def _rms(x, denom):
    return lax.rsqrt(jnp.sum(x * x, axis=-1, keepdims=True) * (1.0 / denom) + EPS)


def _a_prologue_kernel(x_ref, pos_ref, ng_ref, wa_ref, wg_ref, qng_ref, wq_ref, kvng_ref, wkv_ref,
                       gq_ref, gkn_ref, gkp_ref, invf_ref, sgn_ref, qshift_ref, one_ref,
                       q_ref, k_ref, v_ref, sg_ref):
    x = x_ref[...]
    xn = (x * _rms(x, D_MODEL) * ng_ref[...]).astype(BF16)
    za = jnp.dot(xn, wa_ref[...], preferred_element_type=F32)
    gate = jnp.dot(xn, wg_ref[...], preferred_element_type=F32)
    sg_ref[...] = (gate * jax.nn.sigmoid(gate)).astype(BF16)
    rope = _rope_fn(pos_ref, invf_ref, sgn_ref)
    _a_q_rows(za[:, :A_Q_LORA], rope, qng_ref, wq_ref, gq_ref, qshift_ref, q_ref)
    _a_kv_rows(za[:, A_Q_LORA:A_Q_LORA + A_KV_LORA], za[:, A_Q_LORA + A_KV_LORA:], rope,
               kvng_ref, wkv_ref, gkn_ref, gkp_ref, one_ref, k_ref, v_ref)


def _rope_fn(pos_ref, invf_ref, sgn_ref):
    ang = pos_ref[...].astype(F32) * invf_ref[...]
    cos = jnp.cos(ang)
    sin_signed = jnp.sin(ang) * sgn_ref[...]
    lane = lax.broadcasted_iota(jnp.int32, cos.shape, 1)
    first_half = lane < (A_ROPE // 2)

    def rope(t):
        swapped = jnp.where(first_half, pltpu.roll(t, LANES - A_ROPE // 2, 1),
                            pltpu.roll(t, A_ROPE // 2, 1))
        return t * cos + swapped * sin_signed

    return rope


def _a_q_rows(cq, rope, qng_ref, wq_ref, gq_ref, qshift_ref, q_ref):
    cqn = (cq * _rms(cq, A_Q_LORA) * qng_ref[...]).astype(BF16)
    q = jnp.dot(cqn, wq_ref[...], preferred_element_type=F32)
    gq = gq_ref[...]
    qshift = qshift_ref[...]
    for h in range(A_HEADS):
        qh = q[:, h * A_HEAD_PAD:(h + 1) * A_HEAD_PAD]
        qn = qh * _rms(qh, A_QK) * gq
        q_ref[:, h * A_HEAD_PAD:h * A_HEAD_PAD + LANES] = qn[:, :LANES].astype(BF16)
        q_ref[:, h * A_HEAD_PAD + LANES:(h + 1) * A_HEAD_PAD] = (
            rope(qn[:, LANES:]) + qshift).astype(BF16)


def _a_kv_rows(ckv, kpe, rope, kvng_ref, wkv_ref, gkn_ref, gkp_ref, one_ref, k_ref, v_ref):
    ckvn = (ckv * _rms(ckv, A_KV_LORA) * kvng_ref[...]).astype(BF16)
    kv = jnp.dot(ckvn, wkv_ref[...], preferred_element_type=F32)
    gkn = gkn_ref[...]
    one = one_ref[...]
    v_one = jnp.broadcast_to(one, kpe.shape).astype(BF16)
    kpe_ss = jnp.sum(kpe * kpe, axis=-1, keepdims=True)
    kpe_roped = rope(kpe * gkp_ref[...])
    for h in range(A_HEADS):
        kn = kv[:, h * A_NOPE:(h + 1) * A_NOPE]
        r = lax.rsqrt((jnp.sum(kn * kn, axis=-1, keepdims=True) + kpe_ss) * (1.0 / A_QK) + EPS)
        k_ref[:, h * A_HEAD_PAD:h * A_HEAD_PAD + LANES] = (kn * r * gkn).astype(BF16)
        k_ref[:, h * A_HEAD_PAD + LANES:(h + 1) * A_HEAD_PAD] = (kpe_roped * r + one).astype(BF16)

        v_ref[:, h * A_V_PAD:h * A_V_PAD + A_V] = (
            kv[:, A_HEADS * A_NOPE + h * A_V:A_HEADS * A_NOPE + (h + 1) * A_V].astype(BF16))
        v_ref[:, h * A_V_PAD + A_V:(h + 1) * A_V_PAD] = v_one


def _rows_call(kernel_fn, row_inputs, consts, out_widths, out_dtypes, tm, vmem_mib, name):
    s = row_inputs[0][0].shape[0]
    in_specs = [pl.BlockSpec((tm, w), lambda i, cb=cb: (i, cb)) for _, w, cb in row_inputs]
    in_specs += [_const_spec(*c) if isinstance(c, tuple) else _const_spec(c) for c in consts]
    out_specs = [pl.BlockSpec((tm, w), lambda i: (i, 0)) for w in out_widths]
    out_shape = [jax.ShapeDtypeStruct((s, w), dt) for w, dt in zip(out_widths, out_dtypes)]
    return pl.pallas_call(
        kernel_fn,
        grid=(s // tm,),
        in_specs=in_specs,
        out_specs=out_specs,
        out_shape=out_shape,
        compiler_params=pltpu.CompilerParams(dimension_semantics=("arbitrary",),
                                             vmem_limit_bytes=vmem_mib * 1024 * 1024),
        name=name,
    )(*[a for a, _, _ in row_inputs], *[c[0] if isinstance(c, tuple) else c for c in consts])


def _a_prologue(x, pos_col, consts):
    wide = A_HEADS * A_HEAD_PAD
    return _rows_call(_a_prologue_kernel, [(x, D_MODEL, 0), (pos_col, 1, 0)], consts,
                      (wide, wide, A_HEADS * A_V_PAD, A_WIDTH), (BF16, BF16, BF16, BF16),
                      tm=256, vmem_mib=58, name="a_prologue")


def _b_qk_kernel(x_ref, pos_ref, ng_ref, w_ref, g_ref, extc_ref, ea_ref, eb_ref, out_ref):
    x = x_ref[...]
    xn = (x * _rms(x, D_MODEL) * ng_ref[...]).astype(BF16)
    z = jnp.dot(xn, w_ref[...], preferred_element_type=F32)
    pos = pos_ref[...]
    a = jnp.right_shift(pos, POS_SPLIT_SHIFT).astype(F32)
    b = jnp.bitwise_and(pos, (1 << POS_SPLIT_SHIFT) - 1).astype(F32)
    ext_ab = a * ea_ref[...] + b * eb_ref[...]
    g = g_ref[...]
    for h in range(B_HEADS):
        for c in range(2):
            t = z[:, (2 * h + c) * B_HD:(2 * h + c + 1) * B_HD]
            out_ref[:, h * B_HEAD_PAD + c * B_HD:h * B_HEAD_PAD + (c + 1) * B_HD] = (
                t * _rms(t, B_HD) * g).astype(BF16)
        out_ref[:, h * B_HEAD_PAD + 2 * B_HD:(h + 1) * B_HEAD_PAD] = (
            extc_ref[h:h + 1, :] + ext_ab).astype(BF16)


def _b_vg_kernel(x_ref, ng_ref, wv_ref, wg_ref, v_ref, sg_ref):
    x = x_ref[...]
    xn = (x * _rms(x, D_MODEL) * ng_ref[...]).astype(BF16)
    v_ref[...] = jnp.dot(xn, wv_ref[...], preferred_element_type=F32).astype(BF16)
    gate = jnp.dot(xn, wg_ref[...], preferred_element_type=F32)
    sg_ref[...] = (gate * jax.nn.sigmoid(gate)).astype(BF16)


def _out_proj_kernel(x_ref, o_ref, sg_ref, w_ref, y_ref):
    a = o_ref[...] * sg_ref[...]
    y_ref[...] = x_ref[...] + jnp.dot(a, w_ref[...], preferred_element_type=F32)


def _out_proj(x, o, sg, w, name):
    (y,) = _rows_call(_out_proj_kernel, [(x, D_MODEL, 0), (o, D_MODEL, 0), (sg, D_MODEL, 0)], (w,),
                      (D_MODEL,), (F32,), tm=512, vmem_mib=48, name=name)
    return y


def _tile_tables(pos, tq, tk):
    s = pos.shape[0]
    nq, nk = s // tq, s // tk
    ch = jnp.right_shift(pos, CHUNK_SHIFT)
    qmax_ch = ch.reshape(nq, tq).max(axis=1)
    kmin_ch = ch.reshape(nk, tk).min(axis=1)
    qmin = pos.reshape(nq, tq).min(axis=1)
    kmax = pos.reshape(nk, tk).max(axis=1)
    in_range = (pos.min() >= 0) & (pos.max() < (1 << (POS_SPLIT_SHIFT + 8)))
    visible = kmin_ch[None, :] <= qmax_ch[:, None]
    linear = visible & (kmax[None, :] <= qmin[:, None]) & in_range
    general = visible & jnp.logical_not(linear)
    rank = jnp.where(general, 0, jnp.where(linear, 1, 2))
    order = jnp.argsort(rank, axis=1, stable=True)
    n_general = general.sum(axis=1)
    n_visible = visible.sum(axis=1)
    qmax_top = ch.reshape(nq, 2, tq // 2)[:, 0].max(axis=1)
    kmin_right = ch.reshape(nk, 2, tk // 2)[:, 1].min(axis=1)
    quarter = kmin_right[None, :] > qmax_top[:, None]
    at = lambda idx: jnp.take_along_axis(quarter, jnp.take_along_axis(order, idx[:, None], axis=1),
                                         axis=1)[:, 0]
    i32 = lambda a: a.astype(jnp.int32)
    return dict(n_general=i32(n_general), n_visible=i32(n_visible), order=i32(order).reshape(-1),
                quarter_last_visible=i32(at(n_visible - 1)),
                quarter_last_general=i32(at(jnp.maximum(n_general - 1, 0)) & (n_general > 0)))


def _pipelined_tiles(n, tile_at, produce, consume, first_produced=False, last_variant=None):
    if not first_produced:
        produce(tile_at(0), 0)

    def two_steps(t):
        produce(tile_at(t + 1), 1)
        consume(tile_at(t), 0)
        produce(tile_at(t + 2), 0)
        consume(tile_at(t + 1), 1)

    def quad(u, c):
        two_steps(4 * u)
        two_steps(4 * u + 2)
        return c

    n_quads = (n - 1) // 4
    lax.fori_loop(0, n_quads, quad, 0)
    done = 4 * n_quads

    def pair(u, c):
        two_steps(done + 2 * u)
        return c

    lax.fori_loop(0, (n - 1 - done) // 2, pair, 0)
    last = n - 1

    def tail(cond, produce_last, consume_last):
        @pl.when(cond & (last % 2 == 1))
        def _():
            produce_last(tile_at(last), 1)
            consume(tile_at(last - 1), 0)
            consume_last(tile_at(last), 1)

        @pl.when(cond & (last % 2 == 0))
        def _():
            consume_last(tile_at(last), 0)

    if last_variant is None:
        tail(True, produce, consume)
    else:
        flag, produce_variant, consume_variant = last_variant
        tail(flag == 0, produce, consume)
        tail(flag != 0, produce_variant, consume_variant)


def _lane_group_sum(p):
    out = p[:, :LANES]
    for j in range(1, p.shape[1] // LANES):
        out = out + p[:, j * LANES:(j + 1) * LANES]
    return out


def _a_attn_kernel(cnt_ref, list_ref, quarter_ref, q_ref, k_ref, v_ref, pq_ref, pk_ref, o_ref,
                   p_sc, acc_sc, m_sc, alpha_sc, *, tk, nk, online):
    qi = pl.program_id(1)
    q = q_ref[...]
    tq = q.shape[0]
    hq, hk = tq // 2, tk // 2
    qchunk = jnp.right_shift(pq_ref[...], CHUNK_SHIFT)
    acc_sc[...] = jnp.zeros_like(acc_sc)
    if online:
        m_sc[...] = jnp.full_like(m_sc, NEG_BIG)

    def produce(ki, slot):
        k = k_ref[pl.ds(pl.multiple_of(ki * tk, tk), tk), :]
        s = lax.dot_general(q, k, _NT, preferred_element_type=F32)
        kchunk = jnp.right_shift(pk_ref[ki], CHUNK_SHIFT)
        s = jnp.where(kchunk <= qchunk, s, NEG_BIG)
        if online:
            m_prev = m_sc[...]
            m_new = jnp.maximum(m_prev, jnp.max(s, axis=1, keepdims=True))
            alpha_sc[slot] = jnp.exp2(m_prev - m_new)
            m_sc[...] = m_new
            s = s - m_new
        p_sc[slot] = jnp.exp2(s).astype(BF16)

    def consume(ki, slot):
        v = v_ref[pl.ds(pl.multiple_of(ki * tk, tk), tk), :]
        pv = jnp.dot(p_sc[slot], v, preferred_element_type=F32)
        if online:
            acc_sc[...] = alpha_sc[slot] * acc_sc[...] + pv
        else:
            acc_sc[...] += pv

    def produce_skip_quarter(ki, slot):
        start = pl.multiple_of(ki * tk, tk)
        kchunk = jnp.right_shift(pk_ref[ki], CHUNK_SHIFT)
        s = lax.dot_general(q, k_ref[pl.ds(start, hk), :], _NT, preferred_element_type=F32)
        s = jnp.where(kchunk[:, :hk] <= qchunk, s, NEG_BIG)
        p_sc[slot, :, :hk] = jnp.exp2(s).astype(BF16)
        s = lax.dot_general(q[hq:], k_ref[pl.ds(start + hk, hk), :], _NT,
                            preferred_element_type=F32)
        s = jnp.where(kchunk[:, hk:] <= qchunk[hq:], s, NEG_BIG)
        p_sc[slot, hq:, hk:] = jnp.exp2(s).astype(BF16)
        p_sc[slot, :hq, hk:] = jnp.zeros((hq, hk), BF16)

    def consume_skip_quarter(ki, slot):
        start = pl.multiple_of(ki * tk, tk)
        acc_sc[...] += jnp.dot(p_sc[slot, :, :hk], v_ref[pl.ds(start, hk), :],
                               preferred_element_type=F32)
        acc_sc[hq:, :] += jnp.dot(p_sc[slot, hq:, hk:], v_ref[pl.ds(start + hk, hk), :],
                                  preferred_element_type=F32)

    variant = None if online else (quarter_ref[qi], produce_skip_quarter, consume_skip_quarter)
    _pipelined_tiles(cnt_ref[qi], lambda t: list_ref[qi * nk + t], produce, consume,
                     last_variant=variant)
    acc = acc_sc[...]
    o_ref[...] = (acc[:, :A_V] / acc[:, A_ONE_COL:A_ONE_COL + 1]).astype(BF16)


def _a_attn(q, k, v, pos, online):
    s = q.shape[0]
    tq, tk = ATTN_TILES_ONLINE if online else ATTN_TILES
    nq, nk = s // tq, s // tk
    t = _tile_tables(pos, tq, tk)
    tables = (t["n_visible"], t["order"], t["quarter_last_visible"])
    pos_col, pos_k3 = pos.reshape(s, 1), pos.reshape(nk, 1, tk)
    grid_spec = pltpu.PrefetchScalarGridSpec(
        num_scalar_prefetch=3,
        grid=(A_HEADS, nq),
        in_specs=[pl.BlockSpec((tq, A_HEAD_PAD), lambda h, i, *_: (i, h)),
                  pl.BlockSpec((s, A_HEAD_PAD), lambda h, i, *_: (0, h)),
                  pl.BlockSpec((s, A_V_PAD), lambda h, i, *_: (0, h)),
                  pl.BlockSpec((tq, 1), lambda h, i, *_: (i, 0)),
                  pl.BlockSpec((nk, 1, tk), lambda h, i, *_: (0, 0, 0))],
        out_specs=pl.BlockSpec((tq, A_V), lambda h, i, *_: (i, h)),
        scratch_shapes=[pltpu.VMEM((2, tq, tk), BF16), pltpu.VMEM((tq, A_V_PAD), F32),
                        pltpu.VMEM((tq, 1), F32), pltpu.VMEM((2, tq, 1), F32)],
    )
    return pl.pallas_call(
        functools.partial(_a_attn_kernel, tk=tk, nk=nk, online=online),
        grid_spec=grid_spec,
        out_shape=jax.ShapeDtypeStruct((s, A_WIDTH), BF16),
        compiler_params=pltpu.CompilerParams(dimension_semantics=("arbitrary", "arbitrary"),
                                             vmem_limit_bytes=56 * 1024 * 1024),
        name="a_attention_online" if online else "a_attention",
    )(*tables, q, k, v, pos_col, pos_k3)


def _b_attn_kernel(gcnt_ref, cnt_ref, list_ref, quarter_ref, q_ref, k_ref, v_ref, pq_ref, pk_ref,
                   slope_ref, shift_ref, lq1_ref, lk1_ref, lq2_ref, lk2_ref, sub_ref, o_ref,
                   p_sc, acc_sc, l_sc, m_sc, alpha_sc, *, tk, nk, lam_init, online):
    qi = pl.program_id(1)
    q = q_ref[...]
    tq = q.shape[0]
    hq, hk = tq // 2, tk // 2
    q_ext = q[:, 2 * B_HD:]
    q_full = [jnp.concatenate([q[:, c * B_HD:(c + 1) * B_HD], q_ext], axis=1) for c in range(2)]
    pq = pq_ref[...]
    qchunk = jnp.right_shift(pq, CHUNK_SHIFT)
    neg_slope = slope_ref[0, :, 0:1]
    shift = shift_ref[0:1, 0:1]
    l_sc[...] = jnp.zeros_like(l_sc)
    acc_sc[...] = jnp.zeros_like(acc_sc)
    if online:
        m_sc[...] = jnp.full_like(m_sc, NEG_BIG)

    def finish(s, c, slot):
        if online:
            m_prev = m_sc[c]
            m_new = jnp.maximum(m_prev, jnp.max(s, axis=1, keepdims=True))
            alpha = jnp.exp2(m_prev - m_new)
            alpha_sc[slot, c] = alpha
            m_sc[c] = m_new
            p = jnp.exp2(s - m_new)
            l_sc[c] = alpha * l_sc[c] + _lane_group_sum(p)
        else:
            p = jnp.exp2(s)
            l_sc[c] += _lane_group_sum(p)
        p_sc[slot, c] = p.astype(BF16)

    def produce_linear(ki, slot):
        k = k_ref[pl.ds(pl.multiple_of(ki * tk, tk), tk), :]
        k_ext = k[:, 2 * B_HD:]
        for c in range(2):
            k_full = jnp.concatenate([k[:, c * B_HD:(c + 1) * B_HD], k_ext], axis=1)
            finish(lax.dot_general(q_full[c], k_full, _NT, preferred_element_type=F32), c, slot)

    def produce_general(ki, slot):
        k = k_ref[pl.ds(pl.multiple_of(ki * tk, tk), tk), :]
        pk = pk_ref[ki]
        bias = jnp.abs(pq - pk).astype(F32) * neg_slope - shift
        bias = jnp.where(jnp.right_shift(pk, CHUNK_SHIFT) <= qchunk, bias, NEG_BIG)
        for c in range(2):
            s = lax.dot_general(q[:, c * B_HD:(c + 1) * B_HD], k[:, c * B_HD:(c + 1) * B_HD],
                                _NT, preferred_element_type=F32) + bias
            finish(s, c, slot)

    def produce_general_skip_quarter(ki, slot):
        start = pl.multiple_of(ki * tk, tk)
        pk = pk_ref[ki]
        regions = ((slice(0, tq), slice(0, hk)), (slice(hq, tq), slice(hk, tk)))
        for rows, cols in regions:
            k = k_ref[pl.ds(start + cols.start, hk), :]
            pkc = pk[:, cols]
            bias = jnp.abs(pq[rows] - pkc).astype(F32) * neg_slope - shift
            bias = jnp.where(jnp.right_shift(pkc, CHUNK_SHIFT) <= qchunk[rows], bias, NEG_BIG)
            for c in range(2):
                s = lax.dot_general(q[rows, c * B_HD:(c + 1) * B_HD], k[:, c * B_HD:(c + 1) * B_HD],
                                    _NT, preferred_element_type=F32) + bias
                p = jnp.exp2(s)
                l_sc[c, rows, :] += _lane_group_sum(p)
                p_sc[slot, c, rows, cols] = p.astype(BF16)
        for c in range(2):
            p_sc[slot, c, :hq, hk:] = jnp.zeros((hq, hk), BF16)

    def consume(ki, slot):
        v = v_ref[pl.ds(pl.multiple_of(ki * tk, tk), tk), :]
        for c in range(2):
            pv = jnp.dot(p_sc[slot, c], v, preferred_element_type=F32)
            if online:
                acc_sc[c] = alpha_sc[slot, c] * acc_sc[c] + pv
            else:
                acc_sc[c] += pv

    n_general = gcnt_ref[qi]
    tile_at = lambda t: list_ref[qi * nk + t]

    def general_body(t, c):
        produce_general(tile_at(t), 0)
        consume(tile_at(t), 0)
        return c

    first = jnp.maximum(n_general - 1, 0)
    lax.fori_loop(0, first, general_body, 0)

    skip_quarter = False if online else quarter_ref[qi] != 0

    @pl.when((n_general > 0) & jnp.logical_not(skip_quarter))
    def _():
        produce_general(tile_at(first), 0)

    if not online:
        @pl.when((n_general > 0) & skip_quarter)
        def _():
            produce_general_skip_quarter(tile_at(first), 0)

    @pl.when(n_general == 0)
    def _():
        produce_linear(tile_at(first), 0)

    _pipelined_tiles(cnt_ref[qi] - first, lambda t: tile_at(first + t), produce_linear, consume,
                     first_produced=True)

    lam = (jnp.exp(jnp.sum(lq1_ref[...] * lk1_ref[...], axis=-1, keepdims=True))
           - jnp.exp(jnp.sum(lq2_ref[...] * lk2_ref[...], axis=-1, keepdims=True)) + lam_init)
    l0 = jnp.sum(l_sc[0], axis=-1, keepdims=True)
    l1 = jnp.sum(l_sc[1], axis=-1, keepdims=True)
    od = acc_sc[0] / l0 - lam * (acc_sc[1] / l1)
    o_ref[...] = (od * _rms(od, B_V) * (sub_ref[...] * (1.0 - lam_init))).astype(BF16)


def _b_attn(q, k, v, pos, consts, lam_init, online):
    s = q.shape[0]
    tq, tk = ATTN_TILES_ONLINE if online else ATTN_TILES
    nq, nk = s // tq, s // tk
    t = _tile_tables(pos, tq, tk)
    tables = (t["n_general"], t["n_visible"], t["order"], t["quarter_last_general"])
    pos_col, pos_k3 = pos.reshape(s, 1), pos.reshape(nk, 1, tk)
    slopes = consts[0]
    grid_spec = pltpu.PrefetchScalarGridSpec(
        num_scalar_prefetch=4,
        grid=(B_HEADS, nq),
        in_specs=[pl.BlockSpec((tq, B_HEAD_PAD), lambda h, i, *_: (i, h)),
                  pl.BlockSpec((s, B_HEAD_PAD), lambda h, i, *_: (0, h),
                               pipeline_mode=pl.Buffered(1)),
                  pl.BlockSpec((s, B_V), lambda h, i, *_: (0, h)),
                  pl.BlockSpec((tq, 1), lambda h, i, *_: (i, 0)),
                  pl.BlockSpec((nk, 1, tk), lambda h, i, *_: (0, 0, 0)),
                  pl.BlockSpec((1,) + slopes.shape[1:], lambda h, i, *_: (h, 0, 0))]
                 + [pl.BlockSpec(c.shape, lambda h, i, *_: (0, 0)) for c in consts[1:]],
        out_specs=pl.BlockSpec((tq, B_V), lambda h, i, *_: (i, h)),
        scratch_shapes=[pltpu.VMEM((2, 2, tq, tk), BF16), pltpu.VMEM((2, tq, B_V), F32),
                        pltpu.VMEM((2, tq, LANES), F32), pltpu.VMEM((2, tq, 1), F32),
                        pltpu.VMEM((2, 2, tq, 1), F32)],
    )
    return pl.pallas_call(
        functools.partial(_b_attn_kernel, tk=tk, nk=nk, lam_init=lam_init, online=online),
        grid_spec=grid_spec,
        out_shape=jax.ShapeDtypeStruct((s, B_WIDTH), BF16),
        compiler_params=pltpu.CompilerParams(dimension_semantics=("arbitrary", "arbitrary"),
                                             vmem_limit_bytes=56 * 1024 * 1024),
        name="b_attention_online" if online else "b_attention",
    )(*tables, q, k, v, pos_col, pos_k3, *consts)


def _attend(fn, shift_bound, *args, **kwargs):
    return lax.cond(shift_bound <= MAX_CONST_SHIFT,
                    lambda: fn(*args, online=False, **kwargs),
                    lambda: fn(*args, online=True, **kwargs))


def _lane_onehot(value, lane):
    return jnp.zeros((1, LANES), F32).at[0, lane].set(value)


def _a_consts(norm_g, w_in, q_norm_g, w_q_up, kv_norm_g, w_kv_up, q_gain, k_gain):
    n_lat = A_Q_LORA + A_KV_LORA + A_ROPE
    wa = jnp.pad(w_in[:, :n_lat], ((0, 0), (0, LANES - A_ROPE))).astype(BF16)
    wg = w_in[:, n_lat:].astype(BF16)
    wq = w_q_up.reshape(A_Q_LORA, A_HEADS, A_QK)
    wq = jnp.pad(wq, ((0, 0), (0, 0), (0, A_HEAD_PAD - A_QK)))
    wq = wq.reshape(A_Q_LORA, A_HEADS * A_HEAD_PAD).astype(BF16)
    wkv = w_kv_up.reshape(A_KV_LORA, A_HEADS, A_NOPE + A_V)
    wkv = jnp.concatenate([wkv[:, :, :A_NOPE].reshape(A_KV_LORA, -1),
                           wkv[:, :, A_NOPE:].reshape(A_KV_LORA, -1)], axis=1).astype(BF16)
    q_scale = LOG2E / math.sqrt(A_QK)
    gq = jnp.pad(q_gain * q_scale, (0, A_HEAD_PAD - A_QK)).reshape(1, A_HEAD_PAD)
    gkn = k_gain[:A_NOPE].reshape(1, LANES)
    gkp = jnp.pad(k_gain[A_NOPE:], (0, LANES - A_ROPE)).reshape(1, LANES)
    half = A_ROPE // 2
    inv_freq = ROPE_THETA ** (-jnp.arange(0, A_ROPE, 2, dtype=F32) / A_ROPE)
    zeros = jnp.zeros((LANES - A_ROPE,), F32)
    invf = jnp.concatenate([inv_freq, inv_freq, zeros]).reshape(1, LANES)
    sgn = jnp.concatenate([-jnp.ones((half,), F32), jnp.ones((half,), F32), zeros]).reshape(1, LANES)
    bound = (BOUND_MARGIN * A_QK * q_scale
             * jnp.max(jnp.abs(q_gain)) * jnp.max(jnp.abs(k_gain)))
    bound = bound.astype(BF16).astype(F32)
    qshift = _lane_onehot(-bound, A_ROPE)
    one = _lane_onehot(1.0, A_ROPE)
    consts = (norm_g.reshape(1, -1), wa, wg, q_norm_g.reshape(1, -1), wq, kv_norm_g.reshape(1, -1),
              wkv, gq, gkn, gkp, invf, sgn, qshift, one)
    return consts, bound


def _b_consts(q_gain, k_gain):
    b_scale = LOG2E / math.sqrt(B_HD)
    bound = (BOUND_MARGIN * B_HD * b_scale * jnp.max(jnp.abs(q_gain)) * jnp.max(jnp.abs(k_gain)))
    bound = bound.astype(BF16).astype(F32)
    slopes = 2.0 ** (-8.0 * jnp.arange(1, B_HEADS + 1, dtype=F32) / B_HEADS)
    sigma = LOG2E * slopes
    sig_hi = sigma.astype(BF16).astype(F32)
    sig_lo = (sigma - sig_hi).astype(BF16).astype(F32)
    big = float(1 << POS_SPLIT_SHIFT)
    sig_cols = jnp.stack([sig_hi * big, sig_hi, sig_lo * big, sig_lo], axis=1)
    zeros = lambda n: jnp.zeros((B_HEADS, n), F32)
    extc_q = jnp.concatenate([jnp.broadcast_to(-bound, (B_HEADS, 1)), sig_cols, zeros(LANES - 5)],
                             axis=1)
    extc_k = jnp.concatenate([jnp.ones((B_HEADS, 1), F32), zeros(4), sig_cols, zeros(LANES - 9)],
                             axis=1)
    lanes = jnp.arange(LANES)
    pick = lambda ids, val: jnp.where(jnp.isin(lanes, jnp.array(ids)), val, 0.0).reshape(1, LANES)
    ea_q, eb_q = pick([5, 7], -1.0), pick([6, 8], -1.0)
    ea_k, eb_k = pick([1, 3], 1.0), pick([2, 4], 1.0)
    neg_slopes = jnp.broadcast_to(-(sig_hi + sig_lo)[:, None, None], (B_HEADS, 1, LANES))
    return bound, b_scale, neg_slopes, (extc_q, ea_q, eb_q), (extc_k, ea_k, eb_k)


def kernel(x, positions, a_norm, a_w_in, a_q_norm, a_w_q_up, a_kv_norm, a_w_kv_up, a_q_gain, a_k_gain, a_w_out, b_norm, b_w_in, b_q_gain, b_k_gain, b_lambda_q1, b_lambda_k1, b_lambda_q2, b_lambda_k2, b_subln, b_w_out):
    batch, seq, _ = x.shape
    assert batch == 1
    xs = x[0]
    pos = positions[0]
    pos_col = pos.reshape(seq, 1)
    row = lambda a: a.reshape(1, -1)

    consts, a_bound = _a_consts(a_norm[0], a_w_in[0], a_q_norm[0], a_w_q_up[0], a_kv_norm[0],
                                a_w_kv_up[0], a_q_gain[0], a_k_gain[0])
    q, k, v, sg = _a_prologue(xs, pos_col, consts)
    o = _attend(_a_attn, a_bound, q, k, v, pos)
    xs = _out_proj(xs, o, sg, a_w_out[0].astype(BF16), name="a_out_proj")

    layer_idx = 1
    lam_init = 0.8 - 0.6 * math.exp(-0.3 * layer_idx)
    b_bound, b_scale, neg_slopes, q_ext, k_ext = _b_consts(b_q_gain[0], b_k_gain[0])
    w_in = b_w_in[0].astype(BF16)
    ng = row(b_norm[0])
    section = lambda j: (w_in, (D_MODEL, B_WIDTH), (0, j))
    rows = [(xs, D_MODEL, 0), (pos_col, 1, 0)]
    (q,) = _rows_call(_b_qk_kernel, rows, (ng, section(0), row(b_q_gain[0] * b_scale), *q_ext),
                      (B_HEADS * B_HEAD_PAD,), (BF16,), tm=1024, vmem_mib=56, name="b_q_proj")
    (k,) = _rows_call(_b_qk_kernel, rows, (ng, section(1), row(b_k_gain[0]), *k_ext),
                      (B_HEADS * B_HEAD_PAD,), (BF16,), tm=1024, vmem_mib=56, name="b_k_proj")
    v, sg = _rows_call(_b_vg_kernel, rows[:1], (ng, section(2), section(3)),
                       (B_WIDTH, B_WIDTH), (BF16, BF16), tm=512, vmem_mib=56, name="b_vg_proj")
    b_consts = (neg_slopes, jnp.broadcast_to(b_bound.reshape(1, 1), (1, LANES)),
                row(b_lambda_q1[0]), row(b_lambda_k1[0]), row(b_lambda_q2[0]), row(b_lambda_k2[0]),
                row(b_subln[0]))
    o = _attend(_b_attn, b_bound, q, k, v, pos, b_consts, lam_init)
    xs = _out_proj(xs, o, sg, b_w_out[0].astype(BF16), name="b_out_proj")
    return xs[None]
```

```python
import functools
import math

import jax
import jax.numpy as jnp
from jax import lax
from jax.experimental import pallas as pl
from jax.experimental.pallas import tpu as pltpu

D_MODEL = 2048
CHUNK_SHIFT = 6
EPS = 1e-6
LOG2E = 1.4426950408889634

A_HEADS = 16
A_NOPE = 128
A_ROPE = 64
A_QK = A_NOPE + A_ROPE
A_V = 128
A_Q_LORA = 512
A_KV_LORA = 512
A_WIDTH = A_HEADS * A_V
A_HEAD_PAD = 256
A_V_PAD = 256
A_ONE_COL = A_V + A_ROPE
ROPE_THETA = 10000.0

B_HEADS = 8
B_HD = 128
B_V = 2 * B_HD
B_WIDTH = B_HEADS * B_V
B_QK = B_HEADS * 2 * B_HD
B_HEAD_PAD = 3 * B_HD
POS_SPLIT_SHIFT = 7

LANES = 128
NEG_BIG = -1e30
MAX_CONST_SHIFT = 50.0
BOUND_MARGIN = 1.01
ATTN_TILES = (1024, 1024)
ATTN_TILES_ONLINE = (512, 1024)

F32 = jnp.float32
BF16 = jnp.bfloat16
_NT = (((1,), (1,)), ((), ()))


def _const_spec(a, block_shape=None, block_index=None):
    shape = a.shape if block_shape is None else block_shape
    index = (0,) * len(shape) if block_index is None else block_index
    return pl.BlockSpec(shape, lambda *_: index, pipeline_mode=pl.Buffered(1))


def _rms(x, denom):
    return lax.rsqrt(jnp.sum(x * x, axis=-1, keepdims=True) * (1.0 / denom) + EPS)


def _a_prologue_kernel(x_ref, pos_ref, ng_ref, wa_ref, wg_ref, qng_ref, wq_ref, kvng_ref, wkv_ref,
                       gq_ref, gkn_ref, gkp_ref, invf_ref, sgn_ref, qshift_ref, one_ref,
                       q_ref, k_ref, v_ref, sg_ref):
    x = x_ref[...]
    xn = (x * _rms(x, D_MODEL) * ng_ref[...]).astype(BF16)
    za = jnp.dot(xn, wa_ref[...], preferred_element_type=F32)
    gate = jnp.dot(xn, wg_ref[...], preferred_element_type=F32)
    sg_ref[...] = (gate * jax.nn.sigmoid(gate)).astype(BF16)
    rope = _rope_fn(pos_ref, invf_ref, sgn_ref)
    _a_q_rows(za[:, :A_Q_LORA], rope, qng_ref, wq_ref, gq_ref, qshift_ref, q_ref)
    _a_kv_rows(za[:, A_Q_LORA:A_Q_LORA + A_KV_LORA], za[:, A_Q_LORA + A_KV_LORA:], rope,
               kvng_ref, wkv_ref, gkn_ref, gkp_ref, one_ref, k_ref, v_ref)


def _rope_fn(pos_ref, invf_ref, sgn_ref):
    ang = pos_ref[...].astype(F32) * invf_ref[...]
    cos = jnp.cos(ang)
    sin_signed = jnp.sin(ang) * sgn_ref[...]
    lane = lax.broadcasted_iota(jnp.int32, cos.shape, 1)
    first_half = lane < (A_ROPE // 2)

    def rope(t):
        swapped = jnp.where(first_half, pltpu.roll(t, LANES - A_ROPE // 2, 1),
                            pltpu.roll(t, A_ROPE // 2, 1))
        return t * cos + swapped * sin_signed

    return rope


def _a_q_rows(cq, rope, qng_ref, wq_ref, gq_ref, qshift_ref, q_ref):
    cqn = (cq * _rms(cq, A_Q_LORA) * qng_ref[...]).astype(BF16)
    q = jnp.dot(cqn, wq_ref[...], preferred_element_type=F32)
    gq = gq_ref[...]
    qshift = qshift_ref[...]
    for h in range(A_HEADS):
        qh = q[:, h * A_HEAD_PAD:(h + 1) * A_HEAD_PAD]
        qn = qh * _rms(qh, A_QK) * gq
        q_ref[:, h * A_HEAD_PAD:h * A_HEAD_PAD + LANES] = qn[:, :LANES].astype(BF16)
        q_ref[:, h * A_HEAD_PAD + LANES:(h + 1) * A_HEAD_PAD] = (
            rope(qn[:, LANES:]) + qshift).astype(BF16)


def _a_kv_rows(ckv, kpe, rope, kvng_ref, wkv_ref, gkn_ref, gkp_ref, one_ref, k_ref, v_ref):
    ckvn = (ckv * _rms(ckv, A_KV_LORA) * kvng_ref[...]).astype(BF16)
    kv = jnp.dot(ckvn, wkv_ref[...], preferred_element_type=F32)
    gkn = gkn_ref[...]
    one = one_ref[...]
    v_one = jnp.broadcast_to(one, kpe.shape).astype(BF16)
    kpe_ss = jnp.sum(kpe * kpe, axis=-1, keepdims=True)
    kpe_roped = rope(kpe * gkp_ref[...])
    for h in range(A_HEADS):
        kn = kv[:, h * A_NOPE:(h + 1) * A_NOPE]
        r = lax.rsqrt((jnp.sum(kn * kn, axis=-1, keepdims=True) + kpe_ss) * (1.0 / A_QK) + EPS)
        k_ref[:, h * A_HEAD_PAD:h * A_HEAD_PAD + LANES] = (kn * r * gkn).astype(BF16)
        k_ref[:, h * A_HEAD_PAD + LANES:(h + 1) * A_HEAD_PAD] = (kpe_roped * r + one).astype(BF16)

        v_ref[:, h * A_V_PAD:h * A_V_PAD + A_V] = (
            kv[:, A_HEADS * A_NOPE + h * A_V:A_HEADS * A_NOPE + (h + 1) * A_V].astype(BF16))
        v_ref[:, h * A_V_PAD + A_V:(h + 1) * A_V_PAD] = v_one


def _rows_call(kernel_fn, row_inputs, consts, out_widths, out_dtypes, tm, vmem_mib, name):
    s = row_inputs[0][0].shape[0]
    in_specs = [pl.BlockSpec((tm, w), lambda i, cb=cb: (i, cb)) for _, w, cb in row_inputs]
    in_specs += [_const_spec(*c) if isinstance(c, tuple) else _const_spec(c) for c in consts]
    out_specs = [pl.BlockSpec((tm, w), lambda i: (i, 0)) for w in out_widths]
    out_shape = [jax.ShapeDtypeStruct((s, w), dt) for w, dt in zip(out_widths, out_dtypes)]
    return pl.pallas_call(
        kernel_fn,
        grid=(s // tm,),
        in_specs=in_specs,
        out_specs=out_specs,
        out_shape=out_shape,
        compiler_params=pltpu.CompilerParams(dimension_semantics=("arbitrary",),
                                             vmem_limit_bytes=vmem_mib * 1024 * 1024),
        name=name,
    )(*[a for a, _, _ in row_inputs], *[c[0] if isinstance(c, tuple) else c for c in consts])


def _a_prologue(x, pos_col, consts):
    wide = A_HEADS * A_HEAD_PAD
    return _rows_call(_a_prologue_kernel, [(x, D_MODEL, 0), (pos_col, 1, 0)], consts,
                      (wide, wide, A_HEADS * A_V_PAD, A_WIDTH), (BF16, BF16, BF16, BF16),
                      tm=256, vmem_mib=58, name="a_prologue")


def _b_qk_kernel(x_ref, pos_ref, ng_ref, w_ref, g_ref, extc_ref, ea_ref, eb_ref, out_ref):
    x = x_ref[...]
    xn = (x * _rms(x, D_MODEL) * ng_ref[...]).astype(BF16)
    z = jnp.dot(xn, w_ref[...], preferred_element_type=F32)
    pos = pos_ref[...]
    a = jnp.right_shift(pos, POS_SPLIT_SHIFT).astype(F32)
    b = jnp.bitwise_and(pos, (1 << POS_SPLIT_SHIFT) - 1).astype(F32)
    ext_ab = a * ea_ref[...] + b * eb_ref[...]
    g = g_ref[...]
    for h in range(B_HEADS):
        for c in range(2):
            t = z[:, (2 * h + c) * B_HD:(2 * h + c + 1) * B_HD]
            out_ref[:, h * B_HEAD_PAD + c * B_HD:h * B_HEAD_PAD + (c + 1) * B_HD] = (
                t * _rms(t, B_HD) * g).astype(BF16)
        out_ref[:, h * B_HEAD_PAD + 2 * B_HD:(h + 1) * B_HEAD_PAD] = (
            extc_ref[h:h + 1, :] + ext_ab).astype(BF16)


def _b_vg_kernel(x_ref, ng_ref, wv_ref, wg_ref, v_ref, sg_ref):
    x = x_ref[...]
    xn = (x * _rms(x, D_MODEL) * ng_ref[...]).astype(BF16)
    v_ref[...] = jnp.dot(xn, wv_ref[...], preferred_element_type=F32).astype(BF16)
    gate = jnp.dot(xn, wg_ref[...], preferred_element_type=F32)
    sg_ref[...] = (gate * jax.nn.sigmoid(gate)).astype(BF16)


def _out_proj_kernel(x_ref, o_ref, sg_ref, w_ref, y_ref):
    a = o_ref[...] * sg_ref[...]
    y_ref[...] = x_ref[...] + jnp.dot(a, w_ref[...], preferred_element_type=F32)


def _out_proj(x, o, sg, w, name):
    (y,) = _rows_call(_out_proj_kernel, [(x, D_MODEL, 0), (o, D_MODEL, 0), (sg, D_MODEL, 0)], (w,),
                      (D_MODEL,), (F32,), tm=512, vmem_mib=48, name=name)
    return y


def _tile_tables(pos, tq, tk):
    s = pos.shape[0]
    nq, nk = s // tq, s // tk
    ch = jnp.right_shift(pos, CHUNK_SHIFT)
    qmax_ch = ch.reshape(nq, tq).max(axis=1)
    kmin_ch = ch.reshape(nk, tk).min(axis=1)
    qmin = pos.reshape(nq, tq).min(axis=1)
    kmax = pos.reshape(nk, tk).max(axis=1)
    in_range = (pos.min() >= 0) & (pos.max() < (1 << (POS_SPLIT_SHIFT + 8)))
    visible = kmin_ch[None, :] <= qmax_ch[:, None]
    linear = visible & (kmax[None, :] <= qmin[:, None]) & in_range
    general = visible & jnp.logical_not(linear)
    rank = jnp.where(general, 0, jnp.where(linear, 1, 2))
    order = jnp.argsort(rank, axis=1, stable=True)
    n_general = general.sum(axis=1)
    n_visible = visible.sum(axis=1)
    qmax_top = ch.reshape(nq, 2, tq // 2)[:, 0].max(axis=1)
    kmin_right = ch.reshape(nk, 2, tk // 2)[:, 1].min(axis=1)
    quarter = kmin_right[None, :] > qmax_top[:, None]
    last_general = jnp.take_along_axis(order, jnp.maximum(n_general - 1, 0)[:, None], axis=1)
    quarter_last_general = jnp.take_along_axis(quarter, last_general, axis=1)[:, 0] & (n_general > 0)
    i32 = lambda a: a.astype(jnp.int32)
    return dict(n_general=i32(n_general), n_visible=i32(n_visible), order=i32(order).reshape(-1),
                quarter_last_general=i32(quarter_last_general))


def _pipelined_tiles(n, tile_at, produce, consume, first_produced=False):
    if not first_produced:
        produce(tile_at(0), 0)

    def two_steps(t):
        produce(tile_at(t + 1), 1)
        consume(tile_at(t), 0)
        produce(tile_at(t + 2), 0)
        consume(tile_at(t + 1), 1)

    def quad(u, c):
        two_steps(4 * u)
        two_steps(4 * u + 2)
        return c

    n_quads = (n - 1) // 4
    lax.fori_loop(0, n_quads, quad, 0)
    done = 4 * n_quads

    def pair(u, c):
        two_steps(done + 2 * u)
        return c

    lax.fori_loop(0, (n - 1 - done) // 2, pair, 0)
    last = n - 1

    @pl.when(last % 2 == 1)
    def _():
        produce(tile_at(last), 1)
        consume(tile_at(last - 1), 0)
        consume(tile_at(last), 1)

    @pl.when(last % 2 == 0)
    def _():
        consume(tile_at(last), 0)


def _lane_group_sum(p):
    out = p[:, :LANES]
    for j in range(1, p.shape[1] // LANES):
        out = out + p[:, j * LANES:(j + 1) * LANES]
    return out


def _a_attn_kernel(cnt_ref, list_ref, q_ref, k_ref, v_ref, pq_ref, pk_ref, o_ref,
                   p_sc, acc_sc, m_sc, alpha_sc, *, tk, nk, online):
    qi = pl.program_id(1)
    q = q_ref[...]
    qchunk = jnp.right_shift(pq_ref[...], CHUNK_SHIFT)
    acc_sc[...] = jnp.zeros_like(acc_sc)
    if online:
        m_sc[...] = jnp.full_like(m_sc, NEG_BIG)

    def produce(ki, slot):
        k = k_ref[pl.ds(pl.multiple_of(ki * tk, tk), tk), :]
        s = lax.dot_general(q, k, _NT, preferred_element_type=F32)
        kchunk = jnp.right_shift(pk_ref[ki], CHUNK_SHIFT)
        s = jnp.where(kchunk <= qchunk, s, NEG_BIG)
        if online:
            m_prev = m_sc[...]
            m_new = jnp.maximum(m_prev, jnp.max(s, axis=1, keepdims=True))
            alpha_sc[slot] = jnp.exp2(m_prev - m_new)
            m_sc[...] = m_new
            s = s - m_new
        p_sc[slot] = jnp.exp2(s).astype(BF16)

    def consume(ki, slot):
        v = v_ref[pl.ds(pl.multiple_of(ki * tk, tk), tk), :]
        pv = jnp.dot(p_sc[slot], v, preferred_element_type=F32)
        if online:
            acc_sc[...] = alpha_sc[slot] * acc_sc[...] + pv
        else:
            acc_sc[...] += pv

    _pipelined_tiles(cnt_ref[qi], lambda t: list_ref[qi * nk + t], produce, consume)
    acc = acc_sc[...]
    o_ref[...] = (acc[:, :A_V] / acc[:, A_ONE_COL:A_ONE_COL + 1]).astype(BF16)


def _a_attn(q, k, v, pos, shift_tables, online):
    s = q.shape[0]
    tq, tk = ATTN_TILES_ONLINE if online else ATTN_TILES
    nq, nk = s // tq, s // tk
    t = _tile_tables(pos, tq, tk) if online else shift_tables
    tables = (t["n_visible"], t["order"])
    pos_col, pos_k3 = pos.reshape(s, 1), pos.reshape(nk, 1, tk)
    grid_spec = pltpu.PrefetchScalarGridSpec(
        num_scalar_prefetch=2,
        grid=(A_HEADS, nq),
        in_specs=[pl.BlockSpec((tq, A_HEAD_PAD), lambda h, i, *_: (i, h)),
                  pl.BlockSpec((s, A_HEAD_PAD), lambda h, i, *_: (0, h)),
                  pl.BlockSpec((s, A_V_PAD), lambda h, i, *_: (0, h)),
                  pl.BlockSpec((tq, 1), lambda h, i, *_: (i, 0)),
                  pl.BlockSpec((nk, 1, tk), lambda h, i, *_: (0, 0, 0))],
        out_specs=pl.BlockSpec((tq, A_V), lambda h, i, *_: (i, h)),
        scratch_shapes=[pltpu.VMEM((2, tq, tk), BF16), pltpu.VMEM((tq, A_V_PAD), F32),
                        pltpu.VMEM((tq, 1), F32), pltpu.VMEM((2, tq, 1), F32)],
    )
    return pl.pallas_call(
        functools.partial(_a_attn_kernel, tk=tk, nk=nk, online=online),
        grid_spec=grid_spec,
        out_shape=jax.ShapeDtypeStruct((s, A_WIDTH), BF16),
        compiler_params=pltpu.CompilerParams(dimension_semantics=("arbitrary", "arbitrary"),
                                             vmem_limit_bytes=56 * 1024 * 1024),
        name="a_attention_online" if online else "a_attention",
    )(*tables, q, k, v, pos_col, pos_k3)


def _b_attn_kernel(gcnt_ref, cnt_ref, list_ref, quarter_ref, q_ref, k_ref, v_ref, pq_ref, pk_ref,
                   slope_ref, shift_ref, lq1_ref, lk1_ref, lq2_ref, lk2_ref, sub_ref, o_ref,
                   p_sc, acc_sc, l_sc, m_sc, alpha_sc, *, tk, nk, lam_init, online):
    qi = pl.program_id(1)
    q = q_ref[...]
    tq = q.shape[0]
    hq, hk = tq // 2, tk // 2
    q_ext = q[:, 2 * B_HD:]
    q_full = [jnp.concatenate([q[:, c * B_HD:(c + 1) * B_HD], q_ext], axis=1) for c in range(2)]
    pq = pq_ref[...]
    qchunk = jnp.right_shift(pq, CHUNK_SHIFT)
    neg_slope = slope_ref[0, :, 0:1]
    shift = shift_ref[0:1, 0:1]
    l_sc[...] = jnp.zeros_like(l_sc)
    acc_sc[...] = jnp.zeros_like(acc_sc)
    if online:
        m_sc[...] = jnp.full_like(m_sc, NEG_BIG)

    def finish(s, c, slot):
        if online:
            m_prev = m_sc[c]
            m_new = jnp.maximum(m_prev, jnp.max(s, axis=1, keepdims=True))
            alpha = jnp.exp2(m_prev - m_new)
            alpha_sc[slot, c] = alpha
            m_sc[c] = m_new
            p = jnp.exp2(s - m_new)
            l_sc[c] = alpha * l_sc[c] + _lane_group_sum(p)
        else:
            p = jnp.exp2(s)
            l_sc[c] += _lane_group_sum(p)
        p_sc[slot, c] = p.astype(BF16)

    def produce_linear(ki, slot):
        k = k_ref[pl.ds(pl.multiple_of(ki * tk, tk), tk), :]
        k_ext = k[:, 2 * B_HD:]
        for c in range(2):
            k_full = jnp.concatenate([k[:, c * B_HD:(c + 1) * B_HD], k_ext], axis=1)
            finish(lax.dot_general(q_full[c], k_full, _NT, preferred_element_type=F32), c, slot)

    def produce_general(ki, slot):
        k = k_ref[pl.ds(pl.multiple_of(ki * tk, tk), tk), :]
        pk = pk_ref[ki]
        bias = jnp.abs(pq - pk).astype(F32) * neg_slope - shift
        bias = jnp.where(jnp.right_shift(pk, CHUNK_SHIFT) <= qchunk, bias, NEG_BIG)
        for c in range(2):
            s = lax.dot_general(q[:, c * B_HD:(c + 1) * B_HD], k[:, c * B_HD:(c + 1) * B_HD],
                                _NT, preferred_element_type=F32) + bias
            finish(s, c, slot)

    def produce_general_skip_quarter(ki, slot):
        start = pl.multiple_of(ki * tk, tk)
        pk = pk_ref[ki]
        regions = ((slice(0, tq), slice(0, hk)), (slice(hq, tq), slice(hk, tk)))
        for rows, cols in regions:
            k = k_ref[pl.ds(start + cols.start, hk), :]
            pkc = pk[:, cols]
            bias = jnp.abs(pq[rows] - pkc).astype(F32) * neg_slope - shift
            bias = jnp.where(jnp.right_shift(pkc, CHUNK_SHIFT) <= qchunk[rows], bias, NEG_BIG)
            for c in range(2):
                s = lax.dot_general(q[rows, c * B_HD:(c + 1) * B_HD], k[:, c * B_HD:(c + 1) * B_HD],
                                    _NT, preferred_element_type=F32) + bias
                p = jnp.exp2(s)
                l_sc[c, rows, :] += _lane_group_sum(p)
                p_sc[slot, c, rows, cols] = p.astype(BF16)
        for c in range(2):
            p_sc[slot, c, :hq, hk:] = jnp.zeros((hq, hk), BF16)

    def consume(ki, slot):
        v = v_ref[pl.ds(pl.multiple_of(ki * tk, tk), tk), :]
        for c in range(2):
            pv = jnp.dot(p_sc[slot, c], v, preferred_element_type=F32)
            if online:
                acc_sc[c] = alpha_sc[slot, c] * acc_sc[c] + pv
            else:
                acc_sc[c] += pv

    n_general = gcnt_ref[qi]
    tile_at = lambda t: list_ref[qi * nk + t]

    def general_body(t, c):
        produce_general(tile_at(t), 0)
        consume(tile_at(t), 0)
        return c

    first = jnp.maximum(n_general - 1, 0)
    lax.fori_loop(0, first, general_body, 0)

    skip_quarter = False if online else quarter_ref[qi] != 0

    @pl.when((n_general > 0) & jnp.logical_not(skip_quarter))
    def _():
        produce_general(tile_at(first), 0)

    if not online:
        @pl.when((n_general > 0) & skip_quarter)
        def _():
            produce_general_skip_quarter(tile_at(first), 0)

    @pl.when(n_general == 0)
    def _():
        produce_linear(tile_at(first), 0)

    _pipelined_tiles(cnt_ref[qi] - first, lambda t: tile_at(first + t), produce_linear, consume,
                     first_produced=True)

    lam = (jnp.exp(jnp.sum(lq1_ref[...] * lk1_ref[...], axis=-1, keepdims=True))
           - jnp.exp(jnp.sum(lq2_ref[...] * lk2_ref[...], axis=-1, keepdims=True)) + lam_init)
    l0 = jnp.sum(l_sc[0], axis=-1, keepdims=True)
    l1 = jnp.sum(l_sc[1], axis=-1, keepdims=True)
    od = acc_sc[0] / l0 - lam * (acc_sc[1] / l1)
    o_ref[...] = (od * _rms(od, B_V) * (sub_ref[...] * (1.0 - lam_init))).astype(BF16)


def _b_attn(q, k, v, pos, shift_tables, consts, lam_init, online):
    s = q.shape[0]
    tq, tk = ATTN_TILES_ONLINE if online else ATTN_TILES
    nq, nk = s // tq, s // tk
    t = _tile_tables(pos, tq, tk) if online else shift_tables
    tables = (t["n_general"], t["n_visible"], t["order"], t["quarter_last_general"])
    pos_col, pos_k3 = pos.reshape(s, 1), pos.reshape(nk, 1, tk)
    slopes = consts[0]
    grid_spec = pltpu.PrefetchScalarGridSpec(
        num_scalar_prefetch=4,
        grid=(B_HEADS, nq),
        in_specs=[pl.BlockSpec((tq, B_HEAD_PAD), lambda h, i, *_: (i, h)),
                  pl.BlockSpec((s, B_HEAD_PAD), lambda h, i, *_: (0, h),
                               pipeline_mode=pl.Buffered(1)),
                  pl.BlockSpec((s, B_V), lambda h, i, *_: (0, h)),
                  pl.BlockSpec((tq, 1), lambda h, i, *_: (i, 0)),
                  pl.BlockSpec((nk, 1, tk), lambda h, i, *_: (0, 0, 0)),
                  pl.BlockSpec((1,) + slopes.shape[1:], lambda h, i, *_: (h, 0, 0))]
                 + [pl.BlockSpec(c.shape, lambda h, i, *_: (0, 0)) for c in consts[1:]],
        out_specs=pl.BlockSpec((tq, B_V), lambda h, i, *_: (i, h)),
        scratch_shapes=[pltpu.VMEM((2, 2, tq, tk), BF16), pltpu.VMEM((2, tq, B_V), F32),
                        pltpu.VMEM((2, tq, LANES), F32), pltpu.VMEM((2, tq, 1), F32),
                        pltpu.VMEM((2, 2, tq, 1), F32)],
    )
    return pl.pallas_call(
        functools.partial(_b_attn_kernel, tk=tk, nk=nk, lam_init=lam_init, online=online),
        grid_spec=grid_spec,
        out_shape=jax.ShapeDtypeStruct((s, B_WIDTH), BF16),
        compiler_params=pltpu.CompilerParams(dimension_semantics=("arbitrary", "arbitrary"),
                                             vmem_limit_bytes=56 * 1024 * 1024),
        name="b_attention_online" if online else "b_attention",
    )(*tables, q, k, v, pos_col, pos_k3, *consts)


def _attend(fn, shift_bound, *args, **kwargs):
    return lax.cond(shift_bound <= MAX_CONST_SHIFT,
                    lambda: fn(*args, online=False, **kwargs),
                    lambda: fn(*args, online=True, **kwargs))


def _lane_onehot(value, lane):
    return jnp.zeros((1, LANES), F32).at[0, lane].set(value)


def _a_consts(norm_g, w_in, q_norm_g, w_q_up, kv_norm_g, w_kv_up, q_gain, k_gain):
    n_lat = A_Q_LORA + A_KV_LORA + A_ROPE
    wa = jnp.pad(w_in[:, :n_lat], ((0, 0), (0, LANES - A_ROPE))).astype(BF16)
    wg = w_in[:, n_lat:].astype(BF16)
    wq = w_q_up.reshape(A_Q_LORA, A_HEADS, A_QK)
    wq = jnp.pad(wq, ((0, 0), (0, 0), (0, A_HEAD_PAD - A_QK)))
    wq = wq.reshape(A_Q_LORA, A_HEADS * A_HEAD_PAD).astype(BF16)
    wkv = w_kv_up.reshape(A_KV_LORA, A_HEADS, A_NOPE + A_V)
    wkv = jnp.concatenate([wkv[:, :, :A_NOPE].reshape(A_KV_LORA, -1),
                           wkv[:, :, A_NOPE:].reshape(A_KV_LORA, -1)], axis=1).astype(BF16)
    q_scale = LOG2E / math.sqrt(A_QK)
    gq = jnp.pad(q_gain * q_scale, (0, A_HEAD_PAD - A_QK)).reshape(1, A_HEAD_PAD)
    gkn = k_gain[:A_NOPE].reshape(1, LANES)
    gkp = jnp.pad(k_gain[A_NOPE:], (0, LANES - A_ROPE)).reshape(1, LANES)
    half = A_ROPE // 2
    inv_freq = ROPE_THETA ** (-jnp.arange(0, A_ROPE, 2, dtype=F32) / A_ROPE)
    zeros = jnp.zeros((LANES - A_ROPE,), F32)
    invf = jnp.concatenate([inv_freq, inv_freq, zeros]).reshape(1, LANES)
    sgn = jnp.concatenate([-jnp.ones((half,), F32), jnp.ones((half,), F32), zeros]).reshape(1, LANES)
    bound = (BOUND_MARGIN * A_QK * q_scale
             * jnp.max(jnp.abs(q_gain)) * jnp.max(jnp.abs(k_gain)))
    bound = bound.astype(BF16).astype(F32)
    qshift = _lane_onehot(-bound, A_ROPE)
    one = _lane_onehot(1.0, A_ROPE)
    consts = (norm_g.reshape(1, -1), wa, wg, q_norm_g.reshape(1, -1), wq, kv_norm_g.reshape(1, -1),
              wkv, gq, gkn, gkp, invf, sgn, qshift, one)
    return consts, bound


def _b_consts(q_gain, k_gain):
    b_scale = LOG2E / math.sqrt(B_HD)
    bound = (BOUND_MARGIN * B_HD * b_scale * jnp.max(jnp.abs(q_gain)) * jnp.max(jnp.abs(k_gain)))
    bound = bound.astype(BF16).astype(F32)
    slopes = 2.0 ** (-8.0 * jnp.arange(1, B_HEADS + 1, dtype=F32) / B_HEADS)
    sigma = LOG2E * slopes
    sig_hi = sigma.astype(BF16).astype(F32)
    sig_lo = (sigma - sig_hi).astype(BF16).astype(F32)
    big = float(1 << POS_SPLIT_SHIFT)
    sig_cols = jnp.stack([sig_hi * big, sig_hi, sig_lo * big, sig_lo], axis=1)
    zeros = lambda n: jnp.zeros((B_HEADS, n), F32)
    extc_q = jnp.concatenate([jnp.broadcast_to(-bound, (B_HEADS, 1)), sig_cols, zeros(LANES - 5)],
                             axis=1)
    extc_k = jnp.concatenate([jnp.ones((B_HEADS, 1), F32), zeros(4), sig_cols, zeros(LANES - 9)],
                             axis=1)
    lanes = jnp.arange(LANES)
    pick = lambda ids, val: jnp.where(jnp.isin(lanes, jnp.array(ids)), val, 0.0).reshape(1, LANES)
    ea_q, eb_q = pick([5, 7], -1.0), pick([6, 8], -1.0)
    ea_k, eb_k = pick([1, 3], 1.0), pick([2, 4], 1.0)
    neg_slopes = jnp.broadcast_to(-(sig_hi + sig_lo)[:, None, None], (B_HEADS, 1, LANES))
    return bound, b_scale, neg_slopes, (extc_q, ea_q, eb_q), (extc_k, ea_k, eb_k)


def kernel(x, positions, a_norm, a_w_in, a_q_norm, a_w_q_up, a_kv_norm, a_w_kv_up, a_q_gain, a_k_gain, a_w_out, b_norm, b_w_in, b_q_gain, b_k_gain, b_lambda_q1, b_lambda_k1, b_lambda_q2, b_lambda_k2, b_subln, b_w_out):
    batch, seq, _ = x.shape
    assert batch == 1
    xs = x[0]
    pos = positions[0]
    pos_col = pos.reshape(seq, 1)
    row = lambda a: a.reshape(1, -1)

    consts, a_bound = _a_consts(a_norm[0], a_w_in[0], a_q_norm[0], a_w_q_up[0], a_kv_norm[0],
                                a_w_kv_up[0], a_q_gain[0], a_k_gain[0])
    q, k, v, sg = _a_prologue(xs, pos_col, consts)
    shift_tables = _tile_tables(pos, *ATTN_TILES)
    o = _attend(_a_attn, a_bound, q, k, v, pos, shift_tables)
    xs = _out_proj(xs, o, sg, a_w_out[0].astype(BF16), name="a_out_proj")

    layer_idx = 1
    lam_init = 0.8 - 0.6 * math.exp(-0.3 * layer_idx)
    b_bound, b_scale, neg_slopes, q_ext, k_ext = _b_consts(b_q_gain[0], b_k_gain[0])
    w_in = b_w_in[0].astype(BF16)
    ng = row(b_norm[0])
    section = lambda j: (w_in, (D_MODEL, B_WIDTH), (0, j))
    rows = [(xs, D_MODEL, 0), (pos_col, 1, 0)]
    (q,) = _rows_call(_b_qk_kernel, rows, (ng, section(0), row(b_q_gain[0] * b_scale), *q_ext),
                      (B_HEADS * B_HEAD_PAD,), (BF16,), tm=1024, vmem_mib=56, name="b_q_proj")
    (k,) = _rows_call(_b_qk_kernel, rows, (ng, section(1), row(b_k_gain[0]), *k_ext),
                      (B_HEADS * B_HEAD_PAD,), (BF16,), tm=1024, vmem_mib=56, name="b_k_proj")
    v, sg = _rows_call(_b_vg_kernel, rows[:1], (ng, section(2), section(3)),
                       (B_WIDTH, B_WIDTH), (BF16, BF16), tm=512, vmem_mib=56, name="b_vg_proj")
    b_consts = (neg_slopes, jnp.broadcast_to(b_bound.reshape(1, 1), (1, LANES)),
                row(b_lambda_q1[0]), row(b_lambda_k1[0]), row(b_lambda_q2[0]), row(b_lambda_k2[0]),
                row(b_subln[0]))
    o = _attend(_b_attn, b_bound, q, k, v, pos, shift_tables, b_consts, lam_init)
    xs = _out_proj(xs, o, sg, b_w_out[0].astype(BF16), name="b_out_proj")
    return xs[None]
```

```python
import functools
import math

import jax
import jax.numpy as jnp
from jax import lax
from jax.experimental import pallas as pl
from jax.experimental.pallas import tpu as pltpu

D_MODEL = 2048
CHUNK_SHIFT = 6
EPS = 1e-6
LOG2E = 1.4426950408889634

A_HEADS = 16
A_NOPE = 128
A_ROPE = 64
A_QK = A_NOPE + A_ROPE
A_V = 128
A_Q_LORA = 512
A_KV_LORA = 512
A_WIDTH = A_HEADS * A_V
A_HEAD_PAD = 256
A_V_PAD = 256
A_ONE_COL = A_V + A_ROPE
ROPE_THETA = 10000.0

B_HEADS = 8
B_HD = 128
B_V = 2 * B_HD
B_WIDTH = B_HEADS * B_V
B_QK = B_HEADS * 2 * B_HD
B_HEAD_PAD = 3 * B_HD
POS_SPLIT_SHIFT = 7

LANES = 128
NEG_BIG = -1e30
MAX_CONST_SHIFT = 50.0
BOUND_MARGIN = 1.01
ATTN_TILES = (1024, 1024)
ATTN_TILES_ONLINE = (512, 1024)
A_QUERY_GROUP = 4

F32 = jnp.float32
BF16 = jnp.bfloat16
_NT = (((1,), (1,)), ((), ()))


def _const_spec(a, block_shape=None, block_index=None):
    shape = a.shape if block_shape is None else block_shape
    index = (0,) * len(shape) if block_index is None else block_index
    return pl.BlockSpec(shape, lambda *_: index, pipeline_mode=pl.Buffered(1))


def _rms(x, denom):
    return lax.rsqrt(jnp.sum(x * x, axis=-1, keepdims=True) * (1.0 / denom) + EPS)


def _a_prologue_kernel(x_ref, pos_ref, ng_ref, wa_ref, wg_ref, qng_ref, wq_ref, kvng_ref, wkv_ref,
                       gq_ref, gkn_ref, gkp_ref, invf_ref, sgn_ref, qshift_ref, one_ref,
                       q_ref, k_ref, v_ref, sg_ref):
    x = x_ref[...]
    xn = (x * _rms(x, D_MODEL) * ng_ref[...]).astype(BF16)
    za = jnp.dot(xn, wa_ref[...], preferred_element_type=F32)
    gate = jnp.dot(xn, wg_ref[...], preferred_element_type=F32)
    sg_ref[...] = (gate * jax.nn.sigmoid(gate)).astype(BF16)
    rope = _rope_fn(pos_ref, invf_ref, sgn_ref)
    _a_q_rows(za[:, :A_Q_LORA], rope, qng_ref, wq_ref, gq_ref, qshift_ref, q_ref)
    _a_kv_rows(za[:, A_Q_LORA:A_Q_LORA + A_KV_LORA], za[:, A_Q_LORA + A_KV_LORA:], rope,
               kvng_ref, wkv_ref, gkn_ref, gkp_ref, one_ref, k_ref, v_ref)


def _rope_fn(pos_ref, invf_ref, sgn_ref):
    ang = pos_ref[...].astype(F32) * invf_ref[...]
    cos = jnp.cos(ang)
    sin_signed = jnp.sin(ang) * sgn_ref[...]
    lane = lax.broadcasted_iota(jnp.int32, cos.shape, 1)
    first_half = lane < (A_ROPE // 2)

    def rope(t):
        swapped = jnp.where(first_half, pltpu.roll(t, LANES - A_ROPE // 2, 1),
                            pltpu.roll(t, A_ROPE // 2, 1))
        return t * cos + swapped * sin_signed

    return rope


def _a_q_rows(cq, rope, qng_ref, wq_ref, gq_ref, qshift_ref, q_ref):
    cqn = (cq * _rms(cq, A_Q_LORA) * qng_ref[...]).astype(BF16)
    q = jnp.dot(cqn, wq_ref[...], preferred_element_type=F32)
    gq = gq_ref[...]
    qshift = qshift_ref[...]
    for h in range(A_HEADS):
        qh = q[:, h * A_HEAD_PAD:(h + 1) * A_HEAD_PAD]
        qn = qh * _rms(qh, A_QK) * gq
        q_ref[:, h * A_HEAD_PAD:h * A_HEAD_PAD + LANES] = qn[:, :LANES].astype(BF16)
        q_ref[:, h * A_HEAD_PAD + LANES:(h + 1) * A_HEAD_PAD] = (
            rope(qn[:, LANES:]) + qshift).astype(BF16)


def _a_kv_rows(ckv, kpe, rope, kvng_ref, wkv_ref, gkn_ref, gkp_ref, one_ref, k_ref, v_ref):
    ckvn = (ckv * _rms(ckv, A_KV_LORA) * kvng_ref[...]).astype(BF16)
    kv = jnp.dot(ckvn, wkv_ref[...], preferred_element_type=F32)
    gkn = gkn_ref[...]
    one = one_ref[...]
    v_one = jnp.broadcast_to(one, kpe.shape).astype(BF16)
    kpe_ss = jnp.sum(kpe * kpe, axis=-1, keepdims=True)
    kpe_roped = rope(kpe * gkp_ref[...])
    for h in range(A_HEADS):
        kn = kv[:, h * A_NOPE:(h + 1) * A_NOPE]
        r = lax.rsqrt((jnp.sum(kn * kn, axis=-1, keepdims=True) + kpe_ss) * (1.0 / A_QK) + EPS)
        k_ref[:, h * A_HEAD_PAD:h * A_HEAD_PAD + LANES] = (kn * r * gkn).astype(BF16)
        k_ref[:, h * A_HEAD_PAD + LANES:(h + 1) * A_HEAD_PAD] = (kpe_roped * r + one).astype(BF16)

        v_ref[:, h * A_V_PAD:h * A_V_PAD + A_V] = (
            kv[:, A_HEADS * A_NOPE + h * A_V:A_HEADS * A_NOPE + (h + 1) * A_V].astype(BF16))
        v_ref[:, h * A_V_PAD + A_V:(h + 1) * A_V_PAD] = v_one


def _rows_call(kernel_fn, row_inputs, consts, out_widths, out_dtypes, tm, vmem_mib, name):
    s = row_inputs[0][0].shape[0]
    in_specs = [pl.BlockSpec((tm, w), lambda i, cb=cb: (i, cb)) for _, w, cb in row_inputs]
    in_specs += [_const_spec(*c) if isinstance(c, tuple) else _const_spec(c) for c in consts]
    out_specs = [pl.BlockSpec((tm, w), lambda i: (i, 0)) for w in out_widths]
    out_shape = [jax.ShapeDtypeStruct((s, w), dt) for w, dt in zip(out_widths, out_dtypes)]
    return pl.pallas_call(
        kernel_fn,
        grid=(s // tm,),
        in_specs=in_specs,
        out_specs=out_specs,
        out_shape=out_shape,
        compiler_params=pltpu.CompilerParams(dimension_semantics=("arbitrary",),
                                             vmem_limit_bytes=vmem_mib * 1024 * 1024),
        name=name,
    )(*[a for a, _, _ in row_inputs], *[c[0] if isinstance(c, tuple) else c for c in consts])


def _a_prologue(x, pos_col, consts):
    wide = A_HEADS * A_HEAD_PAD
    return _rows_call(_a_prologue_kernel, [(x, D_MODEL, 0), (pos_col, 1, 0)], consts,
                      (wide, wide, A_HEADS * A_V_PAD, A_WIDTH), (BF16, BF16, BF16, BF16),
                      tm=256, vmem_mib=58, name="a_prologue")


def _b_qk_kernel(x_ref, pos_ref, ng_ref, w_ref, g_ref, extc_ref, ea_ref, eb_ref, out_ref):
    x = x_ref[...]
    xn = (x * _rms(x, D_MODEL) * ng_ref[...]).astype(BF16)
    z = jnp.dot(xn, w_ref[...], preferred_element_type=F32)
    pos = pos_ref[...]
    a = jnp.right_shift(pos, POS_SPLIT_SHIFT).astype(F32)
    b = jnp.bitwise_and(pos, (1 << POS_SPLIT_SHIFT) - 1).astype(F32)
    ext_ab = a * ea_ref[...] + b * eb_ref[...]
    g = g_ref[...]
    for h in range(B_HEADS):
        for c in range(2):
            t = z[:, (2 * h + c) * B_HD:(2 * h + c + 1) * B_HD]
            out_ref[:, h * B_HEAD_PAD + c * B_HD:h * B_HEAD_PAD + (c + 1) * B_HD] = (
                t * _rms(t, B_HD) * g).astype(BF16)
        out_ref[:, h * B_HEAD_PAD + 2 * B_HD:(h + 1) * B_HEAD_PAD] = (
            extc_ref[h:h + 1, :] + ext_ab).astype(BF16)


def _b_vg_kernel(x_ref, ng_ref, wv_ref, wg_ref, v_ref, sg_ref):
    x = x_ref[...]
    xn = (x * _rms(x, D_MODEL) * ng_ref[...]).astype(BF16)
    v_ref[...] = jnp.dot(xn, wv_ref[...], preferred_element_type=F32).astype(BF16)
    gate = jnp.dot(xn, wg_ref[...], preferred_element_type=F32)
    sg_ref[...] = (gate * jax.nn.sigmoid(gate)).astype(BF16)


def _out_proj_kernel(x_ref, o_ref, sg_ref, w_ref, y_ref):
    a = o_ref[...] * sg_ref[...]
    y_ref[...] = x_ref[...] + jnp.dot(a, w_ref[...], preferred_element_type=F32)


def _out_proj(x, o, sg, w, name):
    (y,) = _rows_call(_out_proj_kernel, [(x, D_MODEL, 0), (o, D_MODEL, 0), (sg, D_MODEL, 0)], (w,),
                      (D_MODEL,), (F32,), tm=512, vmem_mib=48, name=name)
    return y


def _tile_tables(pos, tq, tk):
    s = pos.shape[0]
    nq, nk = s // tq, s // tk
    ch = jnp.right_shift(pos, CHUNK_SHIFT)
    qmax_ch = ch.reshape(nq, tq).max(axis=1)
    kmin_ch = ch.reshape(nk, tk).min(axis=1)
    qmin = pos.reshape(nq, tq).min(axis=1)
    kmax = pos.reshape(nk, tk).max(axis=1)
    in_range = (pos.min() >= 0) & (pos.max() < (1 << (POS_SPLIT_SHIFT + 8)))
    visible = kmin_ch[None, :] <= qmax_ch[:, None]
    linear = visible & (kmax[None, :] <= qmin[:, None]) & in_range
    general = visible & jnp.logical_not(linear)
    rank = jnp.where(general, 0, jnp.where(linear, 1, 2))
    order = jnp.argsort(rank, axis=1, stable=True)
    n_general = general.sum(axis=1)
    n_visible = visible.sum(axis=1)
    qmax_top = ch.reshape(nq, 2, tq // 2)[:, 0].max(axis=1)
    kmin_right = ch.reshape(nk, 2, tk // 2)[:, 1].min(axis=1)
    quarter = kmin_right[None, :] > qmax_top[:, None]
    last_general = jnp.take_along_axis(order, jnp.maximum(n_general - 1, 0)[:, None], axis=1)
    quarter_last_general = jnp.take_along_axis(quarter, last_general, axis=1)[:, 0] & (n_general > 0)
    i32 = lambda a: a.astype(jnp.int32)
    return dict(visible=visible, n_general=i32(n_general), n_visible=i32(n_visible),
                order=i32(order).reshape(-1),
                quarter_last_general=i32(quarter_last_general))


def _pipelined_tiles(n, tile_at, produce, consume, first_produced=False):
    if not first_produced:
        produce(tile_at(0), 0)

    def two_steps(t):
        produce(tile_at(t + 1), 1)
        consume(tile_at(t), 0)
        produce(tile_at(t + 2), 0)
        consume(tile_at(t + 1), 1)

    def quad(u, c):
        two_steps(4 * u)
        two_steps(4 * u + 2)
        return c

    n_quads = (n - 1) // 4
    lax.fori_loop(0, n_quads, quad, 0)
    done = 4 * n_quads

    def pair(u, c):
        two_steps(done + 2 * u)
        return c

    lax.fori_loop(0, (n - 1 - done) // 2, pair, 0)
    last = n - 1

    @pl.when(last % 2 == 1)
    def _():
        produce(tile_at(last), 1)
        consume(tile_at(last - 1), 0)
        consume(tile_at(last), 1)

    @pl.when(last % 2 == 0)
    def _():
        consume(tile_at(last), 0)


def _lane_group_sum(p):
    out = p[:, :LANES]
    for j in range(1, p.shape[1] // LANES):
        out = out + p[:, j * LANES:(j + 1) * LANES]
    return out


def _a_attn_kernel(cnt_ref, list_ref, q_ref, k_ref, v_ref, pq_ref, pk_ref, o_ref,
                   p_sc, acc_sc, m_sc, alpha_sc, *, tq, tk, nk, group, online):
    step = pl.program_id(1)
    acc_sc[...] = jnp.zeros_like(acc_sc)
    if online:
        m_sc[...] = jnp.full_like(m_sc, NEG_BIG)

    def item_at(t):
        code = list_ref[step * (group * nk) + t]
        return code // nk, code % nk

    def produce(item, slot):
        sub, ki = item
        rows = pl.ds(pl.multiple_of(sub * tq, tq), tq)
        k = k_ref[pl.ds(pl.multiple_of(ki * tk, tk), tk), :]
        s = lax.dot_general(q_ref[rows, :], k, _NT, preferred_element_type=F32)
        qchunk = jnp.right_shift(pq_ref[rows, :], CHUNK_SHIFT)
        kchunk = jnp.right_shift(pk_ref[ki], CHUNK_SHIFT)
        s = jnp.where(kchunk <= qchunk, s, NEG_BIG)
        if online:
            m_prev = m_sc[sub]
            m_new = jnp.maximum(m_prev, jnp.max(s, axis=1, keepdims=True))
            alpha_sc[slot] = jnp.exp2(m_prev - m_new)
            m_sc[sub] = m_new
            s = s - m_new
        p_sc[slot] = jnp.exp2(s).astype(BF16)

    def consume(item, slot):
        sub, ki = item
        v = v_ref[pl.ds(pl.multiple_of(ki * tk, tk), tk), :]
        pv = jnp.dot(p_sc[slot], v, preferred_element_type=F32)
        if online:
            acc_sc[sub] = alpha_sc[slot] * acc_sc[sub] + pv
        else:
            acc_sc[sub] += pv

    _pipelined_tiles(cnt_ref[step], item_at, produce, consume)
    for sub in range(group):
        acc = acc_sc[sub]
        o_ref[sub * tq:(sub + 1) * tq, :] = (
            acc[:, :A_V] / acc[:, A_ONE_COL:A_ONE_COL + 1]).astype(BF16)


def _a_attn(q, k, v, pos, pos_col, shift_tables, online):
    s = q.shape[0]
    tq, tk = ATTN_TILES_ONLINE if online else ATTN_TILES
    nq, nk = s // tq, s // tk
    group = math.gcd(A_QUERY_GROUP, nq)
    t = _tile_tables(pos, tq, tk) if online else shift_tables
    flat = t["visible"].reshape(nq // group, group * nk)
    items = jnp.argsort(jnp.logical_not(flat), axis=1, stable=True).astype(jnp.int32).reshape(-1)
    counts = flat.sum(axis=1).astype(jnp.int32)
    pos_k3 = pos.reshape(nk, 1, tk)
    tg = group * tq
    grid_spec = pltpu.PrefetchScalarGridSpec(
        num_scalar_prefetch=2,
        grid=(A_HEADS, nq // group),
        in_specs=[pl.BlockSpec((tg, A_HEAD_PAD), lambda h, i, *_: (i, h)),
                  pl.BlockSpec((s, A_HEAD_PAD), lambda h, i, *_: (0, h)),
                  pl.BlockSpec((s, A_V_PAD), lambda h, i, *_: (0, h)),
                  pl.BlockSpec((tg, 1), lambda h, i, *_: (i, 0)),
                  pl.BlockSpec((nk, 1, tk), lambda h, i, *_: (0, 0, 0))],
        out_specs=pl.BlockSpec((tg, A_V), lambda h, i, *_: (i, h)),
        scratch_shapes=[pltpu.VMEM((2, tq, tk), BF16), pltpu.VMEM((group, tq, A_V_PAD), F32),
                        pltpu.VMEM((group, tq, 1), F32), pltpu.VMEM((2, tq, 1), F32)],
    )
    return pl.pallas_call(
        functools.partial(_a_attn_kernel, tq=tq, tk=tk, nk=nk, group=group, online=online),
        grid_spec=grid_spec,
        out_shape=jax.ShapeDtypeStruct((s, A_WIDTH), BF16),
        compiler_params=pltpu.CompilerParams(dimension_semantics=("arbitrary", "arbitrary"),
                                             vmem_limit_bytes=56 * 1024 * 1024),
        name="a_attention_online" if online else "a_attention",
    )(counts, items, q, k, v, pos_col, pos_k3)


def _b_attn_kernel(gcnt_ref, cnt_ref, list_ref, quarter_ref, q_ref, k_ref, v_ref, pq_ref, pk_ref,
                   slope_ref, shift_ref, lq1_ref, lk1_ref, lq2_ref, lk2_ref, sub_ref, o_ref,
                   p_sc, acc_sc, l_sc, m_sc, alpha_sc, *, tk, nk, lam_init, online):
    qi = pl.program_id(1)
    q = q_ref[...]
    tq = q.shape[0]
    hq, hk = tq // 2, tk // 2
    q_ext = q[:, 2 * B_HD:]
    q_full = [jnp.concatenate([q[:, c * B_HD:(c + 1) * B_HD], q_ext], axis=1) for c in range(2)]
    pq = pq_ref[...]
    qchunk = jnp.right_shift(pq, CHUNK_SHIFT)
    neg_slope = slope_ref[0, :, 0:1]
    shift = shift_ref[0:1, 0:1]
    l_sc[...] = jnp.zeros_like(l_sc)
    acc_sc[...] = jnp.zeros_like(acc_sc)
    if online:
        m_sc[...] = jnp.full_like(m_sc, NEG_BIG)

    def finish(s, c, slot):
        if online:
            m_prev = m_sc[c]
            m_new = jnp.maximum(m_prev, jnp.max(s, axis=1, keepdims=True))
            alpha = jnp.exp2(m_prev - m_new)
            alpha_sc[slot, c] = alpha
            m_sc[c] = m_new
            p = jnp.exp2(s - m_new)
            l_sc[c] = alpha * l_sc[c] + _lane_group_sum(p)
        else:
            p = jnp.exp2(s)
            l_sc[c] += _lane_group_sum(p)
        p_sc[slot, c] = p.astype(BF16)

    def produce_linear(ki, slot):
        k = k_ref[pl.ds(pl.multiple_of(ki * tk, tk), tk), :]
        k_ext = k[:, 2 * B_HD:]
        for c in range(2):
            k_full = jnp.concatenate([k[:, c * B_HD:(c + 1) * B_HD], k_ext], axis=1)
            finish(lax.dot_general(q_full[c], k_full, _NT, preferred_element_type=F32), c, slot)

    def produce_general(ki, slot):
        k = k_ref[pl.ds(pl.multiple_of(ki * tk, tk), tk), :]
        pk = pk_ref[ki]
        bias = jnp.abs(pq - pk).astype(F32) * neg_slope - shift
        bias = jnp.where(jnp.right_shift(pk, CHUNK_SHIFT) <= qchunk, bias, NEG_BIG)
        for c in range(2):
            s = lax.dot_general(q[:, c * B_HD:(c + 1) * B_HD], k[:, c * B_HD:(c + 1) * B_HD],
                                _NT, preferred_element_type=F32) + bias
            finish(s, c, slot)

    def produce_general_skip_quarter(ki, slot):
        start = pl.multiple_of(ki * tk, tk)
        pk = pk_ref[ki]
        regions = ((slice(0, tq), slice(0, hk)), (slice(hq, tq), slice(hk, tk)))
        for rows, cols in regions:
            k = k_ref[pl.ds(start + cols.start, hk), :]
            pkc = pk[:, cols]
            bias = jnp.abs(pq[rows] - pkc).astype(F32) * neg_slope - shift
            bias = jnp.where(jnp.right_shift(pkc, CHUNK_SHIFT) <= qchunk[rows], bias, NEG_BIG)
            for c in range(2):
                s = lax.dot_general(q[rows, c * B_HD:(c + 1) * B_HD], k[:, c * B_HD:(c + 1) * B_HD],
                                    _NT, preferred_element_type=F32) + bias
                p = jnp.exp2(s)
                l_sc[c, rows, :] += _lane_group_sum(p)
                p_sc[slot, c, rows, cols] = p.astype(BF16)
        for c in range(2):
            p_sc[slot, c, :hq, hk:] = jnp.zeros((hq, hk), BF16)

    def consume(ki, slot):
        v = v_ref[pl.ds(pl.multiple_of(ki * tk, tk), tk), :]
        for c in range(2):
            pv = jnp.dot(p_sc[slot, c], v, preferred_element_type=F32)
            if online:
                acc_sc[c] = alpha_sc[slot, c] * acc_sc[c] + pv
            else:
                acc_sc[c] += pv

    n_general = gcnt_ref[qi]
    tile_at = lambda t: list_ref[qi * nk + t]

    def general_body(t, c):
        produce_general(tile_at(t), 0)
        consume(tile_at(t), 0)
        return c

    first = jnp.maximum(n_general - 1, 0)
    lax.fori_loop(0, first, general_body, 0)

    skip_quarter = False if online else quarter_ref[qi] != 0

    @pl.when((n_general > 0) & jnp.logical_not(skip_quarter))
    def _():
        produce_general(tile_at(first), 0)

    if not online:
        @pl.when((n_general > 0) & skip_quarter)
        def _():
            produce_general_skip_quarter(tile_at(first), 0)

    @pl.when(n_general == 0)
    def _():
        produce_linear(tile_at(first), 0)

    _pipelined_tiles(cnt_ref[qi] - first, lambda t: tile_at(first + t), produce_linear, consume,
                     first_produced=True)

    lam = (jnp.exp(jnp.sum(lq1_ref[...] * lk1_ref[...], axis=-1, keepdims=True))
           - jnp.exp(jnp.sum(lq2_ref[...] * lk2_ref[...], axis=-1, keepdims=True)) + lam_init)
    l0 = jnp.sum(l_sc[0], axis=-1, keepdims=True)
    l1 = jnp.sum(l_sc[1], axis=-1, keepdims=True)
    od = acc_sc[0] / l0 - lam * (acc_sc[1] / l1)
    o_ref[...] = (od * _rms(od, B_V) * (sub_ref[...] * (1.0 - lam_init))).astype(BF16)


def _b_attn(q, k, v, pos, pos_col, shift_tables, consts, lam_init, online):
    s = q.shape[0]
    tq, tk = ATTN_TILES_ONLINE if online else ATTN_TILES
    nq, nk = s // tq, s // tk
    t = _tile_tables(pos, tq, tk) if online else shift_tables
    tables = (t["n_general"], t["n_visible"], t["order"], t["quarter_last_general"])
    pos_k3 = pos.reshape(nk, 1, tk)
    slopes = consts[0]
    grid_spec = pltpu.PrefetchScalarGridSpec(
        num_scalar_prefetch=4,
        grid=(B_HEADS, nq),
        in_specs=[pl.BlockSpec((tq, B_HEAD_PAD), lambda h, i, *_: (i, h)),
                  pl.BlockSpec((s, B_HEAD_PAD), lambda h, i, *_: (0, h),
                               pipeline_mode=pl.Buffered(1)),
                  pl.BlockSpec((s, B_V), lambda h, i, *_: (0, h)),
                  pl.BlockSpec((tq, 1), lambda h, i, *_: (i, 0)),
                  pl.BlockSpec((nk, 1, tk), lambda h, i, *_: (0, 0, 0)),
                  pl.BlockSpec((1,) + slopes.shape[1:], lambda h, i, *_: (h, 0, 0))]
                 + [pl.BlockSpec(c.shape, lambda h, i, *_: (0, 0)) for c in consts[1:]],
        out_specs=pl.BlockSpec((tq, B_V), lambda h, i, *_: (i, h)),
        scratch_shapes=[pltpu.VMEM((2, 2, tq, tk), BF16), pltpu.VMEM((2, tq, B_V), F32),
                        pltpu.VMEM((2, tq, LANES), F32), pltpu.VMEM((2, tq, 1), F32),
                        pltpu.VMEM((2, 2, tq, 1), F32)],
    )
    return pl.pallas_call(
        functools.partial(_b_attn_kernel, tk=tk, nk=nk, lam_init=lam_init, online=online),
        grid_spec=grid_spec,
        out_shape=jax.ShapeDtypeStruct((s, B_WIDTH), BF16),
        compiler_params=pltpu.CompilerParams(dimension_semantics=("arbitrary", "arbitrary"),
                                             vmem_limit_bytes=56 * 1024 * 1024),
        name="b_attention_online" if online else "b_attention",
    )(*tables, q, k, v, pos_col, pos_k3, *consts)


def _attend(fn, shift_bound, *args, **kwargs):
    return lax.cond(shift_bound <= MAX_CONST_SHIFT,
                    lambda: fn(*args, online=False, **kwargs),
                    lambda: fn(*args, online=True, **kwargs))


def _lane_onehot(value, lane):
    return jnp.zeros((1, LANES), F32).at[0, lane].set(value)


def _a_consts(norm_g, w_in, q_norm_g, w_q_up, kv_norm_g, w_kv_up, q_gain, k_gain):
    n_lat = A_Q_LORA + A_KV_LORA + A_ROPE
    wa = jnp.pad(w_in[:, :n_lat], ((0, 0), (0, LANES - A_ROPE))).astype(BF16)
    wg = w_in[:, n_lat:].astype(BF16)
    wq = w_q_up.reshape(A_Q_LORA, A_HEADS, A_QK)
    wq = jnp.pad(wq, ((0, 0), (0, 0), (0, A_HEAD_PAD - A_QK)))
    wq = wq.reshape(A_Q_LORA, A_HEADS * A_HEAD_PAD).astype(BF16)
    wkv = w_kv_up.reshape(A_KV_LORA, A_HEADS, A_NOPE + A_V)
    wkv = jnp.concatenate([wkv[:, :, :A_NOPE].reshape(A_KV_LORA, -1),
                           wkv[:, :, A_NOPE:].reshape(A_KV_LORA, -1)], axis=1).astype(BF16)
    q_scale = LOG2E / math.sqrt(A_QK)
    gq = jnp.pad(q_gain * q_scale, (0, A_HEAD_PAD - A_QK)).reshape(1, A_HEAD_PAD)
    gkn = k_gain[:A_NOPE].reshape(1, LANES)
    gkp = jnp.pad(k_gain[A_NOPE:], (0, LANES - A_ROPE)).reshape(1, LANES)
    half = A_ROPE // 2
    inv_freq = ROPE_THETA ** (-jnp.arange(0, A_ROPE, 2, dtype=F32) / A_ROPE)
    zeros = jnp.zeros((LANES - A_ROPE,), F32)
    invf = jnp.concatenate([inv_freq, inv_freq, zeros]).reshape(1, LANES)
    sgn = jnp.concatenate([-jnp.ones((half,), F32), jnp.ones((half,), F32), zeros]).reshape(1, LANES)
    bound = (BOUND_MARGIN * A_QK * q_scale
             * jnp.max(jnp.abs(q_gain)) * jnp.max(jnp.abs(k_gain)))
    bound = bound.astype(BF16).astype(F32)
    qshift = _lane_onehot(-bound, A_ROPE)
    one = _lane_onehot(1.0, A_ROPE)
    consts = (norm_g.reshape(1, -1), wa, wg, q_norm_g.reshape(1, -1), wq, kv_norm_g.reshape(1, -1),
              wkv, gq, gkn, gkp, invf, sgn, qshift, one)
    return consts, bound


def _b_consts(q_gain, k_gain):
    b_scale = LOG2E / math.sqrt(B_HD)
    bound = (BOUND_MARGIN * B_HD * b_scale * jnp.max(jnp.abs(q_gain)) * jnp.max(jnp.abs(k_gain)))
    bound = bound.astype(BF16).astype(F32)
    slopes = 2.0 ** (-8.0 * jnp.arange(1, B_HEADS + 1, dtype=F32) / B_HEADS)
    sigma = LOG2E * slopes
    sig_hi = sigma.astype(BF16).astype(F32)
    sig_lo = (sigma - sig_hi).astype(BF16).astype(F32)
    big = float(1 << POS_SPLIT_SHIFT)
    sig_cols = jnp.stack([sig_hi * big, sig_hi, sig_lo * big, sig_lo], axis=1)
    zeros = lambda n: jnp.zeros((B_HEADS, n), F32)
    extc_q = jnp.concatenate([jnp.broadcast_to(-bound, (B_HEADS, 1)), sig_cols, zeros(LANES - 5)],
                             axis=1)
    extc_k = jnp.concatenate([jnp.ones((B_HEADS, 1), F32), zeros(4), sig_cols, zeros(LANES - 9)],
                             axis=1)
    lanes = jnp.arange(LANES)
    pick = lambda ids, val: jnp.where(jnp.isin(lanes, jnp.array(ids)), val, 0.0).reshape(1, LANES)
    ea_q, eb_q = pick([5, 7], -1.0), pick([6, 8], -1.0)
    ea_k, eb_k = pick([1, 3], 1.0), pick([2, 4], 1.0)
    neg_slopes = jnp.broadcast_to(-(sig_hi + sig_lo)[:, None, None], (B_HEADS, 1, LANES))
    return bound, b_scale, neg_slopes, (extc_q, ea_q, eb_q), (extc_k, ea_k, eb_k)


def kernel(x, positions, a_norm, a_w_in, a_q_norm, a_w_q_up, a_kv_norm, a_w_kv_up, a_q_gain, a_k_gain, a_w_out, b_norm, b_w_in, b_q_gain, b_k_gain, b_lambda_q1, b_lambda_k1, b_lambda_q2, b_lambda_k2, b_subln, b_w_out):
    batch, seq, _ = x.shape
    assert batch == 1
    xs = x[0]
    pos = positions[0]
    pos_col = pos.reshape(seq, 1)
    row = lambda a: a.reshape(1, -1)

    consts, a_bound = _a_consts(a_norm[0], a_w_in[0], a_q_norm[0], a_w_q_up[0], a_kv_norm[0],
                                a_w_kv_up[0], a_q_gain[0], a_k_gain[0])
    q, k, v, sg = _a_prologue(xs, pos_col, consts)
    shift_tables = _tile_tables(pos, *ATTN_TILES)
    o = _attend(_a_attn, a_bound, q, k, v, pos, pos_col, shift_tables)
    xs = _out_proj(xs, o, sg, a_w_out[0].astype(BF16), name="a_out_proj")

    layer_idx = 1
    lam_init = 0.8 - 0.6 * math.exp(-0.3 * layer_idx)
    b_bound, b_scale, neg_slopes, q_ext, k_ext = _b_consts(b_q_gain[0], b_k_gain[0])
    w_in = b_w_in[0].astype(BF16)
    ng = row(b_norm[0])
    section = lambda j: (w_in, (D_MODEL, B_WIDTH), (0, j))
    rows = [(xs, D_MODEL, 0), (pos_col, 1, 0)]
    (q,) = _rows_call(_b_qk_kernel, rows, (ng, section(0), row(b_q_gain[0] * b_scale), *q_ext),
                      (B_HEADS * B_HEAD_PAD,), (BF16,), tm=1024, vmem_mib=56, name="b_q_proj")
    (k,) = _rows_call(_b_qk_kernel, rows, (ng, section(1), row(b_k_gain[0]), *k_ext),
                      (B_HEADS * B_HEAD_PAD,), (BF16,), tm=1024, vmem_mib=56, name="b_k_proj")
    v, sg = _rows_call(_b_vg_kernel, rows[:1], (ng, section(2), section(3)),
                       (B_WIDTH, B_WIDTH), (BF16, BF16), tm=512, vmem_mib=56, name="b_vg_proj")
    b_consts = (neg_slopes, jnp.broadcast_to(b_bound.reshape(1, 1), (1, LANES)),
                row(b_lambda_q1[0]), row(b_lambda_k1[0]), row(b_lambda_q2[0]), row(b_lambda_k2[0]),
                row(b_subln[0]))
    o = _attend(_b_attn, b_bound, q, k, v, pos, pos_col, shift_tables, b_consts, lam_init)
    xs = _out_proj(xs, o, sg, b_w_out[0].astype(BF16), name="b_out_proj")
    return xs[None]
```

```python
import functools
import math

import jax
import jax.numpy as jnp
from jax import lax
from jax.experimental import pallas as pl
from jax.experimental.pallas import tpu as pltpu

D_MODEL = 2048
CHUNK_SHIFT = 6
EPS = 1e-6
LOG2E = 1.4426950408889634

A_HEADS = 16
A_NOPE = 128
A_ROPE = 64
A_QK = A_NOPE + A_ROPE
A_V = 128
A_Q_LORA = 512
A_KV_LORA = 512
A_WIDTH = A_HEADS * A_V
A_HEAD_PAD = 256
A_V_PAD = 256
A_ONE_COL = A_V + A_ROPE
ROPE_THETA = 10000.0

B_HEADS = 8
B_HD = 128
B_V = 2 * B_HD
B_WIDTH = B_HEADS * B_V
B_QK = B_HEADS * 2 * B_HD
B_HEAD_PAD = 3 * B_HD
POS_SPLIT_SHIFT = 7

LANES = 128
NEG_BIG = -1e30
MAX_CONST_SHIFT = 50.0
BOUND_MARGIN = 1.01
ATTN_TILES = (1024, 1024)
ATTN_TILES_ONLINE = (512, 1024)
A_QUERY_GROUP = 4
V7X_VMEM_BYTES = 64 * 1024 * 1024
VMEM_LIMIT_BYTES = V7X_VMEM_BYTES - 8 * 1024 * 1024

F32 = jnp.float32
BF16 = jnp.bfloat16
_NT = (((1,), (1,)), ((), ()))


def _const_spec(a, block_shape=None, block_index=None):
    shape = a.shape if block_shape is None else block_shape
    index = (0,) * len(shape) if block_index is None else block_index
    return pl.BlockSpec(shape, lambda *_: index, pipeline_mode=pl.Buffered(1))


def _rms(x, denom):
    return lax.rsqrt(jnp.sum(x * x, axis=-1, keepdims=True) * (1.0 / denom) + EPS)


def _a_prologue_kernel(x_ref, pos_ref, ng_ref, wa_ref, wg_ref, qng_ref, wq_ref, kvng_ref, wkv_ref,
                       gq_ref, gkn_ref, gkp_ref, invf_ref, sgn_ref, qshift_ref, one_ref,
                       q_ref, k_ref, v_ref, sg_ref):
    x = x_ref[...]
    xn = (x * _rms(x, D_MODEL) * ng_ref[...]).astype(BF16)
    za = jnp.dot(xn, wa_ref[...], preferred_element_type=F32)
    gate = jnp.dot(xn, wg_ref[...], preferred_element_type=F32)
    sg_ref[...] = (gate * jax.nn.sigmoid(gate)).astype(BF16)
    rope = _rope_fn(pos_ref, invf_ref, sgn_ref)
    _a_q_rows(za[:, :A_Q_LORA], rope, qng_ref, wq_ref, gq_ref, qshift_ref, q_ref)
    _a_kv_rows(za[:, A_Q_LORA:A_Q_LORA + A_KV_LORA], za[:, A_Q_LORA + A_KV_LORA:], rope,
               kvng_ref, wkv_ref, gkn_ref, gkp_ref, one_ref, k_ref, v_ref)


def _rope_fn(pos_ref, invf_ref, sgn_ref):
    ang = pos_ref[...].astype(F32) * invf_ref[...]
    cos = jnp.cos(ang)
    sin_signed = jnp.sin(ang) * sgn_ref[...]
    lane = lax.broadcasted_iota(jnp.int32, cos.shape, 1)
    first_half = lane < (A_ROPE // 2)

    def rope(t):
        swapped = jnp.where(first_half, pltpu.roll(t, LANES - A_ROPE // 2, 1),
                            pltpu.roll(t, A_ROPE // 2, 1))
        return t * cos + swapped * sin_signed

    return rope


def _a_q_rows(cq, rope, qng_ref, wq_ref, gq_ref, qshift_ref, q_ref):
    cqn = (cq * _rms(cq, A_Q_LORA) * qng_ref[...]).astype(BF16)
    q = jnp.dot(cqn, wq_ref[...], preferred_element_type=F32)
    gq = gq_ref[...]
    qshift = qshift_ref[...]
    for h in range(A_HEADS):
        qh = q[:, h * A_HEAD_PAD:(h + 1) * A_HEAD_PAD]
        qn = qh * _rms(qh, A_QK) * gq
        q_ref[:, h * A_HEAD_PAD:h * A_HEAD_PAD + LANES] = qn[:, :LANES].astype(BF16)
        q_ref[:, h * A_HEAD_PAD + LANES:(h + 1) * A_HEAD_PAD] = (
            rope(qn[:, LANES:]) + qshift).astype(BF16)


def _a_kv_rows(ckv, kpe, rope, kvng_ref, wkv_ref, gkn_ref, gkp_ref, one_ref, k_ref, v_ref):
    ckvn = (ckv * _rms(ckv, A_KV_LORA) * kvng_ref[...]).astype(BF16)
    kv = jnp.dot(ckvn, wkv_ref[...], preferred_element_type=F32)
    gkn = gkn_ref[...]
    one = one_ref[...]
    v_one = jnp.broadcast_to(one, kpe.shape).astype(BF16)
    kpe_ss = jnp.sum(kpe * kpe, axis=-1, keepdims=True)
    kpe_roped = rope(kpe * gkp_ref[...])
    for h in range(A_HEADS):
        kn = kv[:, h * A_NOPE:(h + 1) * A_NOPE]
        r = lax.rsqrt((jnp.sum(kn * kn, axis=-1, keepdims=True) + kpe_ss) * (1.0 / A_QK) + EPS)
        k_ref[:, h * A_HEAD_PAD:h * A_HEAD_PAD + LANES] = (kn * r * gkn).astype(BF16)
        k_ref[:, h * A_HEAD_PAD + LANES:(h + 1) * A_HEAD_PAD] = (kpe_roped * r + one).astype(BF16)

        v_ref[:, h * A_V_PAD:h * A_V_PAD + A_V] = (
            kv[:, A_HEADS * A_NOPE + h * A_V:A_HEADS * A_NOPE + (h + 1) * A_V].astype(BF16))
        v_ref[:, h * A_V_PAD + A_V:(h + 1) * A_V_PAD] = v_one


def _rows_call(kernel_fn, row_inputs, consts, out_widths, out_dtypes, tm, name):
    s = row_inputs[0][0].shape[0]
    in_specs = [pl.BlockSpec((tm, w), lambda i, cb=cb: (i, cb)) for _, w, cb in row_inputs]
    in_specs += [_const_spec(*c) if isinstance(c, tuple) else _const_spec(c) for c in consts]
    out_specs = [pl.BlockSpec((tm, w), lambda i: (i, 0)) for w in out_widths]
    out_shape = [jax.ShapeDtypeStruct((s, w), dt) for w, dt in zip(out_widths, out_dtypes)]
    return pl.pallas_call(
        kernel_fn,
        grid=(s // tm,),
        in_specs=in_specs,
        out_specs=out_specs,
        out_shape=out_shape,
        compiler_params=pltpu.CompilerParams(dimension_semantics=("arbitrary",),
                                             vmem_limit_bytes=VMEM_LIMIT_BYTES),
        name=name,
    )(*[a for a, _, _ in row_inputs], *[c[0] if isinstance(c, tuple) else c for c in consts])


def _a_prologue(x, pos_col, consts):
    wide = A_HEADS * A_HEAD_PAD
    return _rows_call(_a_prologue_kernel, [(x, D_MODEL, 0), (pos_col, 1, 0)], consts,
                      (wide, wide, A_HEADS * A_V_PAD, A_WIDTH), (BF16, BF16, BF16, BF16),
                      tm=256, name="a_prologue")


def _b_qk_kernel(x_ref, pos_ref, ng_ref, w_ref, g_ref, extc_ref, ea_ref, eb_ref, out_ref):
    x = x_ref[...]
    xn = (x * _rms(x, D_MODEL) * ng_ref[...]).astype(BF16)
    z = jnp.dot(xn, w_ref[...], preferred_element_type=F32)
    pos = pos_ref[...]
    a = jnp.right_shift(pos, POS_SPLIT_SHIFT).astype(F32)
    b = jnp.bitwise_and(pos, (1 << POS_SPLIT_SHIFT) - 1).astype(F32)
    ext_ab = a * ea_ref[...] + b * eb_ref[...]
    g = g_ref[...]
    for h in range(B_HEADS):
        for c in range(2):
            t = z[:, (2 * h + c) * B_HD:(2 * h + c + 1) * B_HD]
            out_ref[:, h * B_HEAD_PAD + c * B_HD:h * B_HEAD_PAD + (c + 1) * B_HD] = (
                t * _rms(t, B_HD) * g).astype(BF16)
        out_ref[:, h * B_HEAD_PAD + 2 * B_HD:(h + 1) * B_HEAD_PAD] = (
            extc_ref[h:h + 1, :] + ext_ab).astype(BF16)


def _b_vg_kernel(x_ref, ng_ref, wv_ref, wg_ref, v_ref, sg_ref):
    x = x_ref[...]
    xn = (x * _rms(x, D_MODEL) * ng_ref[...]).astype(BF16)
    v_ref[...] = jnp.dot(xn, wv_ref[...], preferred_element_type=F32).astype(BF16)
    gate = jnp.dot(xn, wg_ref[...], preferred_element_type=F32)
    sg_ref[...] = (gate * jax.nn.sigmoid(gate)).astype(BF16)


def _out_proj_kernel(x_ref, o_ref, sg_ref, w_ref, y_ref):
    a = o_ref[...] * sg_ref[...]
    y_ref[...] = x_ref[...] + jnp.dot(a, w_ref[...], preferred_element_type=F32)


def _out_proj(x, o, sg, w, name):
    (y,) = _rows_call(_out_proj_kernel, [(x, D_MODEL, 0), (o, D_MODEL, 0), (sg, D_MODEL, 0)], (w,),
                      (D_MODEL,), (F32,), tm=512, name=name)
    return y


def _tile_tables(pos, tq, tk):
    s = pos.shape[0]
    nq, nk = s // tq, s // tk
    ch = jnp.right_shift(pos, CHUNK_SHIFT)
    qmax_ch = ch.reshape(nq, tq).max(axis=1)
    kmin_ch = ch.reshape(nk, tk).min(axis=1)
    qmin = pos.reshape(nq, tq).min(axis=1)
    kmax = pos.reshape(nk, tk).max(axis=1)
    in_range = (pos.min() >= 0) & (pos.max() < (1 << (POS_SPLIT_SHIFT + 8)))
    visible = kmin_ch[None, :] <= qmax_ch[:, None]
    linear = visible & (kmax[None, :] <= qmin[:, None]) & in_range
    general = visible & jnp.logical_not(linear)
    rank = jnp.where(general, 0, jnp.where(linear, 1, 2))
    order = jnp.argsort(rank, axis=1, stable=True)
    n_general = general.sum(axis=1)
    n_visible = visible.sum(axis=1)
    qmax_top = ch.reshape(nq, 2, tq // 2)[:, 0].max(axis=1)
    kmin_right = ch.reshape(nk, 2, tk // 2)[:, 1].min(axis=1)
    quarter = kmin_right[None, :] > qmax_top[:, None]
    last_general = jnp.take_along_axis(order, jnp.maximum(n_general - 1, 0)[:, None], axis=1)
    quarter_last_general = jnp.take_along_axis(quarter, last_general, axis=1)[:, 0] & (n_general > 0)
    i32 = lambda a: a.astype(jnp.int32)
    return dict(visible=visible, n_general=i32(n_general), n_visible=i32(n_visible),
                order=i32(order).reshape(-1),
                quarter_last_general=i32(quarter_last_general))


def _pipelined_tiles(n, tile_at, produce, consume, first_produced=False):
    if not first_produced:
        produce(tile_at(0), 0)

    def two_steps(t):
        produce(tile_at(t + 1), 1)
        consume(tile_at(t), 0)
        produce(tile_at(t + 2), 0)
        consume(tile_at(t + 1), 1)

    def quad(u, c):
        two_steps(4 * u)
        two_steps(4 * u + 2)
        return c

    n_quads = (n - 1) // 4
    lax.fori_loop(0, n_quads, quad, 0)
    done = 4 * n_quads

    def pair(u, c):
        two_steps(done + 2 * u)
        return c

    lax.fori_loop(0, (n - 1 - done) // 2, pair, 0)
    last = n - 1

    @pl.when(last % 2 == 1)
    def _():
        produce(tile_at(last), 1)
        consume(tile_at(last - 1), 0)
        consume(tile_at(last), 1)

    @pl.when(last % 2 == 0)
    def _():
        consume(tile_at(last), 0)


def _lane_group_sum(p):
    out = p[:, :LANES]
    for j in range(1, p.shape[1] // LANES):
        out = out + p[:, j * LANES:(j + 1) * LANES]
    return out


def _a_attn_kernel(cnt_ref, list_ref, q_ref, k_ref, v_ref, pq_ref, pk_ref, o_ref,
                   p_sc, acc_sc, m_sc, alpha_sc, *, tq, tk, nk, group, online):
    step = pl.program_id(1)
    acc_sc[...] = jnp.zeros_like(acc_sc)
    if online:
        m_sc[...] = jnp.full_like(m_sc, NEG_BIG)

    def item_at(t):
        code = list_ref[step * (group * nk) + t]
        return code // nk, code % nk

    def produce(item, slot):
        sub, ki = item
        rows = pl.ds(pl.multiple_of(sub * tq, tq), tq)
        k = k_ref[pl.ds(pl.multiple_of(ki * tk, tk), tk), :]
        s = lax.dot_general(q_ref[rows, :], k, _NT, preferred_element_type=F32)
        qchunk = jnp.right_shift(pq_ref[rows, :], CHUNK_SHIFT)
        kchunk = jnp.right_shift(pk_ref[ki], CHUNK_SHIFT)
        s = jnp.where(kchunk <= qchunk, s, NEG_BIG)
        if online:
            m_prev = m_sc[sub]
            m_new = jnp.maximum(m_prev, jnp.max(s, axis=1, keepdims=True))
            alpha_sc[slot] = jnp.exp2(m_prev - m_new)
            m_sc[sub] = m_new
            s = s - m_new
        p_sc[slot] = jnp.exp2(s).astype(BF16)

    def consume(item, slot):
        sub, ki = item
        v = v_ref[pl.ds(pl.multiple_of(ki * tk, tk), tk), :]
        pv = jnp.dot(p_sc[slot], v, preferred_element_type=F32)
        if online:
            acc_sc[sub] = alpha_sc[slot] * acc_sc[sub] + pv
        else:
            acc_sc[sub] += pv

    _pipelined_tiles(cnt_ref[step], item_at, produce, consume)
    for sub in range(group):
        acc = acc_sc[sub]
        o_ref[sub * tq:(sub + 1) * tq, :] = (
            acc[:, :A_V] / acc[:, A_ONE_COL:A_ONE_COL + 1]).astype(BF16)


def _a_attn(q, k, v, pos, pos_col, shift_tables, online):
    s = q.shape[0]
    tq, tk = ATTN_TILES_ONLINE if online else ATTN_TILES
    nq, nk = s // tq, s // tk
    group = math.gcd(A_QUERY_GROUP, nq)
    t = _tile_tables(pos, tq, tk) if online else shift_tables
    flat = t["visible"].reshape(nq // group, group * nk)
    items = jnp.argsort(jnp.logical_not(flat), axis=1, stable=True).astype(jnp.int32).reshape(-1)
    counts = flat.sum(axis=1).astype(jnp.int32)
    pos_k3 = pos.reshape(nk, 1, tk)
    tg = group * tq
    grid_spec = pltpu.PrefetchScalarGridSpec(
        num_scalar_prefetch=2,
        grid=(A_HEADS, nq // group),
        in_specs=[pl.BlockSpec((tg, A_HEAD_PAD), lambda h, i, *_: (i, h)),
                  pl.BlockSpec((s, A_HEAD_PAD), lambda h, i, *_: (0, h)),
                  pl.BlockSpec((s, A_V_PAD), lambda h, i, *_: (0, h)),
                  pl.BlockSpec((tg, 1), lambda h, i, *_: (i, 0)),
                  pl.BlockSpec((nk, 1, tk), lambda h, i, *_: (0, 0, 0))],
        out_specs=pl.BlockSpec((tg, A_V), lambda h, i, *_: (i, h)),
        scratch_shapes=[pltpu.VMEM((2, tq, tk), BF16), pltpu.VMEM((group, tq, A_V_PAD), F32),
                        pltpu.VMEM((group, tq, 1), F32), pltpu.VMEM((2, tq, 1), F32)],
    )
    return pl.pallas_call(
        functools.partial(_a_attn_kernel, tq=tq, tk=tk, nk=nk, group=group, online=online),
        grid_spec=grid_spec,
        out_shape=jax.ShapeDtypeStruct((s, A_WIDTH), BF16),
        compiler_params=pltpu.CompilerParams(dimension_semantics=("arbitrary", "arbitrary"),
                                             vmem_limit_bytes=VMEM_LIMIT_BYTES),
        name="a_attention_online" if online else "a_attention",
    )(counts, items, q, k, v, pos_col, pos_k3)


def _b_attn_kernel(gcnt_ref, cnt_ref, list_ref, quarter_ref, q_ref, k_ref, v_ref, pq_ref, pk_ref,
                   slope_ref, shift_ref, lq1_ref, lk1_ref, lq2_ref, lk2_ref, sub_ref, o_ref,
                   p_sc, acc_sc, l_sc, m_sc, alpha_sc, *, tk, nk, lam_init, online):
    qi = pl.program_id(1)
    q = q_ref[...]
    tq = q.shape[0]
    hq, hk = tq // 2, tk // 2
    q_ext = q[:, 2 * B_HD:]
    q_full = [jnp.concatenate([q[:, c * B_HD:(c + 1) * B_HD], q_ext], axis=1) for c in range(2)]
    pq = pq_ref[...]
    qchunk = jnp.right_shift(pq, CHUNK_SHIFT)
    neg_slope = slope_ref[0, :, 0:1]
    shift = shift_ref[0:1, 0:1]
    l_sc[...] = jnp.zeros_like(l_sc)
    acc_sc[...] = jnp.zeros_like(acc_sc)
    if online:
        m_sc[...] = jnp.full_like(m_sc, NEG_BIG)

    def finish(s, c, slot):
        if online:
            m_prev = m_sc[c]
            m_new = jnp.maximum(m_prev, jnp.max(s, axis=1, keepdims=True))
            alpha = jnp.exp2(m_prev - m_new)
            alpha_sc[slot, c] = alpha
            m_sc[c] = m_new
            p = jnp.exp2(s - m_new)
            l_sc[c] = alpha * l_sc[c] + _lane_group_sum(p)
        else:
            p = jnp.exp2(s)
            l_sc[c] += _lane_group_sum(p)
        p_sc[slot, c] = p.astype(BF16)

    def produce_linear(ki, slot):
        k = k_ref[pl.ds(pl.multiple_of(ki * tk, tk), tk), :]
        k_ext = k[:, 2 * B_HD:]
        for c in range(2):
            k_full = jnp.concatenate([k[:, c * B_HD:(c + 1) * B_HD], k_ext], axis=1)
            finish(lax.dot_general(q_full[c], k_full, _NT, preferred_element_type=F32), c, slot)

    def produce_general(ki, slot):
        k = k_ref[pl.ds(pl.multiple_of(ki * tk, tk), tk), :]
        pk = pk_ref[ki]
        bias = jnp.abs(pq - pk).astype(F32) * neg_slope - shift
        bias = jnp.where(jnp.right_shift(pk, CHUNK_SHIFT) <= qchunk, bias, NEG_BIG)
        for c in range(2):
            s = lax.dot_general(q[:, c * B_HD:(c + 1) * B_HD], k[:, c * B_HD:(c + 1) * B_HD],
                                _NT, preferred_element_type=F32) + bias
            finish(s, c, slot)

    def produce_general_skip_quarter(ki, slot):
        start = pl.multiple_of(ki * tk, tk)
        pk = pk_ref[ki]
        regions = ((slice(0, tq), slice(0, hk)), (slice(hq, tq), slice(hk, tk)))
        for rows, cols in regions:
            k = k_ref[pl.ds(start + cols.start, hk), :]
            pkc = pk[:, cols]
            bias = jnp.abs(pq[rows] - pkc).astype(F32) * neg_slope - shift
            bias = jnp.where(jnp.right_shift(pkc, CHUNK_SHIFT) <= qchunk[rows], bias, NEG_BIG)
            for c in range(2):
                s = lax.dot_general(q[rows, c * B_HD:(c + 1) * B_HD], k[:, c * B_HD:(c + 1) * B_HD],
                                    _NT, preferred_element_type=F32) + bias
                p = jnp.exp2(s)
                l_sc[c, rows, :] += _lane_group_sum(p)
                p_sc[slot, c, rows, cols] = p.astype(BF16)
        for c in range(2):
            p_sc[slot, c, :hq, hk:] = jnp.zeros((hq, hk), BF16)

    def consume(ki, slot):
        v = v_ref[pl.ds(pl.multiple_of(ki * tk, tk), tk), :]
        for c in range(2):
            pv = jnp.dot(p_sc[slot, c], v, preferred_element_type=F32)
            if online:
                acc_sc[c] = alpha_sc[slot, c] * acc_sc[c] + pv
            else:
                acc_sc[c] += pv

    n_general = gcnt_ref[qi]
    tile_at = lambda t: list_ref[qi * nk + t]

    def general_body(t, c):
        produce_general(tile_at(t), 0)
        consume(tile_at(t), 0)
        return c

    first = jnp.maximum(n_general - 1, 0)
    lax.fori_loop(0, first, general_body, 0)

    skip_quarter = False if online else quarter_ref[qi] != 0

    @pl.when((n_general > 0) & jnp.logical_not(skip_quarter))
    def _():
        produce_general(tile_at(first), 0)

    if not online:
        @pl.when((n_general > 0) & skip_quarter)
        def _():
            produce_general_skip_quarter(tile_at(first), 0)

    @pl.when(n_general == 0)
    def _():
        produce_linear(tile_at(first), 0)

    _pipelined_tiles(cnt_ref[qi] - first, lambda t: tile_at(first + t), produce_linear, consume,
                     first_produced=True)

    lam = (jnp.exp(jnp.sum(lq1_ref[...] * lk1_ref[...], axis=-1, keepdims=True))
           - jnp.exp(jnp.sum(lq2_ref[...] * lk2_ref[...], axis=-1, keepdims=True)) + lam_init)
    l0 = jnp.sum(l_sc[0], axis=-1, keepdims=True)
    l1 = jnp.sum(l_sc[1], axis=-1, keepdims=True)
    od = acc_sc[0] / l0 - lam * (acc_sc[1] / l1)
    o_ref[...] = (od * _rms(od, B_V) * (sub_ref[...] * (1.0 - lam_init))).astype(BF16)


def _b_attn(q, k, v, pos, pos_col, shift_tables, consts, lam_init, online):
    s = q.shape[0]
    tq, tk = ATTN_TILES_ONLINE if online else ATTN_TILES
    nq, nk = s // tq, s // tk
    t = _tile_tables(pos, tq, tk) if online else shift_tables
    tables = (t["n_general"], t["n_visible"], t["order"], t["quarter_last_general"])
    pos_k3 = pos.reshape(nk, 1, tk)
    slopes = consts[0]
    grid_spec = pltpu.PrefetchScalarGridSpec(
        num_scalar_prefetch=4,
        grid=(B_HEADS, nq),
        in_specs=[pl.BlockSpec((tq, B_HEAD_PAD), lambda h, i, *_: (i, h)),
                  pl.BlockSpec((s, B_HEAD_PAD), lambda h, i, *_: (0, h),
                               pipeline_mode=pl.Buffered(1)),
                  pl.BlockSpec((s, B_V), lambda h, i, *_: (0, h)),
                  pl.BlockSpec((tq, 1), lambda h, i, *_: (i, 0)),
                  pl.BlockSpec((nk, 1, tk), lambda h, i, *_: (0, 0, 0)),
                  pl.BlockSpec((1,) + slopes.shape[1:], lambda h, i, *_: (h, 0, 0))]
                 + [pl.BlockSpec(c.shape, lambda h, i, *_: (0, 0)) for c in consts[1:]],
        out_specs=pl.BlockSpec((tq, B_V), lambda h, i, *_: (i, h)),
        scratch_shapes=[pltpu.VMEM((2, 2, tq, tk), BF16), pltpu.VMEM((2, tq, B_V), F32),
                        pltpu.VMEM((2, tq, LANES), F32), pltpu.VMEM((2, tq, 1), F32),
                        pltpu.VMEM((2, 2, tq, 1), F32)],
    )
    return pl.pallas_call(
        functools.partial(_b_attn_kernel, tk=tk, nk=nk, lam_init=lam_init, online=online),
        grid_spec=grid_spec,
        out_shape=jax.ShapeDtypeStruct((s, B_WIDTH), BF16),
        compiler_params=pltpu.CompilerParams(dimension_semantics=("arbitrary", "arbitrary"),
                                             vmem_limit_bytes=VMEM_LIMIT_BYTES),
        name="b_attention_online" if online else "b_attention",
    )(*tables, q, k, v, pos_col, pos_k3, *consts)


def _attend(fn, shift_bound, *args, **kwargs):
    return lax.cond(shift_bound <= MAX_CONST_SHIFT,
                    lambda: fn(*args, online=False, **kwargs),
                    lambda: fn(*args, online=True, **kwargs))


def _lane_onehot(value, lane):
    return jnp.zeros((1, LANES), F32).at[0, lane].set(value)


def _a_consts(norm_g, w_in, q_norm_g, w_q_up, kv_norm_g, w_kv_up, q_gain, k_gain):
    n_lat = A_Q_LORA + A_KV_LORA + A_ROPE
    wa = jnp.pad(w_in[:, :n_lat], ((0, 0), (0, LANES - A_ROPE))).astype(BF16)
    wg = w_in[:, n_lat:].astype(BF16)
    wq = w_q_up.reshape(A_Q_LORA, A_HEADS, A_QK)
    wq = jnp.pad(wq, ((0, 0), (0, 0), (0, A_HEAD_PAD - A_QK)))
    wq = wq.reshape(A_Q_LORA, A_HEADS * A_HEAD_PAD).astype(BF16)
    wkv = w_kv_up.reshape(A_KV_LORA, A_HEADS, A_NOPE + A_V)
    wkv = jnp.concatenate([wkv[:, :, :A_NOPE].reshape(A_KV_LORA, -1),
                           wkv[:, :, A_NOPE:].reshape(A_KV_LORA, -1)], axis=1).astype(BF16)
    q_scale = LOG2E / math.sqrt(A_QK)
    gq = jnp.pad(q_gain * q_scale, (0, A_HEAD_PAD - A_QK)).reshape(1, A_HEAD_PAD)
    gkn = k_gain[:A_NOPE].reshape(1, LANES)
    gkp = jnp.pad(k_gain[A_NOPE:], (0, LANES - A_ROPE)).reshape(1, LANES)
    half = A_ROPE // 2
    inv_freq = ROPE_THETA ** (-jnp.arange(0, A_ROPE, 2, dtype=F32) / A_ROPE)
    zeros = jnp.zeros((LANES - A_ROPE,), F32)
    invf = jnp.concatenate([inv_freq, inv_freq, zeros]).reshape(1, LANES)
    sgn = jnp.concatenate([-jnp.ones((half,), F32), jnp.ones((half,), F32), zeros]).reshape(1, LANES)
    bound = (BOUND_MARGIN * A_QK * q_scale
             * jnp.max(jnp.abs(q_gain)) * jnp.max(jnp.abs(k_gain)))
    bound = bound.astype(BF16).astype(F32)
    qshift = _lane_onehot(-bound, A_ROPE)
    one = _lane_onehot(1.0, A_ROPE)
    consts = (norm_g.reshape(1, -1), wa, wg, q_norm_g.reshape(1, -1), wq, kv_norm_g.reshape(1, -1),
              wkv, gq, gkn, gkp, invf, sgn, qshift, one)
    return consts, bound


def _b_consts(q_gain, k_gain):
    b_scale = LOG2E / math.sqrt(B_HD)
    bound = (BOUND_MARGIN * B_HD * b_scale * jnp.max(jnp.abs(q_gain)) * jnp.max(jnp.abs(k_gain)))
    bound = bound.astype(BF16).astype(F32)
    slopes = 2.0 ** (-8.0 * jnp.arange(1, B_HEADS + 1, dtype=F32) / B_HEADS)
    sigma = LOG2E * slopes
    sig_hi = sigma.astype(BF16).astype(F32)
    sig_lo = (sigma - sig_hi).astype(BF16).astype(F32)
    big = float(1 << POS_SPLIT_SHIFT)
    sig_cols = jnp.stack([sig_hi * big, sig_hi, sig_lo * big, sig_lo], axis=1)
    zeros = lambda n: jnp.zeros((B_HEADS, n), F32)
    extc_q = jnp.concatenate([jnp.broadcast_to(-bound, (B_HEADS, 1)), sig_cols, zeros(LANES - 5)],
                             axis=1)
    extc_k = jnp.concatenate([jnp.ones((B_HEADS, 1), F32), zeros(4), sig_cols, zeros(LANES - 9)],
                             axis=1)
    lanes = jnp.arange(LANES)
    pick = lambda ids, val: jnp.where(jnp.isin(lanes, jnp.array(ids)), val, 0.0).reshape(1, LANES)
    ea_q, eb_q = pick([5, 7], -1.0), pick([6, 8], -1.0)
    ea_k, eb_k = pick([1, 3], 1.0), pick([2, 4], 1.0)
    neg_slopes = jnp.broadcast_to(-(sig_hi + sig_lo)[:, None, None], (B_HEADS, 1, LANES))
    return bound, b_scale, neg_slopes, (extc_q, ea_q, eb_q), (extc_k, ea_k, eb_k)


def kernel(x, positions, a_norm, a_w_in, a_q_norm, a_w_q_up, a_kv_norm, a_w_kv_up, a_q_gain, a_k_gain, a_w_out, b_norm, b_w_in, b_q_gain, b_k_gain, b_lambda_q1, b_lambda_k1, b_lambda_q2, b_lambda_k2, b_subln, b_w_out):
    batch, seq, _ = x.shape
    assert batch == 1
    xs = x[0]
    pos = positions[0]
    pos_col = pos.reshape(seq, 1)
    row = lambda a: a.reshape(1, -1)

    consts, a_bound = _a_consts(a_norm[0], a_w_in[0], a_q_norm[0], a_w_q_up[0], a_kv_norm[0],
                                a_w_kv_up[0], a_q_gain[0], a_k_gain[0])
    q, k, v, sg = _a_prologue(xs, pos_col, consts)
    shift_tables = _tile_tables(pos, *ATTN_TILES)
    o = _attend(_a_attn, a_bound, q, k, v, pos, pos_col, shift_tables)
    xs = _out_proj(xs, o, sg, a_w_out[0].astype(BF16), name="a_out_proj")

    layer_idx = 1
    lam_init = 0.8 - 0.6 * math.exp(-0.3 * layer_idx)
    b_bound, b_scale, neg_slopes, q_ext, k_ext = _b_consts(b_q_gain[0], b_k_gain[0])
    w_in = b_w_in[0].astype(BF16)
    ng = row(b_norm[0])
    section = lambda j: (w_in, (D_MODEL, B_WIDTH), (0, j))
    rows = [(xs, D_MODEL, 0), (pos_col, 1, 0)]
    (q,) = _rows_call(_b_qk_kernel, rows, (ng, section(0), row(b_q_gain[0] * b_scale), *q_ext),
                      (B_HEADS * B_HEAD_PAD,), (BF16,), tm=1024, name="b_q_proj")
    (k,) = _rows_call(_b_qk_kernel, rows, (ng, section(1), row(b_k_gain[0]), *k_ext),
                      (B_HEADS * B_HEAD_PAD,), (BF16,), tm=1024, name="b_k_proj")
    v, sg = _rows_call(_b_vg_kernel, rows[:1], (ng, section(2), section(3)),
                       (B_WIDTH, B_WIDTH), (BF16, BF16), tm=512, name="b_vg_proj")
    b_consts = (neg_slopes, jnp.broadcast_to(b_bound.reshape(1, 1), (1, LANES)),
                row(b_lambda_q1[0]), row(b_lambda_k1[0]), row(b_lambda_q2[0]), row(b_lambda_k2[0]),
                row(b_subln[0]))
    o = _attend(_b_attn, b_bound, q, k, v, pos, pos_col, shift_tables, b_consts, lam_init)
    xs = _out_proj(xs, o, sg, b_w_out[0].astype(BF16), name="b_out_proj")
    return xs[None]
```

```python
import functools
import math

import jax
import jax.numpy as jnp
from jax import lax
from jax.experimental import pallas as pl
from jax.experimental.pallas import tpu as pltpu

D_MODEL = 2048
CHUNK_SHIFT = 6
EPS = 1e-6
LOG2E = 1.4426950408889634

A_HEADS = 16
A_NOPE = 128
A_ROPE = 64
A_QK = A_NOPE + A_ROPE
A_V = 128
A_Q_LORA = 512
A_KV_LORA = 512
A_WIDTH = A_HEADS * A_V
A_HEAD_PAD = 256
A_V_PAD = 256
A_ONE_COL = A_V + A_ROPE
ROPE_THETA = 10000.0

B_HEADS = 8
B_HD = 128
B_V = 2 * B_HD
B_WIDTH = B_HEADS * B_V
B_QK = B_HEADS * 2 * B_HD
B_HEAD_PAD = 3 * B_HD
POS_SPLIT_SHIFT = 7

LANES = 128
NEG_BIG = -1e30
MAX_CONST_SHIFT = 50.0
BOUND_MARGIN = 1.01
ATTN_TILES = (1024, 1024)
ATTN_TILES_ONLINE = (512, 1024)
A_QUERY_GROUP = 4
V7X_VMEM_BYTES = 64 * 1024 * 1024
VMEM_LIMIT_BYTES = V7X_VMEM_BYTES - 8 * 1024 * 1024

F32 = jnp.float32
BF16 = jnp.bfloat16
_NT = (((1,), (1,)), ((), ()))


def _const_spec(a, block_shape=None, block_index=None):
    shape = a.shape if block_shape is None else block_shape
    index = (0,) * len(shape) if block_index is None else block_index
    return pl.BlockSpec(shape, lambda *_: index, pipeline_mode=pl.Buffered(1))


def _rms(x, denom):
    return lax.rsqrt(jnp.sum(x * x, axis=-1, keepdims=True) * (1.0 / denom) + EPS)


def _a_prologue_kernel(x_ref, pos_ref, ng_ref, wa_ref, wg_ref, qng_ref, wq_ref, kvng_ref, wkv_ref,
                       gq_ref, gkn_ref, gkp_ref, invf_ref, sgn_ref, qshift_ref, one_ref,
                       q_ref, k_ref, v_ref, sg_ref):
    x = x_ref[...]
    xn = (x * _rms(x, D_MODEL) * ng_ref[...]).astype(BF16)
    za = jnp.dot(xn, wa_ref[...], preferred_element_type=F32)
    gate = jnp.dot(xn, wg_ref[...], preferred_element_type=F32)
    sg_ref[...] = (gate * jax.nn.sigmoid(gate)).astype(BF16)
    rope = _rope_fn(pos_ref, invf_ref, sgn_ref)
    _a_q_rows(za[:, :A_Q_LORA], rope, qng_ref, wq_ref, gq_ref, qshift_ref, q_ref)
    _a_kv_rows(za[:, A_Q_LORA:A_Q_LORA + A_KV_LORA], za[:, A_Q_LORA + A_KV_LORA:], rope,
               kvng_ref, wkv_ref, gkn_ref, gkp_ref, one_ref, k_ref, v_ref)


def _rope_fn(pos_ref, invf_ref, sgn_ref):
    ang = pos_ref[...].astype(F32) * invf_ref[...]
    cos = jnp.cos(ang)
    sin_signed = jnp.sin(ang) * sgn_ref[...]
    lane = lax.broadcasted_iota(jnp.int32, cos.shape, 1)
    first_half = lane < (A_ROPE // 2)

    def rope(t):
        swapped = jnp.where(first_half, pltpu.roll(t, LANES - A_ROPE // 2, 1),
                            pltpu.roll(t, A_ROPE // 2, 1))
        return t * cos + swapped * sin_signed

    return rope


def _a_q_rows(cq, rope, qng_ref, wq_ref, gq_ref, qshift_ref, q_ref):
    cqn = (cq * _rms(cq, A_Q_LORA) * qng_ref[...]).astype(BF16)
    q = jnp.dot(cqn, wq_ref[...], preferred_element_type=F32)
    gq = gq_ref[...]
    qshift = qshift_ref[...]
    for h in range(A_HEADS):
        qh = q[:, h * A_HEAD_PAD:(h + 1) * A_HEAD_PAD]
        qn = qh * _rms(qh, A_QK) * gq
        q_ref[:, h * A_HEAD_PAD:h * A_HEAD_PAD + LANES] = qn[:, :LANES].astype(BF16)
        q_ref[:, h * A_HEAD_PAD + LANES:(h + 1) * A_HEAD_PAD] = (
            rope(qn[:, LANES:]) + qshift).astype(BF16)


def _a_kv_rows(ckv, kpe, rope, kvng_ref, wkv_ref, gkn_ref, gkp_ref, one_ref, k_ref, v_ref):
    ckvn = (ckv * _rms(ckv, A_KV_LORA) * kvng_ref[...]).astype(BF16)
    kv = jnp.dot(ckvn, wkv_ref[...], preferred_element_type=F32)
    gkn = gkn_ref[...]
    one = one_ref[...]
    v_one = jnp.broadcast_to(one, kpe.shape).astype(BF16)
    kpe_ss = jnp.sum(kpe * kpe, axis=-1, keepdims=True)
    kpe_roped = rope(kpe * gkp_ref[...])
    for h in range(A_HEADS):
        kn = kv[:, h * A_NOPE:(h + 1) * A_NOPE]
        r = lax.rsqrt((jnp.sum(kn * kn, axis=-1, keepdims=True) + kpe_ss) * (1.0 / A_QK) + EPS)
        k_ref[:, h * A_HEAD_PAD:h * A_HEAD_PAD + LANES] = (kn * r * gkn).astype(BF16)
        k_ref[:, h * A_HEAD_PAD + LANES:(h + 1) * A_HEAD_PAD] = (kpe_roped * r + one).astype(BF16)

        v_ref[:, h * A_V_PAD:h * A_V_PAD + A_V] = (
            kv[:, A_HEADS * A_NOPE + h * A_V:A_HEADS * A_NOPE + (h + 1) * A_V].astype(BF16))
        v_ref[:, h * A_V_PAD + A_V:(h + 1) * A_V_PAD] = v_one


def _rows_call(kernel_fn, row_inputs, consts, out_widths, out_dtypes, tm, name):
    s = row_inputs[0][0].shape[0]
    in_specs = [pl.BlockSpec((tm, w), lambda i, cb=cb: (i, cb)) for _, w, cb in row_inputs]
    in_specs += [_const_spec(*c) if isinstance(c, tuple) else _const_spec(c) for c in consts]
    out_specs = [pl.BlockSpec((tm, w), lambda i: (i, 0)) for w in out_widths]
    out_shape = [jax.ShapeDtypeStruct((s, w), dt) for w, dt in zip(out_widths, out_dtypes)]
    return pl.pallas_call(
        kernel_fn,
        grid=(s // tm,),
        in_specs=in_specs,
        out_specs=out_specs,
        out_shape=out_shape,
        compiler_params=pltpu.CompilerParams(dimension_semantics=("arbitrary",),
                                             vmem_limit_bytes=VMEM_LIMIT_BYTES),
        name=name,
    )(*[a for a, _, _ in row_inputs], *[c[0] if isinstance(c, tuple) else c for c in consts])


def _a_prologue(x, pos_col, consts):
    wide = A_HEADS * A_HEAD_PAD
    return _rows_call(_a_prologue_kernel, [(x, D_MODEL, 0), (pos_col, 1, 0)], consts,
                      (wide, wide, A_HEADS * A_V_PAD, A_WIDTH), (BF16, BF16, BF16, BF16),
                      tm=256, name="a_prologue")


def _b_qk_kernel(x_ref, pos_ref, ng_ref, w_ref, g_ref, extc_ref, ea_ref, eb_ref, out_ref):
    x = x_ref[...]
    xn = (x * _rms(x, D_MODEL) * ng_ref[...]).astype(BF16)
    z = jnp.dot(xn, w_ref[...], preferred_element_type=F32)
    pos = pos_ref[...]
    a = jnp.right_shift(pos, POS_SPLIT_SHIFT).astype(F32)
    b = jnp.bitwise_and(pos, (1 << POS_SPLIT_SHIFT) - 1).astype(F32)
    ext_ab = a * ea_ref[...] + b * eb_ref[...]
    g = g_ref[...]
    for h in range(B_HEADS):
        for c in range(2):
            t = z[:, (2 * h + c) * B_HD:(2 * h + c + 1) * B_HD]
            out_ref[:, h * B_HEAD_PAD + c * B_HD:h * B_HEAD_PAD + (c + 1) * B_HD] = (
                t * _rms(t, B_HD) * g).astype(BF16)
        out_ref[:, h * B_HEAD_PAD + 2 * B_HD:(h + 1) * B_HEAD_PAD] = (
            extc_ref[h:h + 1, :] + ext_ab).astype(BF16)


def _b_vg_kernel(x_ref, ng_ref, wv_ref, wg_ref, v_ref, sg_ref):
    x = x_ref[...]
    xn = (x * _rms(x, D_MODEL) * ng_ref[...]).astype(BF16)
    v_ref[...] = jnp.dot(xn, wv_ref[...], preferred_element_type=F32).astype(BF16)
    gate = jnp.dot(xn, wg_ref[...], preferred_element_type=F32)
    sg_ref[...] = (gate * jax.nn.sigmoid(gate)).astype(BF16)


def _out_proj_kernel(x_ref, o_ref, sg_ref, w_ref, y_ref):
    a = o_ref[...] * sg_ref[...]
    y_ref[...] = x_ref[...] + jnp.dot(a, w_ref[...], preferred_element_type=F32)


def _out_proj(x, o, sg, w, name):
    (y,) = _rows_call(_out_proj_kernel, [(x, D_MODEL, 0), (o, D_MODEL, 0), (sg, D_MODEL, 0)], (w,),
                      (D_MODEL,), (F32,), tm=512, name=name)
    return y


def _tile_tables(pos, tq, tk):
    s = pos.shape[0]
    nq, nk = s // tq, s // tk
    ch = jnp.right_shift(pos, CHUNK_SHIFT)
    qmax_ch = ch.reshape(nq, tq).max(axis=1)
    kmin_ch = ch.reshape(nk, tk).min(axis=1)
    qmin = pos.reshape(nq, tq).min(axis=1)
    kmax = pos.reshape(nk, tk).max(axis=1)
    in_range = (pos.min() >= 0) & (pos.max() < (1 << (POS_SPLIT_SHIFT + 8)))
    visible = kmin_ch[None, :] <= qmax_ch[:, None]
    linear = visible & (kmax[None, :] <= qmin[:, None]) & in_range
    general = visible & jnp.logical_not(linear)
    rank = jnp.where(general, 0, jnp.where(linear, 1, 2))
    order = jnp.argsort(rank, axis=1, stable=True)
    n_general = general.sum(axis=1)
    n_visible = visible.sum(axis=1)
    qmax_top = ch.reshape(nq, 2, tq // 2)[:, 0].max(axis=1)
    kmin_right = ch.reshape(nk, 2, tk // 2)[:, 1].min(axis=1)
    quarter = kmin_right[None, :] > qmax_top[:, None]
    last_general = jnp.take_along_axis(order, jnp.maximum(n_general - 1, 0)[:, None], axis=1)
    quarter_last_general = jnp.take_along_axis(quarter, last_general, axis=1)[:, 0] & (n_general > 0)
    i32 = lambda a: a.astype(jnp.int32)
    return dict(visible=visible, n_general=i32(n_general), n_visible=i32(n_visible),
                order=i32(order).reshape(-1),
                quarter_last_general=i32(quarter_last_general))


def _pipelined_tiles(n, tile_at, produce, consume, first_produced=False, body_steps=(4, 2)):
    if not first_produced:
        produce(tile_at(0), 0)

    def two_steps(t):
        produce(tile_at(t + 1), 1)
        consume(tile_at(t), 0)
        produce(tile_at(t + 2), 0)
        consume(tile_at(t + 1), 1)

    done = 0
    for steps in body_steps:
        def body(u, c, steps=steps, done=done):
            for j in range(0, steps, 2):
                two_steps(done + steps * u + j)
            return c

        trips = (n - 1 - done) // steps
        lax.fori_loop(0, trips, body, 0)
        done = done + steps * trips
    last = n - 1

    @pl.when(last % 2 == 1)
    def _():
        produce(tile_at(last), 1)
        consume(tile_at(last - 1), 0)
        consume(tile_at(last), 1)

    @pl.when(last % 2 == 0)
    def _():
        consume(tile_at(last), 0)


def _lane_group_sum(p):
    out = p[:, :LANES]
    for j in range(1, p.shape[1] // LANES):
        out = out + p[:, j * LANES:(j + 1) * LANES]
    return out


def _a_attn_kernel(cnt_ref, list_ref, q_ref, k_ref, v_ref, pq_ref, pk_ref, o_ref,
                   p_sc, acc_sc, m_sc, alpha_sc, *, tq, tk, nk, group, online):
    step = pl.program_id(1)
    acc_sc[...] = jnp.zeros_like(acc_sc)
    if online:
        m_sc[...] = jnp.full_like(m_sc, NEG_BIG)

    def item_at(t):
        code = list_ref[step * (group * nk) + t]
        return code // nk, code % nk

    def produce(item, slot):
        sub, ki = item
        rows = pl.ds(pl.multiple_of(sub * tq, tq), tq)
        k = k_ref[pl.ds(pl.multiple_of(ki * tk, tk), tk), :]
        s = lax.dot_general(q_ref[rows, :], k, _NT, preferred_element_type=F32)
        qchunk = jnp.right_shift(pq_ref[rows, :], CHUNK_SHIFT)
        kchunk = jnp.right_shift(pk_ref[ki], CHUNK_SHIFT)
        s = jnp.where(kchunk <= qchunk, s, NEG_BIG)
        if online:
            m_prev = m_sc[sub]
            m_new = jnp.maximum(m_prev, jnp.max(s, axis=1, keepdims=True))
            alpha_sc[slot] = jnp.exp2(m_prev - m_new)
            m_sc[sub] = m_new
            s = s - m_new
        p_sc[slot] = jnp.exp2(s).astype(BF16)

    def consume(item, slot):
        sub, ki = item
        v = v_ref[pl.ds(pl.multiple_of(ki * tk, tk), tk), :]
        pv = jnp.dot(p_sc[slot], v, preferred_element_type=F32)
        if online:
            acc_sc[sub] = alpha_sc[slot] * acc_sc[sub] + pv
        else:
            acc_sc[sub] += pv

    _pipelined_tiles(cnt_ref[step], item_at, produce, consume, body_steps=(8, 4, 2))
    for sub in range(group):
        acc = acc_sc[sub]
        o_ref[sub * tq:(sub + 1) * tq, :] = (
            acc[:, :A_V] / acc[:, A_ONE_COL:A_ONE_COL + 1]).astype(BF16)


def _a_attn(q, k, v, pos, pos_col, shift_tables, online):
    s = q.shape[0]
    tq, tk = ATTN_TILES_ONLINE if online else ATTN_TILES
    nq, nk = s // tq, s // tk
    group = math.gcd(A_QUERY_GROUP, nq)
    t = _tile_tables(pos, tq, tk) if online else shift_tables
    flat = t["visible"].reshape(nq // group, group * nk)
    items = jnp.argsort(jnp.logical_not(flat), axis=1, stable=True).astype(jnp.int32).reshape(-1)
    counts = flat.sum(axis=1).astype(jnp.int32)
    pos_k3 = pos.reshape(nk, 1, tk)
    tg = group * tq
    grid_spec = pltpu.PrefetchScalarGridSpec(
        num_scalar_prefetch=2,
        grid=(A_HEADS, nq // group),
        in_specs=[pl.BlockSpec((tg, A_HEAD_PAD), lambda h, i, *_: (i, h)),
                  pl.BlockSpec((s, A_HEAD_PAD), lambda h, i, *_: (0, h)),
                  pl.BlockSpec((s, A_V_PAD), lambda h, i, *_: (0, h)),
                  pl.BlockSpec((tg, 1), lambda h, i, *_: (i, 0)),
                  pl.BlockSpec((nk, 1, tk), lambda h, i, *_: (0, 0, 0))],
        out_specs=pl.BlockSpec((tg, A_V), lambda h, i, *_: (i, h)),
        scratch_shapes=[pltpu.VMEM((2, tq, tk), BF16), pltpu.VMEM((group, tq, A_V_PAD), F32),
                        pltpu.VMEM((group, tq, 1), F32), pltpu.VMEM((2, tq, 1), F32)],
    )
    return pl.pallas_call(
        functools.partial(_a_attn_kernel, tq=tq, tk=tk, nk=nk, group=group, online=online),
        grid_spec=grid_spec,
        out_shape=jax.ShapeDtypeStruct((s, A_WIDTH), BF16),
        compiler_params=pltpu.CompilerParams(dimension_semantics=("arbitrary", "arbitrary"),
                                             vmem_limit_bytes=VMEM_LIMIT_BYTES),
        name="a_attention_online" if online else "a_attention",
    )(counts, items, q, k, v, pos_col, pos_k3)


def _b_attn_kernel(gcnt_ref, cnt_ref, list_ref, quarter_ref, q_ref, k_ref, v_ref, pq_ref, pk_ref,
                   slope_ref, shift_ref, lq1_ref, lk1_ref, lq2_ref, lk2_ref, sub_ref, o_ref,
                   p_sc, acc_sc, l_sc, m_sc, alpha_sc, *, tk, nk, lam_init, online):
    qi = pl.program_id(1)
    q = q_ref[...]
    tq = q.shape[0]
    hq, hk = tq // 2, tk // 2
    q_ext = q[:, 2 * B_HD:]
    q_full = [jnp.concatenate([q[:, c * B_HD:(c + 1) * B_HD], q_ext], axis=1) for c in range(2)]
    pq = pq_ref[...]
    qchunk = jnp.right_shift(pq, CHUNK_SHIFT)
    neg_slope = slope_ref[0, :, 0:1]
    shift = shift_ref[0:1, 0:1]
    l_sc[...] = jnp.zeros_like(l_sc)
    acc_sc[...] = jnp.zeros_like(acc_sc)
    if online:
        m_sc[...] = jnp.full_like(m_sc, NEG_BIG)

    def finish(s, c, slot):
        if online:
            m_prev = m_sc[c]
            m_new = jnp.maximum(m_prev, jnp.max(s, axis=1, keepdims=True))
            alpha = jnp.exp2(m_prev - m_new)
            alpha_sc[slot, c] = alpha
            m_sc[c] = m_new
            p = jnp.exp2(s - m_new)
            l_sc[c] = alpha * l_sc[c] + _lane_group_sum(p)
        else:
            p = jnp.exp2(s)
            l_sc[c] += _lane_group_sum(p)
        p_sc[slot, c] = p.astype(BF16)

    def produce_linear(ki, slot):
        k = k_ref[pl.ds(pl.multiple_of(ki * tk, tk), tk), :]
        k_ext = k[:, 2 * B_HD:]
        for c in range(2):
            k_full = jnp.concatenate([k[:, c * B_HD:(c + 1) * B_HD], k_ext], axis=1)
            finish(lax.dot_general(q_full[c], k_full, _NT, preferred_element_type=F32), c, slot)

    def produce_general(ki, slot):
        k = k_ref[pl.ds(pl.multiple_of(ki * tk, tk), tk), :]
        pk = pk_ref[ki]
        bias = jnp.abs(pq - pk).astype(F32) * neg_slope - shift
        bias = jnp.where(jnp.right_shift(pk, CHUNK_SHIFT) <= qchunk, bias, NEG_BIG)
        for c in range(2):
            s = lax.dot_general(q[:, c * B_HD:(c + 1) * B_HD], k[:, c * B_HD:(c + 1) * B_HD],
                                _NT, preferred_element_type=F32) + bias
            finish(s, c, slot)

    def produce_general_skip_quarter(ki, slot):
        start = pl.multiple_of(ki * tk, tk)
        pk = pk_ref[ki]
        regions = ((slice(0, tq), slice(0, hk)), (slice(hq, tq), slice(hk, tk)))
        for rows, cols in regions:
            k = k_ref[pl.ds(start + cols.start, hk), :]
            pkc = pk[:, cols]
            bias = jnp.abs(pq[rows] - pkc).astype(F32) * neg_slope - shift
            bias = jnp.where(jnp.right_shift(pkc, CHUNK_SHIFT) <= qchunk[rows], bias, NEG_BIG)
            for c in range(2):
                s = lax.dot_general(q[rows, c * B_HD:(c + 1) * B_HD], k[:, c * B_HD:(c + 1) * B_HD],
                                    _NT, preferred_element_type=F32) + bias
                p = jnp.exp2(s)
                l_sc[c, rows, :] += _lane_group_sum(p)
                p_sc[slot, c, rows, cols] = p.astype(BF16)
        for c in range(2):
            p_sc[slot, c, :hq, hk:] = jnp.zeros((hq, hk), BF16)

    def consume(ki, slot):
        v = v_ref[pl.ds(pl.multiple_of(ki * tk, tk), tk), :]
        for c in range(2):
            pv = jnp.dot(p_sc[slot, c], v, preferred_element_type=F32)
            if online:
                acc_sc[c] = alpha_sc[slot, c] * acc_sc[c] + pv
            else:
                acc_sc[c] += pv

    n_general = gcnt_ref[qi]
    tile_at = lambda t: list_ref[qi * nk + t]

    def general_body(t, c):
        produce_general(tile_at(t), 0)
        consume(tile_at(t), 0)
        return c

    first = jnp.maximum(n_general - 1, 0)
    lax.fori_loop(0, first, general_body, 0)

    skip_quarter = False if online else quarter_ref[qi] != 0

    @pl.when((n_general > 0) & jnp.logical_not(skip_quarter))
    def _():
        produce_general(tile_at(first), 0)

    if not online:
        @pl.when((n_general > 0) & skip_quarter)
        def _():
            produce_general_skip_quarter(tile_at(first), 0)

    @pl.when(n_general == 0)
    def _():
        produce_linear(tile_at(first), 0)

    _pipelined_tiles(cnt_ref[qi] - first, lambda t: tile_at(first + t), produce_linear, consume,
                     first_produced=True)

    lam = (jnp.exp(jnp.sum(lq1_ref[...] * lk1_ref[...], axis=-1, keepdims=True))
           - jnp.exp(jnp.sum(lq2_ref[...] * lk2_ref[...], axis=-1, keepdims=True)) + lam_init)
    l0 = jnp.sum(l_sc[0], axis=-1, keepdims=True)
    l1 = jnp.sum(l_sc[1], axis=-1, keepdims=True)
    od = acc_sc[0] / l0 - lam * (acc_sc[1] / l1)
    o_ref[...] = (od * _rms(od, B_V) * (sub_ref[...] * (1.0 - lam_init))).astype(BF16)


def _b_attn(q, k, v, pos, pos_col, shift_tables, consts, lam_init, online):
    s = q.shape[0]
    tq, tk = ATTN_TILES_ONLINE if online else ATTN_TILES
    nq, nk = s // tq, s // tk
    t = _tile_tables(pos, tq, tk) if online else shift_tables
    tables = (t["n_general"], t["n_visible"], t["order"], t["quarter_last_general"])
    pos_k3 = pos.reshape(nk, 1, tk)
    slopes = consts[0]
    grid_spec = pltpu.PrefetchScalarGridSpec(
        num_scalar_prefetch=4,
        grid=(B_HEADS, nq),
        in_specs=[pl.BlockSpec((tq, B_HEAD_PAD), lambda h, i, *_: (i, h)),
                  pl.BlockSpec((s, B_HEAD_PAD), lambda h, i, *_: (0, h),
                               pipeline_mode=pl.Buffered(1)),
                  pl.BlockSpec((s, B_V), lambda h, i, *_: (0, h)),
                  pl.BlockSpec((tq, 1), lambda h, i, *_: (i, 0)),
                  pl.BlockSpec((nk, 1, tk), lambda h, i, *_: (0, 0, 0)),
                  pl.BlockSpec((1,) + slopes.shape[1:], lambda h, i, *_: (h, 0, 0))]
                 + [pl.BlockSpec(c.shape, lambda h, i, *_: (0, 0)) for c in consts[1:]],
        out_specs=pl.BlockSpec((tq, B_V), lambda h, i, *_: (i, h)),
        scratch_shapes=[pltpu.VMEM((2, 2, tq, tk), BF16), pltpu.VMEM((2, tq, B_V), F32),
                        pltpu.VMEM((2, tq, LANES), F32), pltpu.VMEM((2, tq, 1), F32),
                        pltpu.VMEM((2, 2, tq, 1), F32)],
    )
    return pl.pallas_call(
        functools.partial(_b_attn_kernel, tk=tk, nk=nk, lam_init=lam_init, online=online),
        grid_spec=grid_spec,
        out_shape=jax.ShapeDtypeStruct((s, B_WIDTH), BF16),
        compiler_params=pltpu.CompilerParams(dimension_semantics=("arbitrary", "arbitrary"),
                                             vmem_limit_bytes=VMEM_LIMIT_BYTES),
        name="b_attention_online" if online else "b_attention",
    )(*tables, q, k, v, pos_col, pos_k3, *consts)


def _attend(fn, shift_bound, *args, **kwargs):
    return lax.cond(shift_bound <= MAX_CONST_SHIFT,
                    lambda: fn(*args, online=False, **kwargs),
                    lambda: fn(*args, online=True, **kwargs))


def _lane_onehot(value, lane):
    return jnp.zeros((1, LANES), F32).at[0, lane].set(value)


def _a_consts(norm_g, w_in, q_norm_g, w_q_up, kv_norm_g, w_kv_up, q_gain, k_gain):
    n_lat = A_Q_LORA + A_KV_LORA + A_ROPE
    wa = jnp.pad(w_in[:, :n_lat], ((0, 0), (0, LANES - A_ROPE))).astype(BF16)
    wg = w_in[:, n_lat:].astype(BF16)
    wq = w_q_up.reshape(A_Q_LORA, A_HEADS, A_QK)
    wq = jnp.pad(wq, ((0, 0), (0, 0), (0, A_HEAD_PAD - A_QK)))
    wq = wq.reshape(A_Q_LORA, A_HEADS * A_HEAD_PAD).astype(BF16)
    wkv = w_kv_up.reshape(A_KV_LORA, A_HEADS, A_NOPE + A_V)
    wkv = jnp.concatenate([wkv[:, :, :A_NOPE].reshape(A_KV_LORA, -1),
                           wkv[:, :, A_NOPE:].reshape(A_KV_LORA, -1)], axis=1).astype(BF16)
    q_scale = LOG2E / math.sqrt(A_QK)
    gq = jnp.pad(q_gain * q_scale, (0, A_HEAD_PAD - A_QK)).reshape(1, A_HEAD_PAD)
    gkn = k_gain[:A_NOPE].reshape(1, LANES)
    gkp = jnp.pad(k_gain[A_NOPE:], (0, LANES - A_ROPE)).reshape(1, LANES)
    half = A_ROPE // 2
    inv_freq = ROPE_THETA ** (-jnp.arange(0, A_ROPE, 2, dtype=F32) / A_ROPE)
    zeros = jnp.zeros((LANES - A_ROPE,), F32)
    invf = jnp.concatenate([inv_freq, inv_freq, zeros]).reshape(1, LANES)
    sgn = jnp.concatenate([-jnp.ones((half,), F32), jnp.ones((half,), F32), zeros]).reshape(1, LANES)
    bound = (BOUND_MARGIN * A_QK * q_scale
             * jnp.max(jnp.abs(q_gain)) * jnp.max(jnp.abs(k_gain)))
    bound = bound.astype(BF16).astype(F32)
    qshift = _lane_onehot(-bound, A_ROPE)
    one = _lane_onehot(1.0, A_ROPE)
    consts = (norm_g.reshape(1, -1), wa, wg, q_norm_g.reshape(1, -1), wq, kv_norm_g.reshape(1, -1),
              wkv, gq, gkn, gkp, invf, sgn, qshift, one)
    return consts, bound


def _b_consts(q_gain, k_gain):
    b_scale = LOG2E / math.sqrt(B_HD)
    bound = (BOUND_MARGIN * B_HD * b_scale * jnp.max(jnp.abs(q_gain)) * jnp.max(jnp.abs(k_gain)))
    bound = bound.astype(BF16).astype(F32)
    slopes = 2.0 ** (-8.0 * jnp.arange(1, B_HEADS + 1, dtype=F32) / B_HEADS)
    sigma = LOG2E * slopes
    sig_hi = sigma.astype(BF16).astype(F32)
    sig_lo = (sigma - sig_hi).astype(BF16).astype(F32)
    big = float(1 << POS_SPLIT_SHIFT)
    sig_cols = jnp.stack([sig_hi * big, sig_hi, sig_lo * big, sig_lo], axis=1)
    zeros = lambda n: jnp.zeros((B_HEADS, n), F32)
    extc_q = jnp.concatenate([jnp.broadcast_to(-bound, (B_HEADS, 1)), sig_cols, zeros(LANES - 5)],
                             axis=1)
    extc_k = jnp.concatenate([jnp.ones((B_HEADS, 1), F32), zeros(4), sig_cols, zeros(LANES - 9)],
                             axis=1)
    lanes = jnp.arange(LANES)
    pick = lambda ids, val: jnp.where(jnp.isin(lanes, jnp.array(ids)), val, 0.0).reshape(1, LANES)
    ea_q, eb_q = pick([5, 7], -1.0), pick([6, 8], -1.0)
    ea_k, eb_k = pick([1, 3], 1.0), pick([2, 4], 1.0)
    neg_slopes = jnp.broadcast_to(-(sig_hi + sig_lo)[:, None, None], (B_HEADS, 1, LANES))
    return bound, b_scale, neg_slopes, (extc_q, ea_q, eb_q), (extc_k, ea_k, eb_k)


def kernel(x, positions, a_norm, a_w_in, a_q_norm, a_w_q_up, a_kv_norm, a_w_kv_up, a_q_gain, a_k_gain, a_w_out, b_norm, b_w_in, b_q_gain, b_k_gain, b_lambda_q1, b_lambda_k1, b_lambda_q2, b_lambda_k2, b_subln, b_w_out):
    batch, seq, _ = x.shape
    assert batch == 1
    xs = x[0]
    pos = positions[0]
    pos_col = pos.reshape(seq, 1)
    row = lambda a: a.reshape(1, -1)

    consts, a_bound = _a_consts(a_norm[0], a_w_in[0], a_q_norm[0], a_w_q_up[0], a_kv_norm[0],
                                a_w_kv_up[0], a_q_gain[0], a_k_gain[0])
    q, k, v, sg = _a_prologue(xs, pos_col, consts)
    shift_tables = _tile_tables(pos, *ATTN_TILES)
    o = _attend(_a_attn, a_bound, q, k, v, pos, pos_col, shift_tables)
    xs = _out_proj(xs, o, sg, a_w_out[0].astype(BF16), name="a_out_proj")

    layer_idx = 1
    lam_init = 0.8 - 0.6 * math.exp(-0.3 * layer_idx)
    b_bound, b_scale, neg_slopes, q_ext, k_ext = _b_consts(b_q_gain[0], b_k_gain[0])
    w_in = b_w_in[0].astype(BF16)
    ng = row(b_norm[0])
    section = lambda j: (w_in, (D_MODEL, B_WIDTH), (0, j))
    rows = [(xs, D_MODEL, 0), (pos_col, 1, 0)]
    (q,) = _rows_call(_b_qk_kernel, rows, (ng, section(0), row(b_q_gain[0] * b_scale), *q_ext),
                      (B_HEADS * B_HEAD_PAD,), (BF16,), tm=1024, name="b_q_proj")
    (k,) = _rows_call(_b_qk_kernel, rows, (ng, section(1), row(b_k_gain[0]), *k_ext),
                      (B_HEADS * B_HEAD_PAD,), (BF16,), tm=1024, name="b_k_proj")
    v, sg = _rows_call(_b_vg_kernel, rows[:1], (ng, section(2), section(3)),
                       (B_WIDTH, B_WIDTH), (BF16, BF16), tm=512, name="b_vg_proj")
    b_consts = (neg_slopes, jnp.broadcast_to(b_bound.reshape(1, 1), (1, LANES)),
                row(b_lambda_q1[0]), row(b_lambda_k1[0]), row(b_lambda_q2[0]), row(b_lambda_k2[0]),
                row(b_subln[0]))
    o = _attend(_b_attn, b_bound, q, k, v, pos, pos_col, shift_tables, b_consts, lam_init)
    xs = _out_proj(xs, o, sg, b_w_out[0].astype(BF16), name="b_out_proj")
    return xs[None]
```

```python
import functools
import math

import jax
import jax.numpy as jnp
from jax import lax
from jax.experimental import pallas as pl
from jax.experimental.pallas import tpu as pltpu

D_MODEL = 2048
CHUNK_SHIFT = 6
EPS = 1e-6
LOG2E = 1.4426950408889634

A_HEADS = 16
A_NOPE = 128
A_ROPE = 64
A_QK = A_NOPE + A_ROPE
A_V = 128
A_Q_LORA = 512
A_KV_LORA = 512
A_WIDTH = A_HEADS * A_V
A_HEAD_PAD = 256
A_V_PAD = 256
A_ONE_COL = A_V + A_ROPE
ROPE_THETA = 10000.0

B_HEADS = 8
B_HD = 128
B_V = 2 * B_HD
B_WIDTH = B_HEADS * B_V
B_QK = B_HEADS * 2 * B_HD
B_HEAD_PAD = 3 * B_HD
POS_SPLIT_SHIFT = 7

LANES = 128
NEG_BIG = -1e30
MAX_CONST_SHIFT = 50.0
BOUND_MARGIN = 1.01
ATTN_TILES = (1024, 1024)
ATTN_TILES_ONLINE = (512, 1024)
A_QUERY_GROUP = 4
V7X_VMEM_BYTES = 64 * 1024 * 1024
VMEM_LIMIT_BYTES = V7X_VMEM_BYTES - 8 * 1024 * 1024

F32 = jnp.float32
BF16 = jnp.bfloat16
_NT = (((1,), (1,)), ((), ()))


def _const_spec(a, block_shape=None, block_index=None):
    shape = a.shape if block_shape is None else block_shape
    index = (0,) * len(shape) if block_index is None else block_index
    return pl.BlockSpec(shape, lambda *_: index, pipeline_mode=pl.Buffered(1))


def _rms(x, denom):
    return lax.rsqrt(jnp.sum(x * x, axis=-1, keepdims=True) * (1.0 / denom) + EPS)


def _a_prologue_kernel(x_ref, pos_ref, ng_ref, wa_ref, wg_ref, qng_ref, wq_ref, kvng_ref, wkv_ref,
                       gq_ref, gkn_ref, gkp_ref, invf_ref, sgn_ref, qshift_ref, one_ref,
                       q_ref, k_ref, v_ref, sg_ref):
    x = x_ref[...]
    xn = (x * _rms(x, D_MODEL) * ng_ref[...]).astype(BF16)
    za = jnp.dot(xn, wa_ref[...], preferred_element_type=F32)
    gate = jnp.dot(xn, wg_ref[...], preferred_element_type=F32)
    sg_ref[...] = (gate * jax.nn.sigmoid(gate)).astype(BF16)
    rope = _rope_fn(pos_ref, invf_ref, sgn_ref)
    _a_q_rows(za[:, :A_Q_LORA], rope, qng_ref, wq_ref, gq_ref, qshift_ref, q_ref)
    _a_kv_rows(za[:, A_Q_LORA:A_Q_LORA + A_KV_LORA], za[:, A_Q_LORA + A_KV_LORA:], rope,
               kvng_ref, wkv_ref, gkn_ref, gkp_ref, one_ref, k_ref, v_ref)


def _rope_fn(pos_ref, invf_ref, sgn_ref):
    ang = pos_ref[...].astype(F32) * invf_ref[...]
    cos = jnp.cos(ang)
    sin_signed = jnp.sin(ang) * sgn_ref[...]
    lane = lax.broadcasted_iota(jnp.int32, cos.shape, 1)
    first_half = lane < (A_ROPE // 2)

    def rope(t):
        swapped = jnp.where(first_half, pltpu.roll(t, LANES - A_ROPE // 2, 1),
                            pltpu.roll(t, A_ROPE // 2, 1))
        return t * cos + swapped * sin_signed

    return rope


def _a_q_rows(cq, rope, qng_ref, wq_ref, gq_ref, qshift_ref, q_ref):
    cqn = (cq * _rms(cq, A_Q_LORA) * qng_ref[...]).astype(BF16)
    q = jnp.dot(cqn, wq_ref[...], preferred_element_type=F32)
    gq = gq_ref[...]
    qshift = qshift_ref[...]
    for h in range(A_HEADS):
        qh = q[:, h * A_HEAD_PAD:(h + 1) * A_HEAD_PAD]
        qn = qh * _rms(qh, A_QK) * gq
        q_ref[:, h * A_HEAD_PAD:h * A_HEAD_PAD + LANES] = qn[:, :LANES].astype(BF16)
        q_ref[:, h * A_HEAD_PAD + LANES:(h + 1) * A_HEAD_PAD] = (
            rope(qn[:, LANES:]) + qshift).astype(BF16)


def _a_kv_rows(ckv, kpe, rope, kvng_ref, wkv_ref, gkn_ref, gkp_ref, one_ref, k_ref, v_ref):
    ckvn = (ckv * _rms(ckv, A_KV_LORA) * kvng_ref[...]).astype(BF16)
    kv = jnp.dot(ckvn, wkv_ref[...], preferred_element_type=F32)
    gkn = gkn_ref[...]
    one = one_ref[...]
    v_one = jnp.broadcast_to(one, kpe.shape).astype(BF16)
    kpe_ss = jnp.sum(kpe * kpe, axis=-1, keepdims=True)
    kpe_roped = rope(kpe * gkp_ref[...])
    for h in range(A_HEADS):
        kn = kv[:, h * A_NOPE:(h + 1) * A_NOPE]
        r = lax.rsqrt((jnp.sum(kn * kn, axis=-1, keepdims=True) + kpe_ss) * (1.0 / A_QK) + EPS)
        k_ref[:, h * A_HEAD_PAD:h * A_HEAD_PAD + LANES] = (kn * r * gkn).astype(BF16)
        k_ref[:, h * A_HEAD_PAD + LANES:(h + 1) * A_HEAD_PAD] = (kpe_roped * r + one).astype(BF16)

        v_ref[:, h * A_V_PAD:h * A_V_PAD + A_V] = (
            kv[:, A_HEADS * A_NOPE + h * A_V:A_HEADS * A_NOPE + (h + 1) * A_V].astype(BF16))
        v_ref[:, h * A_V_PAD + A_V:(h + 1) * A_V_PAD] = v_one


def _rows_call(kernel_fn, row_inputs, consts, out_widths, out_dtypes, tm, name):
    s = row_inputs[0][0].shape[0]
    in_specs = [pl.BlockSpec((tm, w), lambda i, cb=cb: (i, cb)) for _, w, cb in row_inputs]
    in_specs += [_const_spec(*c) if isinstance(c, tuple) else _const_spec(c) for c in consts]
    out_specs = [pl.BlockSpec((tm, w), lambda i: (i, 0)) for w in out_widths]
    out_shape = [jax.ShapeDtypeStruct((s, w), dt) for w, dt in zip(out_widths, out_dtypes)]
    return pl.pallas_call(
        kernel_fn,
        grid=(s // tm,),
        in_specs=in_specs,
        out_specs=out_specs,
        out_shape=out_shape,
        compiler_params=pltpu.CompilerParams(dimension_semantics=("arbitrary",),
                                             vmem_limit_bytes=VMEM_LIMIT_BYTES),
        name=name,
    )(*[a for a, _, _ in row_inputs], *[c[0] if isinstance(c, tuple) else c for c in consts])


def _a_prologue(x, pos_col, consts):
    wide = A_HEADS * A_HEAD_PAD
    return _rows_call(_a_prologue_kernel, [(x, D_MODEL, 0), (pos_col, 1, 0)], consts,
                      (wide, wide, A_HEADS * A_V_PAD, A_WIDTH), (BF16, BF16, BF16, BF16),
                      tm=256, name="a_prologue")


def _b_qk_kernel(x_ref, pos_ref, ng_ref, w_ref, g_ref, extc_ref, ea_ref, eb_ref, out_ref):
    x = x_ref[...]
    xn = (x * _rms(x, D_MODEL) * ng_ref[...]).astype(BF16)
    z = jnp.dot(xn, w_ref[...], preferred_element_type=F32)
    pos = pos_ref[...]
    a = jnp.right_shift(pos, POS_SPLIT_SHIFT).astype(F32)
    b = jnp.bitwise_and(pos, (1 << POS_SPLIT_SHIFT) - 1).astype(F32)
    ext_ab = a * ea_ref[...] + b * eb_ref[...]
    g = g_ref[...]
    for h in range(B_HEADS):
        for c in range(2):
            t = z[:, (2 * h + c) * B_HD:(2 * h + c + 1) * B_HD]
            out_ref[:, h * B_HEAD_PAD + c * B_HD:h * B_HEAD_PAD + (c + 1) * B_HD] = (
                t * _rms(t, B_HD) * g).astype(BF16)
        out_ref[:, h * B_HEAD_PAD + 2 * B_HD:(h + 1) * B_HEAD_PAD] = (
            extc_ref[h:h + 1, :] + ext_ab).astype(BF16)


def _b_vg_kernel(x_ref, ng_ref, wv_ref, wg_ref, v_ref, sg_ref):
    x = x_ref[...]
    xn = (x * _rms(x, D_MODEL) * ng_ref[...]).astype(BF16)
    v_ref[...] = jnp.dot(xn, wv_ref[...], preferred_element_type=F32).astype(BF16)
    gate = jnp.dot(xn, wg_ref[...], preferred_element_type=F32)
    sg_ref[...] = (gate * jax.nn.sigmoid(gate)).astype(BF16)


def _out_proj_kernel(x_ref, o_ref, sg_ref, w_ref, y_ref):
    a = o_ref[...] * sg_ref[...]
    y_ref[...] = x_ref[...] + jnp.dot(a, w_ref[...], preferred_element_type=F32)


def _out_proj(x, o, sg, w, name):
    (y,) = _rows_call(_out_proj_kernel, [(x, D_MODEL, 0), (o, D_MODEL, 0), (sg, D_MODEL, 0)], (w,),
                      (D_MODEL,), (F32,), tm=512, name=name)
    return y


def _tile_tables(pos, tq, tk):
    s = pos.shape[0]
    nq, nk = s // tq, s // tk
    ch = jnp.right_shift(pos, CHUNK_SHIFT)
    qmax_ch = ch.reshape(nq, tq).max(axis=1)
    kmin_ch = ch.reshape(nk, tk).min(axis=1)
    qmin = pos.reshape(nq, tq).min(axis=1)
    kmax = pos.reshape(nk, tk).max(axis=1)
    in_range = (pos.min() >= 0) & (pos.max() < (1 << (POS_SPLIT_SHIFT + 8)))
    visible = kmin_ch[None, :] <= qmax_ch[:, None]
    linear = visible & (kmax[None, :] <= qmin[:, None]) & in_range
    general = visible & jnp.logical_not(linear)
    rank = jnp.where(general, 0, jnp.where(linear, 1, 2))
    order = jnp.argsort(rank, axis=1, stable=True)
    n_general = general.sum(axis=1)
    n_visible = visible.sum(axis=1)
    qmax_top = ch.reshape(nq, 2, tq // 2)[:, 0].max(axis=1)
    kmin_right = ch.reshape(nk, 2, tk // 2)[:, 1].min(axis=1)
    quarter = kmin_right[None, :] > qmax_top[:, None]
    last_general = jnp.take_along_axis(order, jnp.maximum(n_general - 1, 0)[:, None], axis=1)
    quarter_last_general = jnp.take_along_axis(quarter, last_general, axis=1)[:, 0] & (n_general > 0)
    i32 = lambda a: a.astype(jnp.int32)
    return dict(visible=visible, n_general=i32(n_general), n_visible=i32(n_visible),
                order=i32(order).reshape(-1),
                quarter_last_general=i32(quarter_last_general))


def _pipelined_tiles(n, tile_at, produce, consume, first_produced=False, body_steps=(4, 2)):
    if not first_produced:
        produce(tile_at(0), 0)

    def two_steps(t):
        produce(tile_at(t + 1), 1)
        consume(tile_at(t), 0)
        produce(tile_at(t + 2), 0)
        consume(tile_at(t + 1), 1)

    done = 0
    for steps in body_steps:
        def body(u, c, steps=steps, done=done):
            for j in range(0, steps, 2):
                two_steps(done + steps * u + j)
            return c

        trips = (n - 1 - done) // steps
        lax.fori_loop(0, trips, body, 0)
        done = done + steps * trips
    last = n - 1

    @pl.when(last % 2 == 1)
    def _():
        produce(tile_at(last), 1)
        consume(tile_at(last - 1), 0)
        consume(tile_at(last), 1)

    @pl.when(last % 2 == 0)
    def _():
        consume(tile_at(last), 0)


def _lane_group_sum(p):
    out = p[:, :LANES]
    for j in range(1, p.shape[1] // LANES):
        out = out + p[:, j * LANES:(j + 1) * LANES]
    return out


def _a_attn_kernel(cnt_ref, list_ref, q_ref, k_ref, v_ref, pq_ref, pk_ref, o_ref,
                   p_sc, acc_sc, m_sc, alpha_sc, *, tq, tk, nk, group, online):
    step = pl.program_id(1)
    acc_sc[...] = jnp.zeros_like(acc_sc)
    if online:
        m_sc[...] = jnp.full_like(m_sc, NEG_BIG)

    def item_at(t):
        code = list_ref[step * (group * nk) + t]
        return code // nk, code % nk

    def produce(item, slot):
        sub, ki = item
        rows = pl.ds(pl.multiple_of(sub * tq, tq), tq)
        k = k_ref[pl.ds(pl.multiple_of(ki * tk, tk), tk), :]
        s = lax.dot_general(q_ref[rows, :], k, _NT, preferred_element_type=F32)
        qchunk = jnp.right_shift(pq_ref[rows, :], CHUNK_SHIFT)
        kchunk = jnp.right_shift(pk_ref[ki], CHUNK_SHIFT)
        s = jnp.where(kchunk <= qchunk, s, NEG_BIG)
        if online:
            m_prev = m_sc[sub]
            m_new = jnp.maximum(m_prev, jnp.max(s, axis=1, keepdims=True))
            alpha_sc[slot] = jnp.exp2(m_prev - m_new)
            m_sc[sub] = m_new
            s = s - m_new
        p_sc[slot] = jnp.exp2(s).astype(BF16)

    def consume(item, slot):
        sub, ki = item
        v = v_ref[pl.ds(pl.multiple_of(ki * tk, tk), tk), :]
        pv = jnp.dot(p_sc[slot], v, preferred_element_type=F32)
        if online:
            acc_sc[sub] = alpha_sc[slot] * acc_sc[sub] + pv
        else:
            acc_sc[sub] += pv

    _pipelined_tiles(cnt_ref[step], item_at, produce, consume,
                     body_steps=(4, 2) if online else (8, 4, 2))
    for sub in range(group):
        acc = acc_sc[sub]
        o_ref[sub * tq:(sub + 1) * tq, :] = (
            acc[:, :A_V] / acc[:, A_ONE_COL:A_ONE_COL + 1]).astype(BF16)


def _a_attn(q, k, v, pos, pos_col, shift_tables, online):
    s = q.shape[0]
    tq, tk = ATTN_TILES_ONLINE if online else ATTN_TILES
    nq, nk = s // tq, s // tk
    group = math.gcd(A_QUERY_GROUP, nq)
    t = _tile_tables(pos, tq, tk) if online else shift_tables
    flat = t["visible"].reshape(nq // group, group * nk)
    items = jnp.argsort(jnp.logical_not(flat), axis=1, stable=True).astype(jnp.int32).reshape(-1)
    counts = flat.sum(axis=1).astype(jnp.int32)
    pos_k3 = pos.reshape(nk, 1, tk)
    tg = group * tq
    grid_spec = pltpu.PrefetchScalarGridSpec(
        num_scalar_prefetch=2,
        grid=(A_HEADS, nq // group),
        in_specs=[pl.BlockSpec((tg, A_HEAD_PAD), lambda h, i, *_: (i, h)),
                  pl.BlockSpec((s, A_HEAD_PAD), lambda h, i, *_: (0, h)),
                  pl.BlockSpec((s, A_V_PAD), lambda h, i, *_: (0, h)),
                  pl.BlockSpec((tg, 1), lambda h, i, *_: (i, 0)),
                  pl.BlockSpec((nk, 1, tk), lambda h, i, *_: (0, 0, 0))],
        out_specs=pl.BlockSpec((tg, A_V), lambda h, i, *_: (i, h)),
        scratch_shapes=[pltpu.VMEM((2, tq, tk), BF16), pltpu.VMEM((group, tq, A_V_PAD), F32),
                        pltpu.VMEM((group, tq, 1), F32), pltpu.VMEM((2, tq, 1), F32)],
    )
    return pl.pallas_call(
        functools.partial(_a_attn_kernel, tq=tq, tk=tk, nk=nk, group=group, online=online),
        grid_spec=grid_spec,
        out_shape=jax.ShapeDtypeStruct((s, A_WIDTH), BF16),
        compiler_params=pltpu.CompilerParams(dimension_semantics=("arbitrary", "arbitrary"),
                                             vmem_limit_bytes=VMEM_LIMIT_BYTES),
        name="a_attention_online" if online else "a_attention",
    )(counts, items, q, k, v, pos_col, pos_k3)


def _b_attn_kernel(gcnt_ref, cnt_ref, list_ref, quarter_ref, q_ref, k_ref, v_ref, pq_ref, pk_ref,
                   slope_ref, shift_ref, lq1_ref, lk1_ref, lq2_ref, lk2_ref, sub_ref, o_ref,
                   p_sc, acc_sc, l_sc, m_sc, alpha_sc, *, tk, nk, lam_init, online):
    qi = pl.program_id(1)
    q = q_ref[...]
    tq = q.shape[0]
    hq, hk = tq // 2, tk // 2
    q_ext = q[:, 2 * B_HD:]
    q_full = [jnp.concatenate([q[:, c * B_HD:(c + 1) * B_HD], q_ext], axis=1) for c in range(2)]
    pq = pq_ref[...]
    qchunk = jnp.right_shift(pq, CHUNK_SHIFT)
    neg_slope = slope_ref[0, :, 0:1]
    shift = shift_ref[0:1, 0:1]
    l_sc[...] = jnp.zeros_like(l_sc)
    acc_sc[...] = jnp.zeros_like(acc_sc)
    if online:
        m_sc[...] = jnp.full_like(m_sc, NEG_BIG)

    def finish(s, c, slot):
        if online:
            m_prev = m_sc[c]
            m_new = jnp.maximum(m_prev, jnp.max(s, axis=1, keepdims=True))
            alpha = jnp.exp2(m_prev - m_new)
            alpha_sc[slot, c] = alpha
            m_sc[c] = m_new
            p = jnp.exp2(s - m_new)
            l_sc[c] = alpha * l_sc[c] + _lane_group_sum(p)
        else:
            p = jnp.exp2(s)
            l_sc[c] += _lane_group_sum(p)
        p_sc[slot, c] = p.astype(BF16)

    def produce_linear(ki, slot):
        k = k_ref[pl.ds(pl.multiple_of(ki * tk, tk), tk), :]
        k_ext = k[:, 2 * B_HD:]
        for c in range(2):
            k_full = jnp.concatenate([k[:, c * B_HD:(c + 1) * B_HD], k_ext], axis=1)
            finish(lax.dot_general(q_full[c], k_full, _NT, preferred_element_type=F32), c, slot)

    def produce_general(ki, slot):
        k = k_ref[pl.ds(pl.multiple_of(ki * tk, tk), tk), :]
        pk = pk_ref[ki]
        bias = jnp.abs(pq - pk).astype(F32) * neg_slope - shift
        bias = jnp.where(jnp.right_shift(pk, CHUNK_SHIFT) <= qchunk, bias, NEG_BIG)
        for c in range(2):
            s = lax.dot_general(q[:, c * B_HD:(c + 1) * B_HD], k[:, c * B_HD:(c + 1) * B_HD],
                                _NT, preferred_element_type=F32) + bias
            finish(s, c, slot)

    def produce_general_skip_quarter(ki, slot):
        start = pl.multiple_of(ki * tk, tk)
        pk = pk_ref[ki]
        regions = ((slice(0, tq), slice(0, hk)), (slice(hq, tq), slice(hk, tk)))
        for rows, cols in regions:
            k = k_ref[pl.ds(start + cols.start, hk), :]
            pkc = pk[:, cols]
            bias = jnp.abs(pq[rows] - pkc).astype(F32) * neg_slope - shift
            bias = jnp.where(jnp.right_shift(pkc, CHUNK_SHIFT) <= qchunk[rows], bias, NEG_BIG)
            for c in range(2):
                s = lax.dot_general(q[rows, c * B_HD:(c + 1) * B_HD], k[:, c * B_HD:(c + 1) * B_HD],
                                    _NT, preferred_element_type=F32) + bias
                p = jnp.exp2(s)
                l_sc[c, rows, :] += _lane_group_sum(p)
                p_sc[slot, c, rows, cols] = p.astype(BF16)
        for c in range(2):
            p_sc[slot, c, :hq, hk:] = jnp.zeros((hq, hk), BF16)

    def consume(ki, slot):
        v = v_ref[pl.ds(pl.multiple_of(ki * tk, tk), tk), :]
        for c in range(2):
            pv = jnp.dot(p_sc[slot, c], v, preferred_element_type=F32)
            if online:
                acc_sc[c] = alpha_sc[slot, c] * acc_sc[c] + pv
            else:
                acc_sc[c] += pv

    n_general = gcnt_ref[qi]
    tile_at = lambda t: list_ref[qi * nk + t]

    def general_body(t, c):
        produce_general(tile_at(t), 0)
        consume(tile_at(t), 0)
        return c

    first = jnp.maximum(n_general - 1, 0)
    lax.fori_loop(0, first, general_body, 0)

    skip_quarter = False if online else quarter_ref[qi] != 0

    @pl.when((n_general > 0) & jnp.logical_not(skip_quarter))
    def _():
        produce_general(tile_at(first), 0)

    if not online:
        @pl.when((n_general > 0) & skip_quarter)
        def _():
            produce_general_skip_quarter(tile_at(first), 0)

    @pl.when(n_general == 0)
    def _():
        produce_linear(tile_at(first), 0)

    _pipelined_tiles(cnt_ref[qi] - first, lambda t: tile_at(first + t), produce_linear, consume,
                     first_produced=True, body_steps=(4, 2) if online else (8, 4, 2))

    lam = (jnp.exp(jnp.sum(lq1_ref[...] * lk1_ref[...], axis=-1, keepdims=True))
           - jnp.exp(jnp.sum(lq2_ref[...] * lk2_ref[...], axis=-1, keepdims=True)) + lam_init)
    l0 = jnp.sum(l_sc[0], axis=-1, keepdims=True)
    l1 = jnp.sum(l_sc[1], axis=-1, keepdims=True)
    od = acc_sc[0] / l0 - lam * (acc_sc[1] / l1)
    o_ref[...] = (od * _rms(od, B_V) * (sub_ref[...] * (1.0 - lam_init))).astype(BF16)


def _b_attn(q, k, v, pos, pos_col, shift_tables, consts, lam_init, online):
    s = q.shape[0]
    tq, tk = ATTN_TILES_ONLINE if online else ATTN_TILES
    nq, nk = s // tq, s // tk
    t = _tile_tables(pos, tq, tk) if online else shift_tables
    tables = (t["n_general"], t["n_visible"], t["order"], t["quarter_last_general"])
    pos_k3 = pos.reshape(nk, 1, tk)
    slopes = consts[0]
    grid_spec = pltpu.PrefetchScalarGridSpec(
        num_scalar_prefetch=4,
        grid=(B_HEADS, nq),
        in_specs=[pl.BlockSpec((tq, B_HEAD_PAD), lambda h, i, *_: (i, h)),
                  pl.BlockSpec((s, B_HEAD_PAD), lambda h, i, *_: (0, h),
                               pipeline_mode=pl.Buffered(1)),
                  pl.BlockSpec((s, B_V), lambda h, i, *_: (0, h)),
                  pl.BlockSpec((tq, 1), lambda h, i, *_: (i, 0)),
                  pl.BlockSpec((nk, 1, tk), lambda h, i, *_: (0, 0, 0)),
                  pl.BlockSpec((1,) + slopes.shape[1:], lambda h, i, *_: (h, 0, 0))]
                 + [pl.BlockSpec(c.shape, lambda h, i, *_: (0, 0)) for c in consts[1:]],
        out_specs=pl.BlockSpec((tq, B_V), lambda h, i, *_: (i, h)),
        scratch_shapes=[pltpu.VMEM((2, 2, tq, tk), BF16), pltpu.VMEM((2, tq, B_V), F32),
                        pltpu.VMEM((2, tq, LANES), F32), pltpu.VMEM((2, tq, 1), F32),
                        pltpu.VMEM((2, 2, tq, 1), F32)],
    )
    return pl.pallas_call(
        functools.partial(_b_attn_kernel, tk=tk, nk=nk, lam_init=lam_init, online=online),
        grid_spec=grid_spec,
        out_shape=jax.ShapeDtypeStruct((s, B_WIDTH), BF16),
        compiler_params=pltpu.CompilerParams(dimension_semantics=("arbitrary", "arbitrary"),
                                             vmem_limit_bytes=VMEM_LIMIT_BYTES),
        name="b_attention_online" if online else "b_attention",
    )(*tables, q, k, v, pos_col, pos_k3, *consts)


def _attend(fn, shift_bound, *args, **kwargs):
    return lax.cond(shift_bound <= MAX_CONST_SHIFT,
                    lambda: fn(*args, online=False, **kwargs),
                    lambda: fn(*args, online=True, **kwargs))


def _lane_onehot(value, lane):
    return jnp.zeros((1, LANES), F32).at[0, lane].set(value)


def _a_consts(norm_g, w_in, q_norm_g, w_q_up, kv_norm_g, w_kv_up, q_gain, k_gain):
    n_lat = A_Q_LORA + A_KV_LORA + A_ROPE
    wa = jnp.pad(w_in[:, :n_lat], ((0, 0), (0, LANES - A_ROPE))).astype(BF16)
    wg = w_in[:, n_lat:].astype(BF16)
    wq = w_q_up.reshape(A_Q_LORA, A_HEADS, A_QK)
    wq = jnp.pad(wq, ((0, 0), (0, 0), (0, A_HEAD_PAD - A_QK)))
    wq = wq.reshape(A_Q_LORA, A_HEADS * A_HEAD_PAD).astype(BF16)
    wkv = w_kv_up.reshape(A_KV_LORA, A_HEADS, A_NOPE + A_V)
    wkv = jnp.concatenate([wkv[:, :, :A_NOPE].reshape(A_KV_LORA, -1),
                           wkv[:, :, A_NOPE:].reshape(A_KV_LORA, -1)], axis=1).astype(BF16)
    q_scale = LOG2E / math.sqrt(A_QK)
    gq = jnp.pad(q_gain * q_scale, (0, A_HEAD_PAD - A_QK)).reshape(1, A_HEAD_PAD)
    gkn = k_gain[:A_NOPE].reshape(1, LANES)
    gkp = jnp.pad(k_gain[A_NOPE:], (0, LANES - A_ROPE)).reshape(1, LANES)
    half = A_ROPE // 2
    inv_freq = ROPE_THETA ** (-jnp.arange(0, A_ROPE, 2, dtype=F32) / A_ROPE)
    zeros = jnp.zeros((LANES - A_ROPE,), F32)
    invf = jnp.concatenate([inv_freq, inv_freq, zeros]).reshape(1, LANES)
    sgn = jnp.concatenate([-jnp.ones((half,), F32), jnp.ones((half,), F32), zeros]).reshape(1, LANES)
    bound = (BOUND_MARGIN * A_QK * q_scale
             * jnp.max(jnp.abs(q_gain)) * jnp.max(jnp.abs(k_gain)))
    bound = bound.astype(BF16).astype(F32)
    qshift = _lane_onehot(-bound, A_ROPE)
    one = _lane_onehot(1.0, A_ROPE)
    consts = (norm_g.reshape(1, -1), wa, wg, q_norm_g.reshape(1, -1), wq, kv_norm_g.reshape(1, -1),
              wkv, gq, gkn, gkp, invf, sgn, qshift, one)
    return consts, bound


def _b_consts(q_gain, k_gain):
    b_scale = LOG2E / math.sqrt(B_HD)
    bound = (BOUND_MARGIN * B_HD * b_scale * jnp.max(jnp.abs(q_gain)) * jnp.max(jnp.abs(k_gain)))
    bound = bound.astype(BF16).astype(F32)
    slopes = 2.0 ** (-8.0 * jnp.arange(1, B_HEADS + 1, dtype=F32) / B_HEADS)
    sigma = LOG2E * slopes
    sig_hi = sigma.astype(BF16).astype(F32)
    sig_lo = (sigma - sig_hi).astype(BF16).astype(F32)
    big = float(1 << POS_SPLIT_SHIFT)
    sig_cols = jnp.stack([sig_hi * big, sig_hi, sig_lo * big, sig_lo], axis=1)
    zeros = lambda n: jnp.zeros((B_HEADS, n), F32)
    extc_q = jnp.concatenate([jnp.broadcast_to(-bound, (B_HEADS, 1)), sig_cols, zeros(LANES - 5)],
                             axis=1)
    extc_k = jnp.concatenate([jnp.ones((B_HEADS, 1), F32), zeros(4), sig_cols, zeros(LANES - 9)],
                             axis=1)
    lanes = jnp.arange(LANES)
    pick = lambda ids, val: jnp.where(jnp.isin(lanes, jnp.array(ids)), val, 0.0).reshape(1, LANES)
    ea_q, eb_q = pick([5, 7], -1.0), pick([6, 8], -1.0)
    ea_k, eb_k = pick([1, 3], 1.0), pick([2, 4], 1.0)
    neg_slopes = jnp.broadcast_to(-(sig_hi + sig_lo)[:, None, None], (B_HEADS, 1, LANES))
    return bound, b_scale, neg_slopes, (extc_q, ea_q, eb_q), (extc_k, ea_k, eb_k)


def kernel(x, positions, a_norm, a_w_in, a_q_norm, a_w_q_up, a_kv_norm, a_w_kv_up, a_q_gain, a_k_gain, a_w_out, b_norm, b_w_in, b_q_gain, b_k_gain, b_lambda_q1, b_lambda_k1, b_lambda_q2, b_lambda_k2, b_subln, b_w_out):
    batch, seq, _ = x.shape
    assert batch == 1
    xs = x[0]
    pos = positions[0]
    pos_col = pos.reshape(seq, 1)
    row = lambda a: a.reshape(1, -1)

    consts, a_bound = _a_consts(a_norm[0], a_w_in[0], a_q_norm[0], a_w_q_up[0], a_kv_norm[0],
                                a_w_kv_up[0], a_q_gain[0], a_k_gain[0])
    q, k, v, sg = _a_prologue(xs, pos_col, consts)
    shift_tables = _tile_tables(pos, *ATTN_TILES)
    o = _attend(_a_attn, a_bound, q, k, v, pos, pos_col, shift_tables)
    xs = _out_proj(xs, o, sg, a_w_out[0].astype(BF16), name="a_out_proj")

    layer_idx = 1
    lam_init = 0.8 - 0.6 * math.exp(-0.3 * layer_idx)
    b_bound, b_scale, neg_slopes, q_ext, k_ext = _b_consts(b_q_gain[0], b_k_gain[0])
    w_in = b_w_in[0].astype(BF16)
    ng = row(b_norm[0])
    section = lambda j: (w_in, (D_MODEL, B_WIDTH), (0, j))
    rows = [(xs, D_MODEL, 0), (pos_col, 1, 0)]
    (q,) = _rows_call(_b_qk_kernel, rows, (ng, section(0), row(b_q_gain[0] * b_scale), *q_ext),
                      (B_HEADS * B_HEAD_PAD,), (BF16,), tm=1024, name="b_q_proj")
    (k,) = _rows_call(_b_qk_kernel, rows, (ng, section(1), row(b_k_gain[0]), *k_ext),
                      (B_HEADS * B_HEAD_PAD,), (BF16,), tm=1024, name="b_k_proj")
    v, sg = _rows_call(_b_vg_kernel, rows[:1], (ng, section(2), section(3)),
                       (B_WIDTH, B_WIDTH), (BF16, BF16), tm=512, name="b_vg_proj")
    b_consts = (neg_slopes, jnp.broadcast_to(b_bound.reshape(1, 1), (1, LANES)),
                row(b_lambda_q1[0]), row(b_lambda_k1[0]), row(b_lambda_q2[0]), row(b_lambda_k2[0]),
                row(b_subln[0]))
    o = _attend(_b_attn, b_bound, q, k, v, pos, pos_col, shift_tables, b_consts, lam_init)
    xs = _out_proj(xs, o, sg, b_w_out[0].astype(BF16), name="b_out_proj")
    return xs[None]
```

```python
import functools
import math

import jax
import jax.numpy as jnp
from jax import lax
from jax.experimental import pallas as pl
from jax.experimental.pallas import tpu as pltpu

D_MODEL = 2048
CHUNK_SHIFT = 6
EPS = 1e-6
LOG2E = 1.4426950408889634

A_HEADS = 16
A_NOPE = 128
A_ROPE = 64
A_QK = A_NOPE + A_ROPE
A_V = 128
A_Q_LORA = 512
A_KV_LORA = 512
A_WIDTH = A_HEADS * A_V
A_HEAD_PAD = 256
A_V_PAD = 256
A_ONE_COL = A_V + A_ROPE
ROPE_THETA = 10000.0

B_HEADS = 8
B_HD = 128
B_V = 2 * B_HD
B_WIDTH = B_HEADS * B_V
B_QK = B_HEADS * 2 * B_HD
B_HEAD_PAD = 3 * B_HD
POS_SPLIT_SHIFT = 7

LANES = 128
NEG_BIG = -1e30
MAX_CONST_SHIFT = 50.0
BOUND_MARGIN = 1.01
ATTN_TILES = (1024, 1024)
ATTN_TILES_ONLINE = (512, 1024)
A_QUERY_GROUP = 4
V7X_VMEM_BYTES = 64 * 1024 * 1024
VMEM_LIMIT_BYTES = V7X_VMEM_BYTES - 8 * 1024 * 1024

F32 = jnp.float32
BF16 = jnp.bfloat16
_NT = (((1,), (1,)), ((), ()))


def _const_spec(a, block_shape=None, block_index=None):
    shape = a.shape if block_shape is None else block_shape
    index = (0,) * len(shape) if block_index is None else block_index
    return pl.BlockSpec(shape, lambda *_: index, pipeline_mode=pl.Buffered(1))


def _rms(x, denom):
    return lax.rsqrt(jnp.sum(x * x, axis=-1, keepdims=True) * (1.0 / denom) + EPS)


def _a_prologue_kernel(x_ref, pos_ref, ng_ref, wa_ref, wg_ref, qng_ref, wq_ref, kvng_ref, wkv_ref,
                       gq_ref, gkn_ref, gkp_ref, invf_ref, sgn_ref, qshift_ref, one_ref,
                       q_ref, k_ref, v_ref, sg_ref):
    x = x_ref[...]
    xn = (x * _rms(x, D_MODEL) * ng_ref[...]).astype(BF16)
    za = jnp.dot(xn, wa_ref[...], preferred_element_type=F32)
    gate = jnp.dot(xn, wg_ref[...], preferred_element_type=F32)
    sg_ref[...] = (gate * jax.nn.sigmoid(gate)).astype(BF16)
    rope = _rope_fn(pos_ref, invf_ref, sgn_ref)
    _a_q_rows(za[:, :A_Q_LORA], rope, qng_ref, wq_ref, gq_ref, qshift_ref, q_ref)
    _a_kv_rows(za[:, A_Q_LORA:A_Q_LORA + A_KV_LORA], za[:, A_Q_LORA + A_KV_LORA:], rope,
               kvng_ref, wkv_ref, gkn_ref, gkp_ref, one_ref, k_ref, v_ref)


def _rope_fn(pos_ref, invf_ref, sgn_ref):
    ang = pos_ref[...].astype(F32) * invf_ref[...]
    cos = jnp.cos(ang)
    sin_signed = jnp.sin(ang) * sgn_ref[...]
    lane = lax.broadcasted_iota(jnp.int32, cos.shape, 1)
    first_half = lane < (A_ROPE // 2)

    def rope(t):
        swapped = jnp.where(first_half, pltpu.roll(t, LANES - A_ROPE // 2, 1),
                            pltpu.roll(t, A_ROPE // 2, 1))
        return t * cos + swapped * sin_signed

    return rope


def _a_q_rows(cq, rope, qng_ref, wq_ref, gq_ref, qshift_ref, q_ref):
    cqn = (cq * _rms(cq, A_Q_LORA) * qng_ref[...]).astype(BF16)
    q = jnp.dot(cqn, wq_ref[...], preferred_element_type=F32)
    gq = gq_ref[...]
    qshift = qshift_ref[...]
    for h in range(A_HEADS):
        qh = q[:, h * A_HEAD_PAD:(h + 1) * A_HEAD_PAD]
        qn = qh * _rms(qh, A_QK) * gq
        q_ref[:, h * A_HEAD_PAD:h * A_HEAD_PAD + LANES] = qn[:, :LANES].astype(BF16)
        q_ref[:, h * A_HEAD_PAD + LANES:(h + 1) * A_HEAD_PAD] = (
            rope(qn[:, LANES:]) + qshift).astype(BF16)


def _a_kv_rows(ckv, kpe, rope, kvng_ref, wkv_ref, gkn_ref, gkp_ref, one_ref, k_ref, v_ref):
    ckvn = (ckv * _rms(ckv, A_KV_LORA) * kvng_ref[...]).astype(BF16)
    kv = jnp.dot(ckvn, wkv_ref[...], preferred_element_type=F32)
    gkn = gkn_ref[...]
    one = one_ref[...]
    v_one = jnp.broadcast_to(one, kpe.shape).astype(BF16)
    kpe_ss = jnp.sum(kpe * kpe, axis=-1, keepdims=True)
    kpe_roped = rope(kpe * gkp_ref[...])
    for h in range(A_HEADS):
        kn = kv[:, h * A_NOPE:(h + 1) * A_NOPE]
        r = lax.rsqrt((jnp.sum(kn * kn, axis=-1, keepdims=True) + kpe_ss) * (1.0 / A_QK) + EPS)
        k_ref[:, h * A_HEAD_PAD:h * A_HEAD_PAD + LANES] = (kn * r * gkn).astype(BF16)
        k_ref[:, h * A_HEAD_PAD + LANES:(h + 1) * A_HEAD_PAD] = (kpe_roped * r + one).astype(BF16)

        v_ref[:, h * A_V_PAD:h * A_V_PAD + A_V] = (
            kv[:, A_HEADS * A_NOPE + h * A_V:A_HEADS * A_NOPE + (h + 1) * A_V].astype(BF16))
        v_ref[:, h * A_V_PAD + A_V:(h + 1) * A_V_PAD] = v_one


def _rows_call(kernel_fn, row_inputs, consts, out_widths, out_dtypes, tm, name):
    s = row_inputs[0][0].shape[0]
    in_specs = [pl.BlockSpec((tm, w), lambda i, cb=cb: (i, cb)) for _, w, cb in row_inputs]
    in_specs += [_const_spec(*c) if isinstance(c, tuple) else _const_spec(c) for c in consts]
    out_specs = [pl.BlockSpec((tm, w), lambda i: (i, 0)) for w in out_widths]
    out_shape = [jax.ShapeDtypeStruct((s, w), dt) for w, dt in zip(out_widths, out_dtypes)]
    return pl.pallas_call(
        kernel_fn,
        grid=(s // tm,),
        in_specs=in_specs,
        out_specs=out_specs,
        out_shape=out_shape,
        compiler_params=pltpu.CompilerParams(dimension_semantics=("arbitrary",),
                                             vmem_limit_bytes=VMEM_LIMIT_BYTES),
        name=name,
    )(*[a for a, _, _ in row_inputs], *[c[0] if isinstance(c, tuple) else c for c in consts])


def _a_prologue(x, pos_col, consts):
    wide = A_HEADS * A_HEAD_PAD
    return _rows_call(_a_prologue_kernel, [(x, D_MODEL, 0), (pos_col, 1, 0)], consts,
                      (wide, wide, A_HEADS * A_V_PAD, A_WIDTH), (BF16, BF16, BF16, BF16),
                      tm=256, name="a_prologue")


def _b_qk_kernel(x_ref, pos_ref, ng_ref, w_ref, g_ref, extc_ref, ea_ref, eb_ref, out_ref):
    x = x_ref[...]
    xn = (x * _rms(x, D_MODEL) * ng_ref[...]).astype(BF16)
    z = jnp.dot(xn, w_ref[...], preferred_element_type=F32)
    pos = pos_ref[...]
    a = jnp.right_shift(pos, POS_SPLIT_SHIFT).astype(F32)
    b = jnp.bitwise_and(pos, (1 << POS_SPLIT_SHIFT) - 1).astype(F32)
    ext_ab = a * ea_ref[...] + b * eb_ref[...]
    g = g_ref[...]
    for h in range(B_HEADS):
        for c in range(2):
            t = z[:, (2 * h + c) * B_HD:(2 * h + c + 1) * B_HD]
            out_ref[:, h * B_HEAD_PAD + c * B_HD:h * B_HEAD_PAD + (c + 1) * B_HD] = (
                t * _rms(t, B_HD) * g).astype(BF16)
        out_ref[:, h * B_HEAD_PAD + 2 * B_HD:(h + 1) * B_HEAD_PAD] = (
            extc_ref[h:h + 1, :] + ext_ab).astype(BF16)


def _b_vg_kernel(x_ref, ng_ref, wv_ref, wg_ref, v_ref, sg_ref):
    x = x_ref[...]
    xn = (x * _rms(x, D_MODEL) * ng_ref[...]).astype(BF16)
    v_ref[...] = jnp.dot(xn, wv_ref[...], preferred_element_type=F32).astype(BF16)
    gate = jnp.dot(xn, wg_ref[...], preferred_element_type=F32)
    sg_ref[...] = (gate * jax.nn.sigmoid(gate)).astype(BF16)


def _out_proj_kernel(x_ref, o_ref, sg_ref, w_ref, y_ref):
    a = o_ref[...] * sg_ref[...]
    y_ref[...] = x_ref[...] + jnp.dot(a, w_ref[...], preferred_element_type=F32)


def _out_proj(x, o, sg, w, name):
    (y,) = _rows_call(_out_proj_kernel, [(x, D_MODEL, 0), (o, D_MODEL, 0), (sg, D_MODEL, 0)], (w,),
                      (D_MODEL,), (F32,), tm=512, name=name)
    return y


def _tile_tables(pos, tq, tk):
    s = pos.shape[0]
    nq, nk = s // tq, s // tk
    ch = jnp.right_shift(pos, CHUNK_SHIFT)
    qmax_ch = ch.reshape(nq, tq).max(axis=1)
    kmin_ch = ch.reshape(nk, tk).min(axis=1)
    qmin = pos.reshape(nq, tq).min(axis=1)
    kmax = pos.reshape(nk, tk).max(axis=1)
    in_range = (pos.min() >= 0) & (pos.max() < (1 << (POS_SPLIT_SHIFT + 8)))
    visible = kmin_ch[None, :] <= qmax_ch[:, None]
    linear = visible & (kmax[None, :] <= qmin[:, None]) & in_range
    general = visible & jnp.logical_not(linear)
    rank = jnp.where(general, 0, jnp.where(linear, 1, 2))
    order = jnp.argsort(rank, axis=1, stable=True)
    n_general = general.sum(axis=1)
    n_visible = visible.sum(axis=1)
    qmax_top = ch.reshape(nq, 2, tq // 2)[:, 0].max(axis=1)
    kmin_right = ch.reshape(nk, 2, tk // 2)[:, 1].min(axis=1)
    quarter = kmin_right[None, :] > qmax_top[:, None]
    last_general = jnp.take_along_axis(order, jnp.maximum(n_general - 1, 0)[:, None], axis=1)
    quarter_last_general = jnp.take_along_axis(quarter, last_general, axis=1)[:, 0] & (n_general > 0)
    i32 = lambda a: a.astype(jnp.int32)
    return dict(visible=visible, n_general=i32(n_general), n_visible=i32(n_visible),
                order=i32(order).reshape(-1),
                quarter_last_general=i32(quarter_last_general))


def _pipelined_tiles(n, tile_at, produce, consume, first_produced=False, body_steps=(4, 2)):
    if not first_produced:
        produce(tile_at(0), 0)

    def two_steps(t):
        produce(tile_at(t + 1), 1)
        consume(tile_at(t), 0)
        produce(tile_at(t + 2), 0)
        consume(tile_at(t + 1), 1)

    done = 0
    for steps in body_steps:
        def body(u, c, steps=steps, done=done):
            for j in range(0, steps, 2):
                two_steps(done + steps * u + j)
            return c

        trips = (n - 1 - done) // steps
        lax.fori_loop(0, trips, body, 0)
        done = done + steps * trips
    last = n - 1

    @pl.when(last % 2 == 1)
    def _():
        produce(tile_at(last), 1)
        consume(tile_at(last - 1), 0)
        consume(tile_at(last), 1)

    @pl.when(last % 2 == 0)
    def _():
        consume(tile_at(last), 0)


def _lane_group_sum(p):
    out = p[:, :LANES]
    for j in range(1, p.shape[1] // LANES):
        out = out + p[:, j * LANES:(j + 1) * LANES]
    return out


def _a_attn_kernel(cnt_ref, list_ref, q_ref, k_ref, v_ref, pq_ref, pk_ref, o_ref,
                   p_sc, acc_sc, m_sc, alpha_sc, *, tq, tk, nk, group, online):
    step = pl.program_id(1)
    acc_sc[...] = jnp.zeros_like(acc_sc)
    if online:
        m_sc[...] = jnp.full_like(m_sc, NEG_BIG)

    def item_at(t):
        code = list_ref[step * (group * nk) + t]
        return code // nk, code % nk

    def produce(item, slot):
        sub, ki = item
        rows = pl.ds(pl.multiple_of(sub * tq, tq), tq)
        k = k_ref[pl.ds(pl.multiple_of(ki * tk, tk), tk), :]
        s = lax.dot_general(q_ref[rows, :], k, _NT, preferred_element_type=F32)
        qchunk = jnp.right_shift(pq_ref[rows, :], CHUNK_SHIFT)
        kchunk = jnp.right_shift(pk_ref[ki], CHUNK_SHIFT)
        s = jnp.where(kchunk <= qchunk, s, NEG_BIG)
        if online:
            m_prev = m_sc[sub]
            m_new = jnp.maximum(m_prev, jnp.max(s, axis=1, keepdims=True))
            alpha_sc[slot] = jnp.exp2(m_prev - m_new)
            m_sc[sub] = m_new
            s = s - m_new
        p_sc[slot] = jnp.exp2(s).astype(BF16)

    def consume(item, slot):
        sub, ki = item
        v = v_ref[pl.ds(pl.multiple_of(ki * tk, tk), tk), :]
        pv = jnp.dot(p_sc[slot], v, preferred_element_type=F32)
        if online:
            acc_sc[sub] = alpha_sc[slot] * acc_sc[sub] + pv
        else:
            acc_sc[sub] += pv

    _pipelined_tiles(cnt_ref[step], item_at, produce, consume, body_steps=(8, 4, 2))
    for sub in range(group):
        acc = acc_sc[sub]
        o_ref[sub * tq:(sub + 1) * tq, :] = (
            acc[:, :A_V] / acc[:, A_ONE_COL:A_ONE_COL + 1]).astype(BF16)


def _a_attn(q, k, v, pos, pos_col, shift_tables, online):
    s = q.shape[0]
    tq, tk = ATTN_TILES_ONLINE if online else ATTN_TILES
    nq, nk = s // tq, s // tk
    group = math.gcd(A_QUERY_GROUP, nq)
    t = _tile_tables(pos, tq, tk) if online else shift_tables
    flat = t["visible"].reshape(nq // group, group * nk)
    items = jnp.argsort(jnp.logical_not(flat), axis=1, stable=True).astype(jnp.int32).reshape(-1)
    counts = flat.sum(axis=1).astype(jnp.int32)
    pos_k3 = pos.reshape(nk, 1, tk)
    tg = group * tq
    grid_spec = pltpu.PrefetchScalarGridSpec(
        num_scalar_prefetch=2,
        grid=(A_HEADS, nq // group),
        in_specs=[pl.BlockSpec((tg, A_HEAD_PAD), lambda h, i, *_: (i, h)),
                  pl.BlockSpec((s, A_HEAD_PAD), lambda h, i, *_: (0, h)),
                  pl.BlockSpec((s, A_V_PAD), lambda h, i, *_: (0, h)),
                  pl.BlockSpec((tg, 1), lambda h, i, *_: (i, 0)),
                  pl.BlockSpec((nk, 1, tk), lambda h, i, *_: (0, 0, 0))],
        out_specs=pl.BlockSpec((tg, A_V), lambda h, i, *_: (i, h)),
        scratch_shapes=[pltpu.VMEM((2, tq, tk), BF16), pltpu.VMEM((group, tq, A_V_PAD), F32),
                        pltpu.VMEM((group, tq, 1), F32), pltpu.VMEM((2, tq, 1), F32)],
    )
    return pl.pallas_call(
        functools.partial(_a_attn_kernel, tq=tq, tk=tk, nk=nk, group=group, online=online),
        grid_spec=grid_spec,
        out_shape=jax.ShapeDtypeStruct((s, A_WIDTH), BF16),
        compiler_params=pltpu.CompilerParams(dimension_semantics=("arbitrary", "arbitrary"),
                                             vmem_limit_bytes=VMEM_LIMIT_BYTES),
        name="a_attention_online" if online else "a_attention",
    )(counts, items, q, k, v, pos_col, pos_k3)


def _b_attn_kernel(gcnt_ref, cnt_ref, list_ref, quarter_ref, q_ref, k_ref, v_ref, pq_ref, pk_ref,
                   slope_ref, shift_ref, lq1_ref, lk1_ref, lq2_ref, lk2_ref, sub_ref, o_ref,
                   p_sc, acc_sc, l_sc, m_sc, alpha_sc, *, tk, nk, lam_init, online):
    qi = pl.program_id(1)
    q = q_ref[...]
    tq = q.shape[0]
    hq, hk = tq // 2, tk // 2
    q_ext = q[:, 2 * B_HD:]
    q_full = [jnp.concatenate([q[:, c * B_HD:(c + 1) * B_HD], q_ext], axis=1) for c in range(2)]
    pq = pq_ref[...]
    qchunk = jnp.right_shift(pq, CHUNK_SHIFT)
    neg_slope = slope_ref[0, :, 0:1]
    shift = shift_ref[0:1, 0:1]
    l_sc[...] = jnp.zeros_like(l_sc)
    acc_sc[...] = jnp.zeros_like(acc_sc)
    if online:
        m_sc[...] = jnp.full_like(m_sc, NEG_BIG)

    def finish(s, c, slot):
        if online:
            m_prev = m_sc[c]
            m_new = jnp.maximum(m_prev, jnp.max(s, axis=1, keepdims=True))
            alpha = jnp.exp2(m_prev - m_new)
            alpha_sc[slot, c] = alpha
            m_sc[c] = m_new
            p = jnp.exp2(s - m_new)
            l_sc[c] = alpha * l_sc[c] + _lane_group_sum(p)
        else:
            p = jnp.exp2(s)
            l_sc[c] += _lane_group_sum(p)
        p_sc[slot, c] = p.astype(BF16)

    def produce_linear(ki, slot):
        k = k_ref[pl.ds(pl.multiple_of(ki * tk, tk), tk), :]
        k_ext = k[:, 2 * B_HD:]
        for c in range(2):
            k_full = jnp.concatenate([k[:, c * B_HD:(c + 1) * B_HD], k_ext], axis=1)
            finish(lax.dot_general(q_full[c], k_full, _NT, preferred_element_type=F32), c, slot)

    def produce_general(ki, slot):
        k = k_ref[pl.ds(pl.multiple_of(ki * tk, tk), tk), :]
        pk = pk_ref[ki]
        bias = jnp.abs(pq - pk).astype(F32) * neg_slope - shift
        bias = jnp.where(jnp.right_shift(pk, CHUNK_SHIFT) <= qchunk, bias, NEG_BIG)
        for c in range(2):
            s = lax.dot_general(q[:, c * B_HD:(c + 1) * B_HD], k[:, c * B_HD:(c + 1) * B_HD],
                                _NT, preferred_element_type=F32) + bias
            finish(s, c, slot)

    def produce_general_skip_quarter(ki, slot):
        start = pl.multiple_of(ki * tk, tk)
        pk = pk_ref[ki]
        regions = ((slice(0, tq), slice(0, hk)), (slice(hq, tq), slice(hk, tk)))
        for rows, cols in regions:
            k = k_ref[pl.ds(start + cols.start, hk), :]
            pkc = pk[:, cols]
            bias = jnp.abs(pq[rows] - pkc).astype(F32) * neg_slope - shift
            bias = jnp.where(jnp.right_shift(pkc, CHUNK_SHIFT) <= qchunk[rows], bias, NEG_BIG)
            for c in range(2):
                s = lax.dot_general(q[rows, c * B_HD:(c + 1) * B_HD], k[:, c * B_HD:(c + 1) * B_HD],
                                    _NT, preferred_element_type=F32) + bias
                p = jnp.exp2(s)
                l_sc[c, rows, :] += _lane_group_sum(p)
                p_sc[slot, c, rows, cols] = p.astype(BF16)
        for c in range(2):
            p_sc[slot, c, :hq, hk:] = jnp.zeros((hq, hk), BF16)

    def consume(ki, slot):
        v = v_ref[pl.ds(pl.multiple_of(ki * tk, tk), tk), :]
        for c in range(2):
            pv = jnp.dot(p_sc[slot, c], v, preferred_element_type=F32)
            if online:
                acc_sc[c] = alpha_sc[slot, c] * acc_sc[c] + pv
            else:
                acc_sc[c] += pv

    n_general = gcnt_ref[qi]
    tile_at = lambda t: list_ref[qi * nk + t]

    def general_body(t, c):
        produce_general(tile_at(t), 0)
        consume(tile_at(t), 0)
        return c

    first = jnp.maximum(n_general - 1, 0)
    lax.fori_loop(0, first, general_body, 0)

    skip_quarter = False if online else quarter_ref[qi] != 0

    @pl.when((n_general > 0) & jnp.logical_not(skip_quarter))
    def _():
        produce_general(tile_at(first), 0)

    if not online:
        @pl.when((n_general > 0) & skip_quarter)
        def _():
            produce_general_skip_quarter(tile_at(first), 0)

    @pl.when(n_general == 0)
    def _():
        produce_linear(tile_at(first), 0)

    _pipelined_tiles(cnt_ref[qi] - first, lambda t: tile_at(first + t), produce_linear, consume,
                     first_produced=True, body_steps=(4, 2) if online else (6, 4, 2))

    lam = (jnp.exp(jnp.sum(lq1_ref[...] * lk1_ref[...], axis=-1, keepdims=True))
           - jnp.exp(jnp.sum(lq2_ref[...] * lk2_ref[...], axis=-1, keepdims=True)) + lam_init)
    l0 = jnp.sum(l_sc[0], axis=-1, keepdims=True)
    l1 = jnp.sum(l_sc[1], axis=-1, keepdims=True)
    od = acc_sc[0] / l0 - lam * (acc_sc[1] / l1)
    o_ref[...] = (od * _rms(od, B_V) * (sub_ref[...] * (1.0 - lam_init))).astype(BF16)


def _b_attn(q, k, v, pos, pos_col, shift_tables, consts, lam_init, online):
    s = q.shape[0]
    tq, tk = ATTN_TILES_ONLINE if online else ATTN_TILES
    nq, nk = s // tq, s // tk
    t = _tile_tables(pos, tq, tk) if online else shift_tables
    tables = (t["n_general"], t["n_visible"], t["order"], t["quarter_last_general"])
    pos_k3 = pos.reshape(nk, 1, tk)
    slopes = consts[0]
    grid_spec = pltpu.PrefetchScalarGridSpec(
        num_scalar_prefetch=4,
        grid=(B_HEADS, nq),
        in_specs=[pl.BlockSpec((tq, B_HEAD_PAD), lambda h, i, *_: (i, h)),
                  pl.BlockSpec((s, B_HEAD_PAD), lambda h, i, *_: (0, h),
                               pipeline_mode=pl.Buffered(1)),
                  pl.BlockSpec((s, B_V), lambda h, i, *_: (0, h)),
                  pl.BlockSpec((tq, 1), lambda h, i, *_: (i, 0)),
                  pl.BlockSpec((nk, 1, tk), lambda h, i, *_: (0, 0, 0)),
                  pl.BlockSpec((1,) + slopes.shape[1:], lambda h, i, *_: (h, 0, 0))]
                 + [pl.BlockSpec(c.shape, lambda h, i, *_: (0, 0)) for c in consts[1:]],
        out_specs=pl.BlockSpec((tq, B_V), lambda h, i, *_: (i, h)),
        scratch_shapes=[pltpu.VMEM((2, 2, tq, tk), BF16), pltpu.VMEM((2, tq, B_V), F32),
                        pltpu.VMEM((2, tq, LANES), F32), pltpu.VMEM((2, tq, 1), F32),
                        pltpu.VMEM((2, 2, tq, 1), F32)],
    )
    return pl.pallas_call(
        functools.partial(_b_attn_kernel, tk=tk, nk=nk, lam_init=lam_init, online=online),
        grid_spec=grid_spec,
        out_shape=jax.ShapeDtypeStruct((s, B_WIDTH), BF16),
        compiler_params=pltpu.CompilerParams(dimension_semantics=("arbitrary", "arbitrary"),
                                             vmem_limit_bytes=VMEM_LIMIT_BYTES),
        name="b_attention_online" if online else "b_attention",
    )(*tables, q, k, v, pos_col, pos_k3, *consts)


def _attend(fn, shift_bound, *args, **kwargs):
    return lax.cond(shift_bound <= MAX_CONST_SHIFT,
                    lambda: fn(*args, online=False, **kwargs),
                    lambda: fn(*args, online=True, **kwargs))


def _lane_onehot(value, lane):
    return jnp.zeros((1, LANES), F32).at[0, lane].set(value)


def _a_consts(norm_g, w_in, q_norm_g, w_q_up, kv_norm_g, w_kv_up, q_gain, k_gain):
    n_lat = A_Q_LORA + A_KV_LORA + A_ROPE
    wa = jnp.pad(w_in[:, :n_lat], ((0, 0), (0, LANES - A_ROPE))).astype(BF16)
    wg = w_in[:, n_lat:].astype(BF16)
    wq = w_q_up.reshape(A_Q_LORA, A_HEADS, A_QK)
    wq = jnp.pad(wq, ((0, 0), (0, 0), (0, A_HEAD_PAD - A_QK)))
    wq = wq.reshape(A_Q_LORA, A_HEADS * A_HEAD_PAD).astype(BF16)
    wkv = w_kv_up.reshape(A_KV_LORA, A_HEADS, A_NOPE + A_V)
    wkv = jnp.concatenate([wkv[:, :, :A_NOPE].reshape(A_KV_LORA, -1),
                           wkv[:, :, A_NOPE:].reshape(A_KV_LORA, -1)], axis=1).astype(BF16)
    q_scale = LOG2E / math.sqrt(A_QK)
    gq = jnp.pad(q_gain * q_scale, (0, A_HEAD_PAD - A_QK)).reshape(1, A_HEAD_PAD)
    gkn = k_gain[:A_NOPE].reshape(1, LANES)
    gkp = jnp.pad(k_gain[A_NOPE:], (0, LANES - A_ROPE)).reshape(1, LANES)
    half = A_ROPE // 2
    inv_freq = ROPE_THETA ** (-jnp.arange(0, A_ROPE, 2, dtype=F32) / A_ROPE)
    zeros = jnp.zeros((LANES - A_ROPE,), F32)
    invf = jnp.concatenate([inv_freq, inv_freq, zeros]).reshape(1, LANES)
    sgn = jnp.concatenate([-jnp.ones((half,), F32), jnp.ones((half,), F32), zeros]).reshape(1, LANES)
    bound = (BOUND_MARGIN * A_QK * q_scale
             * jnp.max(jnp.abs(q_gain)) * jnp.max(jnp.abs(k_gain)))
    bound = bound.astype(BF16).astype(F32)
    qshift = _lane_onehot(-bound, A_ROPE)
    one = _lane_onehot(1.0, A_ROPE)
    consts = (norm_g.reshape(1, -1), wa, wg, q_norm_g.reshape(1, -1), wq, kv_norm_g.reshape(1, -1),
              wkv, gq, gkn, gkp, invf, sgn, qshift, one)
    return consts, bound


def _b_consts(q_gain, k_gain):
    b_scale = LOG2E / math.sqrt(B_HD)
    bound = (BOUND_MARGIN * B_HD * b_scale * jnp.max(jnp.abs(q_gain)) * jnp.max(jnp.abs(k_gain)))
    bound = bound.astype(BF16).astype(F32)
    slopes = 2.0 ** (-8.0 * jnp.arange(1, B_HEADS + 1, dtype=F32) / B_HEADS)
    sigma = LOG2E * slopes
    sig_hi = sigma.astype(BF16).astype(F32)
    sig_lo = (sigma - sig_hi).astype(BF16).astype(F32)
    big = float(1 << POS_SPLIT_SHIFT)
    sig_cols = jnp.stack([sig_hi * big, sig_hi, sig_lo * big, sig_lo], axis=1)
    zeros = lambda n: jnp.zeros((B_HEADS, n), F32)
    extc_q = jnp.concatenate([jnp.broadcast_to(-bound, (B_HEADS, 1)), sig_cols, zeros(LANES - 5)],
                             axis=1)
    extc_k = jnp.concatenate([jnp.ones((B_HEADS, 1), F32), zeros(4), sig_cols, zeros(LANES - 9)],
                             axis=1)
    lanes = jnp.arange(LANES)
    pick = lambda ids, val: jnp.where(jnp.isin(lanes, jnp.array(ids)), val, 0.0).reshape(1, LANES)
    ea_q, eb_q = pick([5, 7], -1.0), pick([6, 8], -1.0)
    ea_k, eb_k = pick([1, 3], 1.0), pick([2, 4], 1.0)
    neg_slopes = jnp.broadcast_to(-(sig_hi + sig_lo)[:, None, None], (B_HEADS, 1, LANES))
    return bound, b_scale, neg_slopes, (extc_q, ea_q, eb_q), (extc_k, ea_k, eb_k)


def kernel(x, positions, a_norm, a_w_in, a_q_norm, a_w_q_up, a_kv_norm, a_w_kv_up, a_q_gain, a_k_gain, a_w_out, b_norm, b_w_in, b_q_gain, b_k_gain, b_lambda_q1, b_lambda_k1, b_lambda_q2, b_lambda_k2, b_subln, b_w_out):
    batch, seq, _ = x.shape
    assert batch == 1
    xs = x[0]
    pos = positions[0]
    pos_col = pos.reshape(seq, 1)
    row = lambda a: a.reshape(1, -1)

    consts, a_bound = _a_consts(a_norm[0], a_w_in[0], a_q_norm[0], a_w_q_up[0], a_kv_norm[0],
                                a_w_kv_up[0], a_q_gain[0], a_k_gain[0])
    q, k, v, sg = _a_prologue(xs, pos_col, consts)
    shift_tables = _tile_tables(pos, *ATTN_TILES)
    o = _attend(_a_attn, a_bound, q, k, v, pos, pos_col, shift_tables)
    xs = _out_proj(xs, o, sg, a_w_out[0].astype(BF16), name="a_out_proj")

    layer_idx = 1
    lam_init = 0.8 - 0.6 * math.exp(-0.3 * layer_idx)
    b_bound, b_scale, neg_slopes, q_ext, k_ext = _b_consts(b_q_gain[0], b_k_gain[0])
    w_in = b_w_in[0].astype(BF16)
    ng = row(b_norm[0])
    section = lambda j: (w_in, (D_MODEL, B_WIDTH), (0, j))
    rows = [(xs, D_MODEL, 0), (pos_col, 1, 0)]
    (q,) = _rows_call(_b_qk_kernel, rows, (ng, section(0), row(b_q_gain[0] * b_scale), *q_ext),
                      (B_HEADS * B_HEAD_PAD,), (BF16,), tm=1024, name="b_q_proj")
    (k,) = _rows_call(_b_qk_kernel, rows, (ng, section(1), row(b_k_gain[0]), *k_ext),
                      (B_HEADS * B_HEAD_PAD,), (BF16,), tm=1024, name="b_k_proj")
    v, sg = _rows_call(_b_vg_kernel, rows[:1], (ng, section(2), section(3)),
                       (B_WIDTH, B_WIDTH), (BF16, BF16), tm=512, name="b_vg_proj")
    b_consts = (neg_slopes, jnp.broadcast_to(b_bound.reshape(1, 1), (1, LANES)),
                row(b_lambda_q1[0]), row(b_lambda_k1[0]), row(b_lambda_q2[0]), row(b_lambda_k2[0]),
                row(b_subln[0]))
    o = _attend(_b_attn, b_bound, q, k, v, pos, pos_col, shift_tables, b_consts, lam_init)
    xs = _out_proj(xs, o, sg, b_w_out[0].astype(BF16), name="b_out_proj")
    return xs[None]
```

```python
import functools
import math

import jax
import jax.numpy as jnp
from jax import lax
from jax.experimental import pallas as pl
from jax.experimental.pallas import tpu as pltpu

D_MODEL = 2048
CHUNK_SHIFT = 6
EPS = 1e-6
LOG2E = 1.4426950408889634

A_HEADS = 16
A_NOPE = 128
A_ROPE = 64
A_QK = A_NOPE + A_ROPE
A_V = 128
A_Q_LORA = 512
A_KV_LORA = 512
A_WIDTH = A_HEADS * A_V
A_HEAD_PAD = 256
A_V_PAD = 256
A_ONE_COL = A_V + A_ROPE
ROPE_THETA = 10000.0

B_HEADS = 8
B_HD = 128
B_V = 2 * B_HD
B_WIDTH = B_HEADS * B_V
B_QK = B_HEADS * 2 * B_HD
B_HEAD_PAD = 3 * B_HD
POS_SPLIT_SHIFT = 7

LANES = 128
NEG_BIG = -1e30
MAX_CONST_SHIFT = 50.0
BOUND_MARGIN = 1.01
ATTN_TILES = (1024, 1024)
ATTN_TILES_ONLINE = (512, 1024)
A_QUERY_GROUP = 4
V7X_VMEM_BYTES = 64 * 1024 * 1024
VMEM_LIMIT_BYTES = V7X_VMEM_BYTES - 8 * 1024 * 1024

F32 = jnp.float32
BF16 = jnp.bfloat16
_NT = (((1,), (1,)), ((), ()))


def _const_spec(a, block_shape=None, block_index=None):
    shape = a.shape if block_shape is None else block_shape
    index = (0,) * len(shape) if block_index is None else block_index
    return pl.BlockSpec(shape, lambda *_: index, pipeline_mode=pl.Buffered(1))


def _rms(x, denom):
    return lax.rsqrt(jnp.sum(x * x, axis=-1, keepdims=True) * (1.0 / denom) + EPS)


def _a_prologue_kernel(x_ref, pos_ref, wa_ref, wg_ref, wq_ref, wkv_ref,
                       gq_ref, gkn_ref, gkp_ref, invf_ref, sgn_ref, qshift_ref, one_ref,
                       q_ref, k_ref, v_ref, sg_ref):
    x = x_ref[...]
    xn = (x * _rms(x, D_MODEL)).astype(BF16)
    za = jnp.dot(xn, wa_ref[...], preferred_element_type=F32)
    gate = jnp.dot(xn, wg_ref[...], preferred_element_type=F32)
    sg_ref[...] = (gate * jax.nn.sigmoid(gate)).astype(BF16)
    rope = _rope_fn(pos_ref, invf_ref, sgn_ref)
    _a_q_rows(za[:, :A_Q_LORA], rope, wq_ref, gq_ref, qshift_ref, q_ref)
    _a_kv_rows(za[:, A_Q_LORA:A_Q_LORA + A_KV_LORA], za[:, A_Q_LORA + A_KV_LORA:], rope,
               wkv_ref, gkn_ref, gkp_ref, one_ref, k_ref, v_ref)


def _rope_fn(pos_ref, invf_ref, sgn_ref):
    ang = pos_ref[...].astype(F32) * invf_ref[...]
    cos = jnp.cos(ang)
    sin_signed = jnp.sin(ang) * sgn_ref[...]
    lane = lax.broadcasted_iota(jnp.int32, cos.shape, 1)
    first_half = lane < (A_ROPE // 2)

    def rope(t):
        swapped = jnp.where(first_half, pltpu.roll(t, LANES - A_ROPE // 2, 1),
                            pltpu.roll(t, A_ROPE // 2, 1))
        return t * cos + swapped * sin_signed

    return rope


def _a_q_rows(cq, rope, wq_ref, gq_ref, qshift_ref, q_ref):
    cqn = (cq * _rms(cq, A_Q_LORA)).astype(BF16)
    q = jnp.dot(cqn, wq_ref[...], preferred_element_type=F32)
    gq = gq_ref[...]
    qshift = qshift_ref[...]
    for h in range(A_HEADS):
        qh = q[:, h * A_HEAD_PAD:(h + 1) * A_HEAD_PAD]
        qn = qh * _rms(qh, A_QK) * gq
        q_ref[:, h * A_HEAD_PAD:h * A_HEAD_PAD + LANES] = qn[:, :LANES].astype(BF16)
        q_ref[:, h * A_HEAD_PAD + LANES:(h + 1) * A_HEAD_PAD] = (
            rope(qn[:, LANES:]) + qshift).astype(BF16)


def _a_kv_rows(ckv, kpe, rope, wkv_ref, gkn_ref, gkp_ref, one_ref, k_ref, v_ref):
    ckvn = (ckv * _rms(ckv, A_KV_LORA)).astype(BF16)
    kv = jnp.dot(ckvn, wkv_ref[...], preferred_element_type=F32)
    gkn = gkn_ref[...]
    one = one_ref[...]
    v_one = jnp.broadcast_to(one, kpe.shape).astype(BF16)
    kpe_ss = jnp.sum(kpe * kpe, axis=-1, keepdims=True)
    kpe_roped = rope(kpe * gkp_ref[...])
    for h in range(A_HEADS):
        kn = kv[:, h * A_NOPE:(h + 1) * A_NOPE]
        r = lax.rsqrt((jnp.sum(kn * kn, axis=-1, keepdims=True) + kpe_ss) * (1.0 / A_QK) + EPS)
        k_ref[:, h * A_HEAD_PAD:h * A_HEAD_PAD + LANES] = (kn * r * gkn).astype(BF16)
        k_ref[:, h * A_HEAD_PAD + LANES:(h + 1) * A_HEAD_PAD] = (kpe_roped * r + one).astype(BF16)

        v_ref[:, h * A_V_PAD:h * A_V_PAD + A_V] = (
            kv[:, A_HEADS * A_NOPE + h * A_V:A_HEADS * A_NOPE + (h + 1) * A_V].astype(BF16))
        v_ref[:, h * A_V_PAD + A_V:(h + 1) * A_V_PAD] = v_one


def _rows_call(kernel_fn, row_inputs, consts, out_widths, out_dtypes, tm, name):
    s = row_inputs[0][0].shape[0]
    in_specs = [pl.BlockSpec((tm, w), lambda i, cb=cb: (i, cb)) for _, w, cb in row_inputs]
    in_specs += [_const_spec(*c) if isinstance(c, tuple) else _const_spec(c) for c in consts]
    out_specs = [pl.BlockSpec((tm, w), lambda i: (i, 0)) for w in out_widths]
    out_shape = [jax.ShapeDtypeStruct((s, w), dt) for w, dt in zip(out_widths, out_dtypes)]
    return pl.pallas_call(
        kernel_fn,
        grid=(s // tm,),
        in_specs=in_specs,
        out_specs=out_specs,
        out_shape=out_shape,
        compiler_params=pltpu.CompilerParams(dimension_semantics=("arbitrary",),
                                             vmem_limit_bytes=VMEM_LIMIT_BYTES),
        name=name,
    )(*[a for a, _, _ in row_inputs], *[c[0] if isinstance(c, tuple) else c for c in consts])


def _a_prologue(x, pos_col, consts):
    wide = A_HEADS * A_HEAD_PAD
    return _rows_call(_a_prologue_kernel, [(x, D_MODEL, 0), (pos_col, 1, 0)], consts,
                      (wide, wide, A_HEADS * A_V_PAD, A_WIDTH), (BF16, BF16, BF16, BF16),
                      tm=256, name="a_prologue")


def _b_qk_kernel(x_ref, pos_ref, w_ref, g_ref, extc_ref, ea_ref, eb_ref, out_ref):
    x = x_ref[...]
    xn = (x * _rms(x, D_MODEL)).astype(BF16)
    z = jnp.dot(xn, w_ref[...], preferred_element_type=F32)
    pos = pos_ref[...]
    a = jnp.right_shift(pos, POS_SPLIT_SHIFT).astype(F32)
    b = jnp.bitwise_and(pos, (1 << POS_SPLIT_SHIFT) - 1).astype(F32)
    ext_ab = a * ea_ref[...] + b * eb_ref[...]
    g = g_ref[...]
    for h in range(B_HEADS):
        for c in range(2):
            t = z[:, (2 * h + c) * B_HD:(2 * h + c + 1) * B_HD]
            out_ref[:, h * B_HEAD_PAD + c * B_HD:h * B_HEAD_PAD + (c + 1) * B_HD] = (
                t * _rms(t, B_HD) * g).astype(BF16)
        out_ref[:, h * B_HEAD_PAD + 2 * B_HD:(h + 1) * B_HEAD_PAD] = (
            extc_ref[h:h + 1, :] + ext_ab).astype(BF16)


def _b_vg_kernel(x_ref, wv_ref, wg_ref, v_ref, sg_ref):
    x = x_ref[...]
    xn = (x * _rms(x, D_MODEL)).astype(BF16)
    v_ref[...] = jnp.dot(xn, wv_ref[...], preferred_element_type=F32).astype(BF16)
    gate = jnp.dot(xn, wg_ref[...], preferred_element_type=F32)
    sg_ref[...] = (gate * jax.nn.sigmoid(gate)).astype(BF16)


def _out_proj_kernel(x_ref, o_ref, sg_ref, w_ref, y_ref):
    a = o_ref[...] * sg_ref[...]
    y_ref[...] = x_ref[...] + jnp.dot(a, w_ref[...], preferred_element_type=F32)


def _out_proj(x, o, sg, w, name):
    (y,) = _rows_call(_out_proj_kernel, [(x, D_MODEL, 0), (o, D_MODEL, 0), (sg, D_MODEL, 0)], (w,),
                      (D_MODEL,), (F32,), tm=512, name=name)
    return y


def _tile_tables(pos, tq, tk):
    s = pos.shape[0]
    nq, nk = s // tq, s // tk
    ch = jnp.right_shift(pos, CHUNK_SHIFT)
    qmax_ch = ch.reshape(nq, tq).max(axis=1)
    kmin_ch = ch.reshape(nk, tk).min(axis=1)
    qmin = pos.reshape(nq, tq).min(axis=1)
    kmax = pos.reshape(nk, tk).max(axis=1)
    in_range = (pos.min() >= 0) & (pos.max() < (1 << (POS_SPLIT_SHIFT + 8)))
    visible = kmin_ch[None, :] <= qmax_ch[:, None]
    linear = visible & (kmax[None, :] <= qmin[:, None]) & in_range
    general = visible & jnp.logical_not(linear)
    rank = jnp.where(general, 0, jnp.where(linear, 1, 2))
    order = jnp.argsort(rank, axis=1, stable=True)
    n_general = general.sum(axis=1)
    n_visible = visible.sum(axis=1)
    qmax_top = ch.reshape(nq, 2, tq // 2)[:, 0].max(axis=1)
    kmin_right = ch.reshape(nk, 2, tk // 2)[:, 1].min(axis=1)
    quarter = kmin_right[None, :] > qmax_top[:, None]
    last_general = jnp.take_along_axis(order, jnp.maximum(n_general - 1, 0)[:, None], axis=1)
    quarter_last_general = jnp.take_along_axis(quarter, last_general, axis=1)[:, 0] & (n_general > 0)
    i32 = lambda a: a.astype(jnp.int32)
    return dict(visible=visible, n_general=i32(n_general), n_visible=i32(n_visible),
                order=i32(order).reshape(-1),
                quarter_last_general=i32(quarter_last_general))


def _pipelined_tiles(n, tile_at, produce, consume, first_produced=False, body_steps=(4, 2)):
    if not first_produced:
        produce(tile_at(0), 0)

    def two_steps(t):
        produce(tile_at(t + 1), 1)
        consume(tile_at(t), 0)
        produce(tile_at(t + 2), 0)
        consume(tile_at(t + 1), 1)

    done = 0
    for steps in body_steps:
        def body(u, c, steps=steps, done=done):
            for j in range(0, steps, 2):
                two_steps(done + steps * u + j)
            return c

        trips = (n - 1 - done) // steps
        lax.fori_loop(0, trips, body, 0)
        done = done + steps * trips
    last = n - 1

    @pl.when(last % 2 == 1)
    def _():
        produce(tile_at(last), 1)
        consume(tile_at(last - 1), 0)
        consume(tile_at(last), 1)

    @pl.when(last % 2 == 0)
    def _():
        consume(tile_at(last), 0)


def _lane_group_sum(p):
    out = p[:, :LANES]
    for j in range(1, p.shape[1] // LANES):
        out = out + p[:, j * LANES:(j + 1) * LANES]
    return out


def _a_attn_kernel(cnt_ref, list_ref, q_ref, k_ref, v_ref, pq_ref, pk_ref, o_ref,
                   p_sc, acc_sc, m_sc, alpha_sc, *, tq, tk, nk, group, online):
    step = pl.program_id(1)
    acc_sc[...] = jnp.zeros_like(acc_sc)
    if online:
        m_sc[...] = jnp.full_like(m_sc, NEG_BIG)

    def item_at(t):
        code = list_ref[step * (group * nk) + t]
        return code // nk, code % nk

    def produce(item, slot):
        sub, ki = item
        rows = pl.ds(pl.multiple_of(sub * tq, tq), tq)
        k = k_ref[pl.ds(pl.multiple_of(ki * tk, tk), tk), :]
        s = lax.dot_general(q_ref[rows, :], k, _NT, preferred_element_type=F32)
        qchunk = jnp.right_shift(pq_ref[rows, :], CHUNK_SHIFT)
        kchunk = jnp.right_shift(pk_ref[ki], CHUNK_SHIFT)
        s = jnp.where(kchunk <= qchunk, s, NEG_BIG)
        if online:
            m_prev = m_sc[sub]
            m_new = jnp.maximum(m_prev, jnp.max(s, axis=1, keepdims=True))
            alpha_sc[slot] = jnp.exp2(m_prev - m_new)
            m_sc[sub] = m_new
            s = s - m_new
        p_sc[slot] = jnp.exp2(s).astype(BF16)

    def consume(item, slot):
        sub, ki = item
        v = v_ref[pl.ds(pl.multiple_of(ki * tk, tk), tk), :]
        pv = jnp.dot(p_sc[slot], v, preferred_element_type=F32)
        if online:
            acc_sc[sub] = alpha_sc[slot] * acc_sc[sub] + pv
        else:
            acc_sc[sub] += pv

    _pipelined_tiles(cnt_ref[step], item_at, produce, consume, body_steps=(8, 4, 2))
    for sub in range(group):
        acc = acc_sc[sub]
        o_ref[sub * tq:(sub + 1) * tq, :] = (
            acc[:, :A_V] / acc[:, A_ONE_COL:A_ONE_COL + 1]).astype(BF16)


def _a_attn(q, k, v, pos, pos_col, shift_tables, online):
    s = q.shape[0]
    tq, tk = ATTN_TILES_ONLINE if online else ATTN_TILES
    nq, nk = s // tq, s // tk
    group = math.gcd(A_QUERY_GROUP, nq)
    t = _tile_tables(pos, tq, tk) if online else shift_tables
    flat = t["visible"].reshape(nq // group, group * nk)
    items = jnp.argsort(jnp.logical_not(flat), axis=1, stable=True).astype(jnp.int32).reshape(-1)
    counts = flat.sum(axis=1).astype(jnp.int32)
    pos_k3 = pos.reshape(nk, 1, tk)
    tg = group * tq
    grid_spec = pltpu.PrefetchScalarGridSpec(
        num_scalar_prefetch=2,
        grid=(A_HEADS, nq // group),
        in_specs=[pl.BlockSpec((tg, A_HEAD_PAD), lambda h, i, *_: (i, h)),
                  pl.BlockSpec((s, A_HEAD_PAD), lambda h, i, *_: (0, h)),
                  pl.BlockSpec((s, A_V_PAD), lambda h, i, *_: (0, h)),
                  pl.BlockSpec((tg, 1), lambda h, i, *_: (i, 0)),
                  pl.BlockSpec((nk, 1, tk), lambda h, i, *_: (0, 0, 0))],
        out_specs=pl.BlockSpec((tg, A_V), lambda h, i, *_: (i, h)),
        scratch_shapes=[pltpu.VMEM((2, tq, tk), BF16), pltpu.VMEM((group, tq, A_V_PAD), F32),
                        pltpu.VMEM((group, tq, 1), F32), pltpu.VMEM((2, tq, 1), F32)],
    )
    return pl.pallas_call(
        functools.partial(_a_attn_kernel, tq=tq, tk=tk, nk=nk, group=group, online=online),
        grid_spec=grid_spec,
        out_shape=jax.ShapeDtypeStruct((s, A_WIDTH), BF16),
        compiler_params=pltpu.CompilerParams(dimension_semantics=("arbitrary", "arbitrary"),
                                             vmem_limit_bytes=VMEM_LIMIT_BYTES),
        name="a_attention_online" if online else "a_attention",
    )(counts, items, q, k, v, pos_col, pos_k3)


def _b_attn_kernel(gcnt_ref, cnt_ref, list_ref, quarter_ref, q_ref, k_ref, v_ref, pq_ref, pk_ref,
                   slope_ref, shift_ref, lq1_ref, lk1_ref, lq2_ref, lk2_ref, sub_ref, o_ref,
                   p_sc, acc_sc, l_sc, m_sc, alpha_sc, *, tk, nk, lam_init, online):
    qi = pl.program_id(1)
    q = q_ref[...]
    tq = q.shape[0]
    hq, hk = tq // 2, tk // 2
    q_ext = q[:, 2 * B_HD:]
    q_full = [jnp.concatenate([q[:, c * B_HD:(c + 1) * B_HD], q_ext], axis=1) for c in range(2)]
    pq = pq_ref[...]
    qchunk = jnp.right_shift(pq, CHUNK_SHIFT)
    neg_slope = slope_ref[0, :, 0:1]
    shift = shift_ref[0:1, 0:1]
    l_sc[...] = jnp.zeros_like(l_sc)
    acc_sc[...] = jnp.zeros_like(acc_sc)
    if online:
        m_sc[...] = jnp.full_like(m_sc, NEG_BIG)

    def finish(s, c, slot):
        if online:
            m_prev = m_sc[c]
            m_new = jnp.maximum(m_prev, jnp.max(s, axis=1, keepdims=True))
            alpha = jnp.exp2(m_prev - m_new)
            alpha_sc[slot, c] = alpha
            m_sc[c] = m_new
            p = jnp.exp2(s - m_new)
            l_sc[c] = alpha * l_sc[c] + _lane_group_sum(p)
        else:
            p = jnp.exp2(s)
            l_sc[c] += _lane_group_sum(p)
        p_sc[slot, c] = p.astype(BF16)

    def produce_linear(ki, slot):
        k = k_ref[pl.ds(pl.multiple_of(ki * tk, tk), tk), :]
        k_ext = k[:, 2 * B_HD:]
        for c in range(2):
            k_full = jnp.concatenate([k[:, c * B_HD:(c + 1) * B_HD], k_ext], axis=1)
            finish(lax.dot_general(q_full[c], k_full, _NT, preferred_element_type=F32), c, slot)

    def produce_general(ki, slot):
        k = k_ref[pl.ds(pl.multiple_of(ki * tk, tk), tk), :]
        pk = pk_ref[ki]
        bias = jnp.abs(pq - pk).astype(F32) * neg_slope - shift
        bias = jnp.where(jnp.right_shift(pk, CHUNK_SHIFT) <= qchunk, bias, NEG_BIG)
        for c in range(2):
            s = lax.dot_general(q[:, c * B_HD:(c + 1) * B_HD], k[:, c * B_HD:(c + 1) * B_HD],
                                _NT, preferred_element_type=F32) + bias
            finish(s, c, slot)

    def produce_general_skip_quarter(ki, slot):
        start = pl.multiple_of(ki * tk, tk)
        pk = pk_ref[ki]
        regions = ((slice(0, tq), slice(0, hk)), (slice(hq, tq), slice(hk, tk)))
        for rows, cols in regions:
            k = k_ref[pl.ds(start + cols.start, hk), :]
            pkc = pk[:, cols]
            bias = jnp.abs(pq[rows] - pkc).astype(F32) * neg_slope - shift
            bias = jnp.where(jnp.right_shift(pkc, CHUNK_SHIFT) <= qchunk[rows], bias, NEG_BIG)
            for c in range(2):
                s = lax.dot_general(q[rows, c * B_HD:(c + 1) * B_HD], k[:, c * B_HD:(c + 1) * B_HD],
                                    _NT, preferred_element_type=F32) + bias
                p = jnp.exp2(s)
                l_sc[c, rows, :] += _lane_group_sum(p)
                p_sc[slot, c, rows, cols] = p.astype(BF16)
        for c in range(2):
            p_sc[slot, c, :hq, hk:] = jnp.zeros((hq, hk), BF16)

    def consume(ki, slot):
        v = v_ref[pl.ds(pl.multiple_of(ki * tk, tk), tk), :]
        for c in range(2):
            pv = jnp.dot(p_sc[slot, c], v, preferred_element_type=F32)
            if online:
                acc_sc[c] = alpha_sc[slot, c] * acc_sc[c] + pv
            else:
                acc_sc[c] += pv

    n_general = gcnt_ref[qi]
    tile_at = lambda t: list_ref[qi * nk + t]

    def general_body(t, c):
        produce_general(tile_at(t), 0)
        consume(tile_at(t), 0)
        return c

    first = jnp.maximum(n_general - 1, 0)
    lax.fori_loop(0, first, general_body, 0)

    skip_quarter = False if online else quarter_ref[qi] != 0

    @pl.when((n_general > 0) & jnp.logical_not(skip_quarter))
    def _():
        produce_general(tile_at(first), 0)

    if not online:
        @pl.when((n_general > 0) & skip_quarter)
        def _():
            produce_general_skip_quarter(tile_at(first), 0)

    @pl.when(n_general == 0)
    def _():
        produce_linear(tile_at(first), 0)

    _pipelined_tiles(cnt_ref[qi] - first, lambda t: tile_at(first + t), produce_linear, consume,
                     first_produced=True)

    lam = (jnp.exp(jnp.sum(lq1_ref[...] * lk1_ref[...], axis=-1, keepdims=True))
           - jnp.exp(jnp.sum(lq2_ref[...] * lk2_ref[...], axis=-1, keepdims=True)) + lam_init)
    l0 = jnp.sum(l_sc[0], axis=-1, keepdims=True)
    l1 = jnp.sum(l_sc[1], axis=-1, keepdims=True)
    od = acc_sc[0] / l0 - lam * (acc_sc[1] / l1)
    o_ref[...] = (od * _rms(od, B_V) * (sub_ref[...] * (1.0 - lam_init))).astype(BF16)


def _b_attn(q, k, v, pos, pos_col, shift_tables, consts, lam_init, online):
    s = q.shape[0]
    tq, tk = ATTN_TILES_ONLINE if online else ATTN_TILES
    nq, nk = s // tq, s // tk
    t = _tile_tables(pos, tq, tk) if online else shift_tables
    tables = (t["n_general"], t["n_visible"], t["order"], t["quarter_last_general"])
    pos_k3 = pos.reshape(nk, 1, tk)
    slopes = consts[0]
    grid_spec = pltpu.PrefetchScalarGridSpec(
        num_scalar_prefetch=4,
        grid=(B_HEADS, nq),
        in_specs=[pl.BlockSpec((tq, B_HEAD_PAD), lambda h, i, *_: (i, h)),
                  pl.BlockSpec((s, B_HEAD_PAD), lambda h, i, *_: (0, h),
                               pipeline_mode=pl.Buffered(1)),
                  pl.BlockSpec((s, B_V), lambda h, i, *_: (0, h)),
                  pl.BlockSpec((tq, 1), lambda h, i, *_: (i, 0)),
                  pl.BlockSpec((nk, 1, tk), lambda h, i, *_: (0, 0, 0)),
                  pl.BlockSpec((1,) + slopes.shape[1:], lambda h, i, *_: (h, 0, 0))]
                 + [pl.BlockSpec(c.shape, lambda h, i, *_: (0, 0)) for c in consts[1:]],
        out_specs=pl.BlockSpec((tq, B_V), lambda h, i, *_: (i, h)),
        scratch_shapes=[pltpu.VMEM((2, 2, tq, tk), BF16), pltpu.VMEM((2, tq, B_V), F32),
                        pltpu.VMEM((2, tq, LANES), F32), pltpu.VMEM((2, tq, 1), F32),
                        pltpu.VMEM((2, 2, tq, 1), F32)],
    )
    return pl.pallas_call(
        functools.partial(_b_attn_kernel, tk=tk, nk=nk, lam_init=lam_init, online=online),
        grid_spec=grid_spec,
        out_shape=jax.ShapeDtypeStruct((s, B_WIDTH), BF16),
        compiler_params=pltpu.CompilerParams(dimension_semantics=("arbitrary", "arbitrary"),
                                             vmem_limit_bytes=VMEM_LIMIT_BYTES),
        name="b_attention_online" if online else "b_attention",
    )(*tables, q, k, v, pos_col, pos_k3, *consts)


def _attend(fn, shift_bound, *args, **kwargs):
    return lax.cond(shift_bound <= MAX_CONST_SHIFT,
                    lambda: fn(*args, online=False, **kwargs),
                    lambda: fn(*args, online=True, **kwargs))


def _lane_onehot(value, lane):
    return jnp.zeros((1, LANES), F32).at[0, lane].set(value)


def _a_consts(norm_g, w_in, q_norm_g, w_q_up, kv_norm_g, w_kv_up, q_gain, k_gain):
    n_lat = A_Q_LORA + A_KV_LORA + A_ROPE
    w_in = norm_g[:, None] * w_in
    w_q_up = q_norm_g[:, None] * w_q_up
    w_kv_up = kv_norm_g[:, None] * w_kv_up
    wa = jnp.pad(w_in[:, :n_lat], ((0, 0), (0, LANES - A_ROPE))).astype(BF16)
    wg = w_in[:, n_lat:].astype(BF16)
    wq = w_q_up.reshape(A_Q_LORA, A_HEADS, A_QK)
    wq = jnp.pad(wq, ((0, 0), (0, 0), (0, A_HEAD_PAD - A_QK)))
    wq = wq.reshape(A_Q_LORA, A_HEADS * A_HEAD_PAD).astype(BF16)
    wkv = w_kv_up.reshape(A_KV_LORA, A_HEADS, A_NOPE + A_V)
    wkv = jnp.concatenate([wkv[:, :, :A_NOPE].reshape(A_KV_LORA, -1),
                           wkv[:, :, A_NOPE:].reshape(A_KV_LORA, -1)], axis=1).astype(BF16)
    q_scale = LOG2E / math.sqrt(A_QK)
    gq = jnp.pad(q_gain * q_scale, (0, A_HEAD_PAD - A_QK)).reshape(1, A_HEAD_PAD)
    gkn = k_gain[:A_NOPE].reshape(1, LANES)
    gkp = jnp.pad(k_gain[A_NOPE:], (0, LANES - A_ROPE)).reshape(1, LANES)
    half = A_ROPE // 2
    inv_freq = ROPE_THETA ** (-jnp.arange(0, A_ROPE, 2, dtype=F32) / A_ROPE)
    zeros = jnp.zeros((LANES - A_ROPE,), F32)
    invf = jnp.concatenate([inv_freq, inv_freq, zeros]).reshape(1, LANES)
    sgn = jnp.concatenate([-jnp.ones((half,), F32), jnp.ones((half,), F32), zeros]).reshape(1, LANES)
    bound = (BOUND_MARGIN * A_QK * q_scale
             * jnp.max(jnp.abs(q_gain)) * jnp.max(jnp.abs(k_gain)))
    bound = bound.astype(BF16).astype(F32)
    qshift = _lane_onehot(-bound, A_ROPE)
    one = _lane_onehot(1.0, A_ROPE)
    consts = (wa, wg, wq, wkv, gq, gkn, gkp, invf, sgn, qshift, one)
    return consts, bound


def _b_consts(q_gain, k_gain):
    b_scale = LOG2E / math.sqrt(B_HD)
    bound = (BOUND_MARGIN * B_HD * b_scale * jnp.max(jnp.abs(q_gain)) * jnp.max(jnp.abs(k_gain)))
    bound = bound.astype(BF16).astype(F32)
    slopes = 2.0 ** (-8.0 * jnp.arange(1, B_HEADS + 1, dtype=F32) / B_HEADS)
    sigma = LOG2E * slopes
    sig_hi = sigma.astype(BF16).astype(F32)
    sig_lo = (sigma - sig_hi).astype(BF16).astype(F32)
    big = float(1 << POS_SPLIT_SHIFT)
    sig_cols = jnp.stack([sig_hi * big, sig_hi, sig_lo * big, sig_lo], axis=1)
    zeros = lambda n: jnp.zeros((B_HEADS, n), F32)
    extc_q = jnp.concatenate([jnp.broadcast_to(-bound, (B_HEADS, 1)), sig_cols, zeros(LANES - 5)],
                             axis=1)
    extc_k = jnp.concatenate([jnp.ones((B_HEADS, 1), F32), zeros(4), sig_cols, zeros(LANES - 9)],
                             axis=1)
    lanes = jnp.arange(LANES)
    pick = lambda ids, val: jnp.where(jnp.isin(lanes, jnp.array(ids)), val, 0.0).reshape(1, LANES)
    ea_q, eb_q = pick([5, 7], -1.0), pick([6, 8], -1.0)
    ea_k, eb_k = pick([1, 3], 1.0), pick([2, 4], 1.0)
    neg_slopes = jnp.broadcast_to(-(sig_hi + sig_lo)[:, None, None], (B_HEADS, 1, LANES))
    return bound, b_scale, neg_slopes, (extc_q, ea_q, eb_q), (extc_k, ea_k, eb_k)


def kernel(x, positions, a_norm, a_w_in, a_q_norm, a_w_q_up, a_kv_norm, a_w_kv_up, a_q_gain, a_k_gain, a_w_out, b_norm, b_w_in, b_q_gain, b_k_gain, b_lambda_q1, b_lambda_k1, b_lambda_q2, b_lambda_k2, b_subln, b_w_out):
    batch, seq, _ = x.shape
    assert batch == 1
    xs = x[0]
    pos = positions[0]
    pos_col = pos.reshape(seq, 1)
    row = lambda a: a.reshape(1, -1)

    consts, a_bound = _a_consts(a_norm[0], a_w_in[0], a_q_norm[0], a_w_q_up[0], a_kv_norm[0],
                                a_w_kv_up[0], a_q_gain[0], a_k_gain[0])
    q, k, v, sg = _a_prologue(xs, pos_col, consts)
    shift_tables = _tile_tables(pos, *ATTN_TILES)
    o = _attend(_a_attn, a_bound, q, k, v, pos, pos_col, shift_tables)
    xs = _out_proj(xs, o, sg, a_w_out[0].astype(BF16), name="a_out_proj")

    layer_idx = 1
    lam_init = 0.8 - 0.6 * math.exp(-0.3 * layer_idx)
    b_bound, b_scale, neg_slopes, q_ext, k_ext = _b_consts(b_q_gain[0], b_k_gain[0])
    w_in = (b_norm[0][:, None] * b_w_in[0]).astype(BF16)
    section = lambda j: (w_in, (D_MODEL, B_WIDTH), (0, j))
    rows = [(xs, D_MODEL, 0), (pos_col, 1, 0)]
    (q,) = _rows_call(_b_qk_kernel, rows, (section(0), row(b_q_gain[0] * b_scale), *q_ext),
                      (B_HEADS * B_HEAD_PAD,), (BF16,), tm=1024, name="b_q_proj")
    (k,) = _rows_call(_b_qk_kernel, rows, (section(1), row(b_k_gain[0]), *k_ext),
                      (B_HEADS * B_HEAD_PAD,), (BF16,), tm=1024, name="b_k_proj")
    v, sg = _rows_call(_b_vg_kernel, rows[:1], (section(2), section(3)),
                       (B_WIDTH, B_WIDTH), (BF16, BF16), tm=512, name="b_vg_proj")
    b_consts = (neg_slopes, jnp.broadcast_to(b_bound.reshape(1, 1), (1, LANES)),
                row(b_lambda_q1[0]), row(b_lambda_k1[0]), row(b_lambda_q2[0]), row(b_lambda_k2[0]),
                row(b_subln[0]))
    o = _attend(_b_attn, b_bound, q, k, v, pos, pos_col, shift_tables, b_consts, lam_init)
    xs = _out_proj(xs, o, sg, b_w_out[0].astype(BF16), name="b_out_proj")
    return xs[None]
```

```python
import functools
import math

import jax
import jax.numpy as jnp
from jax import lax
from jax.experimental import pallas as pl
from jax.experimental.pallas import tpu as pltpu

D_MODEL = 2048
CHUNK_SHIFT = 6
EPS = 1e-6
LOG2E = 1.4426950408889634

A_HEADS = 16
A_NOPE = 128
A_ROPE = 64
A_QK = A_NOPE + A_ROPE
A_V = 128
A_Q_LORA = 512
A_KV_LORA = 512
A_WIDTH = A_HEADS * A_V
A_HEAD_PAD = 256
A_V_PAD = 256
A_ONE_COL = A_V + A_ROPE
ROPE_THETA = 10000.0

B_HEADS = 8
B_HD = 128
B_V = 2 * B_HD
B_WIDTH = B_HEADS * B_V
B_QK = B_HEADS * 2 * B_HD
B_HEAD_PAD = 3 * B_HD
POS_SPLIT_SHIFT = 7

LANES = 128
NEG_BIG = -1e30
MAX_CONST_SHIFT = 50.0
BOUND_MARGIN = 1.01
ATTN_TILES = (1024, 1024)
ATTN_TILES_ONLINE = (512, 1024)
A_QUERY_GROUP = 4
V7X_VMEM_BYTES = 64 * 1024 * 1024
VMEM_LIMIT_BYTES = V7X_VMEM_BYTES - 8 * 1024 * 1024

F32 = jnp.float32
BF16 = jnp.bfloat16
_NT = (((1,), (1,)), ((), ()))


def _const_spec(a, block_shape=None, block_index=None):
    shape = a.shape if block_shape is None else block_shape
    index = (0,) * len(shape) if block_index is None else block_index
    return pl.BlockSpec(shape, lambda *_: index, pipeline_mode=pl.Buffered(1))


def _rms(x, denom):
    return lax.rsqrt(jnp.sum(x * x, axis=-1, keepdims=True) * (1.0 / denom) + EPS)


def _a_prologue_kernel(x_ref, pos_ref, ng_ref, wa_ref, wg_ref, qng_ref, wq_ref, kvng_ref, wkv_ref,
                       gq_ref, gkn_ref, gkp_ref, invf_ref, sgn_ref, qshift_ref, one_ref,
                       q_ref, k_ref, v_ref, sg_ref):
    x = x_ref[...]
    xn = (x * _rms(x, D_MODEL) * ng_ref[...]).astype(BF16)
    za = jnp.dot(xn, wa_ref[...], preferred_element_type=F32)
    gate = jnp.dot(xn, wg_ref[...], preferred_element_type=F32)
    sg_ref[...] = (gate * jax.nn.sigmoid(gate)).astype(BF16)
    rope = _rope_fn(pos_ref, invf_ref, sgn_ref)
    _a_q_rows(za[:, :A_Q_LORA], rope, qng_ref, wq_ref, gq_ref, qshift_ref, q_ref)
    _a_kv_rows(za[:, A_Q_LORA:A_Q_LORA + A_KV_LORA], za[:, A_Q_LORA + A_KV_LORA:], rope,
               kvng_ref, wkv_ref, gkn_ref, gkp_ref, one_ref, k_ref, v_ref)


def _rope_fn(pos_ref, invf_ref, sgn_ref):
    ang = pos_ref[...].astype(F32) * invf_ref[...]
    cos = jnp.cos(ang)
    sin_signed = jnp.sin(ang) * sgn_ref[...]
    lane = lax.broadcasted_iota(jnp.int32, cos.shape, 1)
    first_half = lane < (A_ROPE // 2)

    def rope(t):
        swapped = jnp.where(first_half, pltpu.roll(t, LANES - A_ROPE // 2, 1),
                            pltpu.roll(t, A_ROPE // 2, 1))
        return t * cos + swapped * sin_signed

    return rope


def _a_q_rows(cq, rope, qng_ref, wq_ref, gq_ref, qshift_ref, q_ref):
    cqn = (cq * _rms(cq, A_Q_LORA) * qng_ref[...]).astype(BF16)
    q = jnp.dot(cqn, wq_ref[...], preferred_element_type=F32)
    gq = gq_ref[...]
    qshift = qshift_ref[...]
    for h in range(A_HEADS):
        qh = q[:, h * A_HEAD_PAD:(h + 1) * A_HEAD_PAD]
        qn = qh * _rms(qh, A_QK) * gq
        q_ref[:, h * A_HEAD_PAD:h * A_HEAD_PAD + LANES] = qn[:, :LANES].astype(BF16)
        q_ref[:, h * A_HEAD_PAD + LANES:(h + 1) * A_HEAD_PAD] = (
            rope(qn[:, LANES:]) + qshift).astype(BF16)


def _a_kv_rows(ckv, kpe, rope, kvng_ref, wkv_ref, gkn_ref, gkp_ref, one_ref, k_ref, v_ref):
    ckvn = (ckv * _rms(ckv, A_KV_LORA) * kvng_ref[...]).astype(BF16)
    kv = jnp.dot(ckvn, wkv_ref[...], preferred_element_type=F32)
    gkn = gkn_ref[...]
    one = one_ref[...]
    v_one = jnp.broadcast_to(one, kpe.shape).astype(BF16)
    kpe_ss = jnp.sum(kpe * kpe, axis=-1, keepdims=True)
    kpe_roped = rope(kpe * gkp_ref[...])
    for h in range(A_HEADS):
        kn = kv[:, h * A_NOPE:(h + 1) * A_NOPE]
        r = lax.rsqrt((jnp.sum(kn * kn, axis=-1, keepdims=True) + kpe_ss) * (1.0 / A_QK) + EPS)
        k_ref[:, h * A_HEAD_PAD:h * A_HEAD_PAD + LANES] = (kn * r * gkn).astype(BF16)
        k_ref[:, h * A_HEAD_PAD + LANES:(h + 1) * A_HEAD_PAD] = (kpe_roped * r + one).astype(BF16)

        v_ref[:, h * A_V_PAD:h * A_V_PAD + A_V] = (
            kv[:, A_HEADS * A_NOPE + h * A_V:A_HEADS * A_NOPE + (h + 1) * A_V].astype(BF16))
        v_ref[:, h * A_V_PAD + A_V:(h + 1) * A_V_PAD] = v_one


def _rows_call(kernel_fn, row_inputs, consts, out_widths, out_dtypes, tm, name):
    s = row_inputs[0][0].shape[0]
    in_specs = [pl.BlockSpec((tm, w), lambda i, cb=cb: (i, cb)) for _, w, cb in row_inputs]
    in_specs += [_const_spec(*c) if isinstance(c, tuple) else _const_spec(c) for c in consts]
    out_specs = [pl.BlockSpec((tm, w), lambda i: (i, 0)) for w in out_widths]
    out_shape = [jax.ShapeDtypeStruct((s, w), dt) for w, dt in zip(out_widths, out_dtypes)]
    return pl.pallas_call(
        kernel_fn,
        grid=(s // tm,),
        in_specs=in_specs,
        out_specs=out_specs,
        out_shape=out_shape,
        compiler_params=pltpu.CompilerParams(dimension_semantics=("arbitrary",),
                                             vmem_limit_bytes=VMEM_LIMIT_BYTES),
        name=name,
    )(*[a for a, _, _ in row_inputs], *[c[0] if isinstance(c, tuple) else c for c in consts])


def _a_prologue(x, pos_col, consts):
    wide = A_HEADS * A_HEAD_PAD
    return _rows_call(_a_prologue_kernel, [(x, D_MODEL, 0), (pos_col, 1, 0)], consts,
                      (wide, wide, A_HEADS * A_V_PAD, A_WIDTH), (BF16, BF16, BF16, BF16),
                      tm=256, name="a_prologue")


def _b_qk_kernel(x_ref, pos_ref, ng_ref, w_ref, g_ref, extc_ref, ea_ref, eb_ref, out_ref):
    x = x_ref[...]
    xn = (x * _rms(x, D_MODEL) * ng_ref[...]).astype(BF16)
    z = jnp.dot(xn, w_ref[...], preferred_element_type=F32)
    pos = pos_ref[...]
    a = jnp.right_shift(pos, POS_SPLIT_SHIFT).astype(F32)
    b = jnp.bitwise_and(pos, (1 << POS_SPLIT_SHIFT) - 1).astype(F32)
    ext_ab = a * ea_ref[...] + b * eb_ref[...]
    g = g_ref[...]
    for h in range(B_HEADS):
        for c in range(2):
            t = z[:, (2 * h + c) * B_HD:(2 * h + c + 1) * B_HD]
            out_ref[:, h * B_HEAD_PAD + c * B_HD:h * B_HEAD_PAD + (c + 1) * B_HD] = (
                t * _rms(t, B_HD) * g).astype(BF16)
        out_ref[:, h * B_HEAD_PAD + 2 * B_HD:(h + 1) * B_HEAD_PAD] = (
            extc_ref[h:h + 1, :] + ext_ab).astype(BF16)


def _b_vg_kernel(x_ref, ng_ref, wv_ref, wg_ref, v_ref, sg_ref):
    x = x_ref[...]
    xn = (x * _rms(x, D_MODEL) * ng_ref[...]).astype(BF16)
    v_ref[...] = jnp.dot(xn, wv_ref[...], preferred_element_type=F32).astype(BF16)
    gate = jnp.dot(xn, wg_ref[...], preferred_element_type=F32)
    sg_ref[...] = (gate * jax.nn.sigmoid(gate)).astype(BF16)


def _out_proj_kernel(x_ref, o_ref, sg_ref, w_ref, y_ref):
    a = o_ref[...] * sg_ref[...]
    y_ref[...] = x_ref[...] + jnp.dot(a, w_ref[...], preferred_element_type=F32)


def _out_proj(x, o, sg, w, name):
    (y,) = _rows_call(_out_proj_kernel, [(x, D_MODEL, 0), (o, D_MODEL, 0), (sg, D_MODEL, 0)], (w,),
                      (D_MODEL,), (F32,), tm=512, name=name)
    return y


def _tile_tables(pos, tq, tk):
    s = pos.shape[0]
    nq, nk = s // tq, s // tk
    ch = jnp.right_shift(pos, CHUNK_SHIFT)
    qmax_ch = ch.reshape(nq, tq).max(axis=1)
    kmin_ch = ch.reshape(nk, tk).min(axis=1)
    qmin = pos.reshape(nq, tq).min(axis=1)
    kmax = pos.reshape(nk, tk).max(axis=1)
    in_range = (pos.min() >= 0) & (pos.max() < (1 << (POS_SPLIT_SHIFT + 8)))
    visible = kmin_ch[None, :] <= qmax_ch[:, None]
    linear = visible & (kmax[None, :] <= qmin[:, None]) & in_range
    general = visible & jnp.logical_not(linear)
    rank = jnp.where(general, 0, jnp.where(linear, 1, 2))
    order = jnp.argsort(rank, axis=1, stable=True)
    n_general = general.sum(axis=1)
    n_visible = visible.sum(axis=1)
    qmax_top = ch.reshape(nq, 2, tq // 2)[:, 0].max(axis=1)
    kmin_right = ch.reshape(nk, 2, tk // 2)[:, 1].min(axis=1)
    quarter = kmin_right[None, :] > qmax_top[:, None]
    last_general = jnp.take_along_axis(order, jnp.maximum(n_general - 1, 0)[:, None], axis=1)
    quarter_last_general = jnp.take_along_axis(quarter, last_general, axis=1)[:, 0] & (n_general > 0)
    i32 = lambda a: a.astype(jnp.int32)
    return dict(visible=visible, n_general=i32(n_general), n_visible=i32(n_visible),
                order=i32(order).reshape(-1),
                quarter_last_general=i32(quarter_last_general))


def _pipelined_tiles(n, tile_at, produce, consume, first_produced=False, body_steps=(4, 2)):
    if not first_produced:
        produce(tile_at(0), 0)

    def two_steps(t):
        produce(tile_at(t + 1), 1)
        consume(tile_at(t), 0)
        produce(tile_at(t + 2), 0)
        consume(tile_at(t + 1), 1)

    done = 0
    for steps in body_steps:
        def body(u, c, steps=steps, done=done):
            for j in range(0, steps, 2):
                two_steps(done + steps * u + j)
            return c

        trips = (n - 1 - done) // steps
        lax.fori_loop(0, trips, body, 0)
        done = done + steps * trips
    last = n - 1

    @pl.when(last % 2 == 1)
    def _():
        produce(tile_at(last), 1)
        consume(tile_at(last - 1), 0)
        consume(tile_at(last), 1)

    @pl.when(last % 2 == 0)
    def _():
        consume(tile_at(last), 0)


def _lane_group_sum(p):
    out = p[:, :LANES]
    for j in range(1, p.shape[1] // LANES):
        out = out + p[:, j * LANES:(j + 1) * LANES]
    return out


def _a_attn_kernel(cnt_ref, list_ref, q_ref, k_ref, v_ref, pq_ref, pk_ref, o_ref,
                   p_sc, acc_sc, m_sc, alpha_sc, *, tq, tk, nk, group, online):
    step = pl.program_id(1)
    acc_sc[...] = jnp.zeros_like(acc_sc)
    if online:
        m_sc[...] = jnp.full_like(m_sc, NEG_BIG)

    def item_at(t):
        code = list_ref[step * (group * nk) + t]
        return code // nk, code % nk

    def produce(item, slot):
        sub, ki = item
        rows = pl.ds(pl.multiple_of(sub * tq, tq), tq)
        k = k_ref[pl.ds(pl.multiple_of(ki * tk, tk), tk), :]
        s = lax.dot_general(q_ref[rows, :], k, _NT, preferred_element_type=F32)
        qchunk = jnp.right_shift(pq_ref[rows, :], CHUNK_SHIFT)
        kchunk = jnp.right_shift(pk_ref[ki], CHUNK_SHIFT)
        s = jnp.where(kchunk <= qchunk, s, NEG_BIG)
        if online:
            m_prev = m_sc[sub]
            m_new = jnp.maximum(m_prev, jnp.max(s, axis=1, keepdims=True))
            alpha_sc[slot] = jnp.exp2(m_prev - m_new)
            m_sc[sub] = m_new
            s = s - m_new
        p_sc[slot] = jnp.exp2(s).astype(BF16)

    def consume(item, slot):
        sub, ki = item
        v = v_ref[pl.ds(pl.multiple_of(ki * tk, tk), tk), :]
        pv = jnp.dot(p_sc[slot], v, preferred_element_type=F32)
        if online:
            acc_sc[sub] = alpha_sc[slot] * acc_sc[sub] + pv
        else:
            acc_sc[sub] += pv

    _pipelined_tiles(cnt_ref[step], item_at, produce, consume, body_steps=(8, 4, 2))
    for sub in range(group):
        acc = acc_sc[sub]
        o_ref[sub * tq:(sub + 1) * tq, :] = (
            acc[:, :A_V] / acc[:, A_ONE_COL:A_ONE_COL + 1]).astype(BF16)


def _a_attn(q, k, v, pos, pos_col, shift_tables, online):
    s = q.shape[0]
    tq, tk = ATTN_TILES_ONLINE if online else ATTN_TILES
    nq, nk = s // tq, s // tk
    group = math.gcd(A_QUERY_GROUP, nq)
    t = _tile_tables(pos, tq, tk) if online else shift_tables
    flat = t["visible"].reshape(nq // group, group * nk)
    items = jnp.argsort(jnp.logical_not(flat), axis=1, stable=True).astype(jnp.int32).reshape(-1)
    counts = flat.sum(axis=1).astype(jnp.int32)
    pos_k3 = pos.reshape(nk, 1, tk)
    tg = group * tq
    grid_spec = pltpu.PrefetchScalarGridSpec(
        num_scalar_prefetch=2,
        grid=(A_HEADS, nq // group),
        in_specs=[pl.BlockSpec((tg, A_HEAD_PAD), lambda h, i, *_: (i, h)),
                  pl.BlockSpec((s, A_HEAD_PAD), lambda h, i, *_: (0, h)),
                  pl.BlockSpec((s, A_V_PAD), lambda h, i, *_: (0, h)),
                  pl.BlockSpec((tg, 1), lambda h, i, *_: (i, 0)),
                  pl.BlockSpec((nk, 1, tk), lambda h, i, *_: (0, 0, 0))],
        out_specs=pl.BlockSpec((tg, A_V), lambda h, i, *_: (i, h)),
        scratch_shapes=[pltpu.VMEM((2, tq, tk), BF16), pltpu.VMEM((group, tq, A_V_PAD), F32),
                        pltpu.VMEM((group, tq, 1), F32), pltpu.VMEM((2, tq, 1), F32)],
    )
    return pl.pallas_call(
        functools.partial(_a_attn_kernel, tq=tq, tk=tk, nk=nk, group=group, online=online),
        grid_spec=grid_spec,
        out_shape=jax.ShapeDtypeStruct((s, A_WIDTH), BF16),
        compiler_params=pltpu.CompilerParams(dimension_semantics=("arbitrary", "arbitrary"),
                                             vmem_limit_bytes=VMEM_LIMIT_BYTES),
        name="a_attention_online" if online else "a_attention",
    )(counts, items, q, k, v, pos_col, pos_k3)


def _b_attn_kernel(gcnt_ref, cnt_ref, list_ref, quarter_ref, q_ref, k_ref, v_ref, pq_ref, pk_ref,
                   slope_ref, shift_ref, lq1_ref, lk1_ref, lq2_ref, lk2_ref, sub_ref, o_ref,
                   p_sc, acc_sc, l_sc, m_sc, alpha_sc, *, tk, nk, lam_init, online):
    qi = pl.program_id(1)
    q = q_ref[...]
    tq = q.shape[0]
    hq, hk = tq // 2, tk // 2
    q_ext = q[:, 2 * B_HD:]
    q_full = [jnp.concatenate([q[:, c * B_HD:(c + 1) * B_HD], q_ext], axis=1) for c in range(2)]
    pq = pq_ref[...]
    qchunk = jnp.right_shift(pq, CHUNK_SHIFT)
    neg_slope = slope_ref[0, :, 0:1]
    shift = shift_ref[0:1, 0:1]
    l_sc[...] = jnp.zeros_like(l_sc)
    acc_sc[...] = jnp.zeros_like(acc_sc)
    if online:
        m_sc[...] = jnp.full_like(m_sc, NEG_BIG)

    def finish(s, c, slot):
        if online:
            m_prev = m_sc[c]
            m_new = jnp.maximum(m_prev, jnp.max(s, axis=1, keepdims=True))
            alpha = jnp.exp2(m_prev - m_new)
            alpha_sc[slot, c] = alpha
            m_sc[c] = m_new
            p = jnp.exp2(s - m_new)
            l_sc[c] = alpha * l_sc[c] + _lane_group_sum(p)
        else:
            p = jnp.exp2(s)
            l_sc[c] += _lane_group_sum(p)
        p_sc[slot, c] = p.astype(BF16)

    def produce_linear(ki, slot):
        k = k_ref[pl.ds(pl.multiple_of(ki * tk, tk), tk), :]
        k_ext = k[:, 2 * B_HD:]
        for c in range(2):
            k_full = jnp.concatenate([k[:, c * B_HD:(c + 1) * B_HD], k_ext], axis=1)
            finish(lax.dot_general(q_full[c], k_full, _NT, preferred_element_type=F32), c, slot)

    def produce_general(ki, slot):
        k = k_ref[pl.ds(pl.multiple_of(ki * tk, tk), tk), :]
        pk = pk_ref[ki]
        bias = jnp.abs(pq - pk).astype(F32) * neg_slope - shift
        bias = jnp.where(jnp.right_shift(pk, CHUNK_SHIFT) <= qchunk, bias, NEG_BIG)
        for c in range(2):
            s = lax.dot_general(q[:, c * B_HD:(c + 1) * B_HD], k[:, c * B_HD:(c + 1) * B_HD],
                                _NT, preferred_element_type=F32) + bias
            finish(s, c, slot)

    def produce_general_skip_quarter(ki, slot):
        start = pl.multiple_of(ki * tk, tk)
        pk = pk_ref[ki]
        regions = ((slice(0, tq), slice(0, hk)), (slice(hq, tq), slice(hk, tk)))
        for rows, cols in regions:
            k = k_ref[pl.ds(start + cols.start, hk), :]
            pkc = pk[:, cols]
            bias = jnp.abs(pq[rows] - pkc).astype(F32) * neg_slope - shift
            bias = jnp.where(jnp.right_shift(pkc, CHUNK_SHIFT) <= qchunk[rows], bias, NEG_BIG)
            for c in range(2):
                s = lax.dot_general(q[rows, c * B_HD:(c + 1) * B_HD], k[:, c * B_HD:(c + 1) * B_HD],
                                    _NT, preferred_element_type=F32) + bias
                p = jnp.exp2(s)
                l_sc[c, rows, :] += _lane_group_sum(p)
                p_sc[slot, c, rows, cols] = p.astype(BF16)
        for c in range(2):
            p_sc[slot, c, :hq, hk:] = jnp.zeros((hq, hk), BF16)

    def consume(ki, slot):
        v = v_ref[pl.ds(pl.multiple_of(ki * tk, tk), tk), :]
        for c in range(2):
            pv = jnp.dot(p_sc[slot, c], v, preferred_element_type=F32)
            if online:
                acc_sc[c] = alpha_sc[slot, c] * acc_sc[c] + pv
            else:
                acc_sc[c] += pv

    n_general = gcnt_ref[qi]
    tile_at = lambda t: list_ref[qi * nk + t]

    def general_body(t, c):
        produce_general(tile_at(t), 0)
        consume(tile_at(t), 0)
        return c

    first = jnp.maximum(n_general - 1, 0)
    lax.fori_loop(0, first, general_body, 0)

    skip_quarter = False if online else quarter_ref[qi] != 0

    @pl.when((n_general > 0) & jnp.logical_not(skip_quarter))
    def _():
        produce_general(tile_at(first), 0)

    if not online:
        @pl.when((n_general > 0) & skip_quarter)
        def _():
            produce_general_skip_quarter(tile_at(first), 0)

    @pl.when(n_general == 0)
    def _():
        produce_linear(tile_at(first), 0)

    _pipelined_tiles(cnt_ref[qi] - first, lambda t: tile_at(first + t), produce_linear, consume,
                     first_produced=True)

    lam = (jnp.exp(jnp.sum(lq1_ref[...] * lk1_ref[...], axis=-1, keepdims=True))
           - jnp.exp(jnp.sum(lq2_ref[...] * lk2_ref[...], axis=-1, keepdims=True)) + lam_init)
    l0 = jnp.sum(l_sc[0], axis=-1, keepdims=True)
    l1 = jnp.sum(l_sc[1], axis=-1, keepdims=True)
    od = acc_sc[0] / l0 - lam * (acc_sc[1] / l1)
    o_ref[...] = (od * _rms(od, B_V) * (sub_ref[...] * (1.0 - lam_init))).astype(BF16)


def _b_attn(q, k, v, pos, pos_col, shift_tables, consts, lam_init, online):
    s = q.shape[0]
    tq, tk = ATTN_TILES_ONLINE if online else ATTN_TILES
    nq, nk = s // tq, s // tk
    t = _tile_tables(pos, tq, tk) if online else shift_tables
    tables = (t["n_general"], t["n_visible"], t["order"], t["quarter_last_general"])
    pos_k3 = pos.reshape(nk, 1, tk)
    slopes = consts[0]
    grid_spec = pltpu.PrefetchScalarGridSpec(
        num_scalar_prefetch=4,
        grid=(B_HEADS, nq),
        in_specs=[pl.BlockSpec((tq, B_HEAD_PAD), lambda h, i, *_: (i, h)),
                  pl.BlockSpec((s, B_HEAD_PAD), lambda h, i, *_: (0, h)),
                  pl.BlockSpec((s, B_V), lambda h, i, *_: (0, h), pipeline_mode=pl.Buffered(1)),
                  pl.BlockSpec((tq, 1), lambda h, i, *_: (i, 0)),
                  pl.BlockSpec((nk, 1, tk), lambda h, i, *_: (0, 0, 0)),
                  pl.BlockSpec((1,) + slopes.shape[1:], lambda h, i, *_: (h, 0, 0))]
                 + [pl.BlockSpec(c.shape, lambda h, i, *_: (0, 0)) for c in consts[1:]],
        out_specs=pl.BlockSpec((tq, B_V), lambda h, i, *_: (i, h)),
        scratch_shapes=[pltpu.VMEM((2, 2, tq, tk), BF16), pltpu.VMEM((2, tq, B_V), F32),
                        pltpu.VMEM((2, tq, LANES), F32), pltpu.VMEM((2, tq, 1), F32),
                        pltpu.VMEM((2, 2, tq, 1), F32)],
    )
    return pl.pallas_call(
        functools.partial(_b_attn_kernel, tk=tk, nk=nk, lam_init=lam_init, online=online),
        grid_spec=grid_spec,
        out_shape=jax.ShapeDtypeStruct((s, B_WIDTH), BF16),
        compiler_params=pltpu.CompilerParams(dimension_semantics=("arbitrary", "arbitrary"),
                                             vmem_limit_bytes=VMEM_LIMIT_BYTES),
        name="b_attention_online" if online else "b_attention",
    )(*tables, q, k, v, pos_col, pos_k3, *consts)


def _attend(fn, shift_bound, *args, **kwargs):
    return lax.cond(shift_bound <= MAX_CONST_SHIFT,
                    lambda: fn(*args, online=False, **kwargs),
                    lambda: fn(*args, online=True, **kwargs))


def _lane_onehot(value, lane):
    return jnp.zeros((1, LANES), F32).at[0, lane].set(value)


def _a_consts(norm_g, w_in, q_norm_g, w_q_up, kv_norm_g, w_kv_up, q_gain, k_gain):
    n_lat = A_Q_LORA + A_KV_LORA + A_ROPE
    wa = jnp.pad(w_in[:, :n_lat], ((0, 0), (0, LANES - A_ROPE))).astype(BF16)
    wg = w_in[:, n_lat:].astype(BF16)
    wq = w_q_up.reshape(A_Q_LORA, A_HEADS, A_QK)
    wq = jnp.pad(wq, ((0, 0), (0, 0), (0, A_HEAD_PAD - A_QK)))
    wq = wq.reshape(A_Q_LORA, A_HEADS * A_HEAD_PAD).astype(BF16)
    wkv = w_kv_up.reshape(A_KV_LORA, A_HEADS, A_NOPE + A_V)
    wkv = jnp.concatenate([wkv[:, :, :A_NOPE].reshape(A_KV_LORA, -1),
                           wkv[:, :, A_NOPE:].reshape(A_KV_LORA, -1)], axis=1).astype(BF16)
    q_scale = LOG2E / math.sqrt(A_QK)
    gq = jnp.pad(q_gain * q_scale, (0, A_HEAD_PAD - A_QK)).reshape(1, A_HEAD_PAD)
    gkn = k_gain[:A_NOPE].reshape(1, LANES)
    gkp = jnp.pad(k_gain[A_NOPE:], (0, LANES - A_ROPE)).reshape(1, LANES)
    half = A_ROPE // 2
    inv_freq = ROPE_THETA ** (-jnp.arange(0, A_ROPE, 2, dtype=F32) / A_ROPE)
    zeros = jnp.zeros((LANES - A_ROPE,), F32)
    invf = jnp.concatenate([inv_freq, inv_freq, zeros]).reshape(1, LANES)
    sgn = jnp.concatenate([-jnp.ones((half,), F32), jnp.ones((half,), F32), zeros]).reshape(1, LANES)
    bound = (BOUND_MARGIN * A_QK * q_scale
             * jnp.max(jnp.abs(q_gain)) * jnp.max(jnp.abs(k_gain)))
    bound = bound.astype(BF16).astype(F32)
    qshift = _lane_onehot(-bound, A_ROPE)
    one = _lane_onehot(1.0, A_ROPE)
    consts = (norm_g.reshape(1, -1), wa, wg, q_norm_g.reshape(1, -1), wq, kv_norm_g.reshape(1, -1),
              wkv, gq, gkn, gkp, invf, sgn, qshift, one)
    return consts, bound


def _b_consts(q_gain, k_gain):
    b_scale = LOG2E / math.sqrt(B_HD)
    bound = (BOUND_MARGIN * B_HD * b_scale * jnp.max(jnp.abs(q_gain)) * jnp.max(jnp.abs(k_gain)))
    bound = bound.astype(BF16).astype(F32)
    slopes = 2.0 ** (-8.0 * jnp.arange(1, B_HEADS + 1, dtype=F32) / B_HEADS)
    sigma = LOG2E * slopes
    sig_hi = sigma.astype(BF16).astype(F32)
    sig_lo = (sigma - sig_hi).astype(BF16).astype(F32)
    big = float(1 << POS_SPLIT_SHIFT)
    sig_cols = jnp.stack([sig_hi * big, sig_hi, sig_lo * big, sig_lo], axis=1)
    zeros = lambda n: jnp.zeros((B_HEADS, n), F32)
    extc_q = jnp.concatenate([jnp.broadcast_to(-bound, (B_HEADS, 1)), sig_cols, zeros(LANES - 5)],
                             axis=1)
    extc_k = jnp.concatenate([jnp.ones((B_HEADS, 1), F32), zeros(4), sig_cols, zeros(LANES - 9)],
                             axis=1)
    lanes = jnp.arange(LANES)
    pick = lambda ids, val: jnp.where(jnp.isin(lanes, jnp.array(ids)), val, 0.0).reshape(1, LANES)
    ea_q, eb_q = pick([5, 7], -1.0), pick([6, 8], -1.0)
    ea_k, eb_k = pick([1, 3], 1.0), pick([2, 4], 1.0)
    neg_slopes = jnp.broadcast_to(-(sig_hi + sig_lo)[:, None, None], (B_HEADS, 1, LANES))
    return bound, b_scale, neg_slopes, (extc_q, ea_q, eb_q), (extc_k, ea_k, eb_k)


def kernel(x, positions, a_norm, a_w_in, a_q_norm, a_w_q_up, a_kv_norm, a_w_kv_up, a_q_gain, a_k_gain, a_w_out, b_norm, b_w_in, b_q_gain, b_k_gain, b_lambda_q1, b_lambda_k1, b_lambda_q2, b_lambda_k2, b_subln, b_w_out):
    batch, seq, _ = x.shape
    assert batch == 1
    xs = x[0]
    pos = positions[0]
    pos_col = pos.reshape(seq, 1)
    row = lambda a: a.reshape(1, -1)

    consts, a_bound = _a_consts(a_norm[0], a_w_in[0], a_q_norm[0], a_w_q_up[0], a_kv_norm[0],
                                a_w_kv_up[0], a_q_gain[0], a_k_gain[0])
    q, k, v, sg = _a_prologue(xs, pos_col, consts)
    shift_tables = _tile_tables(pos, *ATTN_TILES)
    o = _attend(_a_attn, a_bound, q, k, v, pos, pos_col, shift_tables)
    xs = _out_proj(xs, o, sg, a_w_out[0].astype(BF16), name="a_out_proj")

    layer_idx = 1
    lam_init = 0.8 - 0.6 * math.exp(-0.3 * layer_idx)
    b_bound, b_scale, neg_slopes, q_ext, k_ext = _b_consts(b_q_gain[0], b_k_gain[0])
    w_in = b_w_in[0].astype(BF16)
    ng = row(b_norm[0])
    section = lambda j: (w_in, (D_MODEL, B_WIDTH), (0, j))
    rows = [(xs, D_MODEL, 0), (pos_col, 1, 0)]
    (q,) = _rows_call(_b_qk_kernel, rows, (ng, section(0), row(b_q_gain[0] * b_scale), *q_ext),
                      (B_HEADS * B_HEAD_PAD,), (BF16,), tm=1024, name="b_q_proj")
    (k,) = _rows_call(_b_qk_kernel, rows, (ng, section(1), row(b_k_gain[0]), *k_ext),
                      (B_HEADS * B_HEAD_PAD,), (BF16,), tm=1024, name="b_k_proj")
    v, sg = _rows_call(_b_vg_kernel, rows[:1], (ng, section(2), section(3)),
                       (B_WIDTH, B_WIDTH), (BF16, BF16), tm=512, name="b_vg_proj")
    b_consts = (neg_slopes, jnp.broadcast_to(b_bound.reshape(1, 1), (1, LANES)),
                row(b_lambda_q1[0]), row(b_lambda_k1[0]), row(b_lambda_q2[0]), row(b_lambda_k2[0]),
                row(b_subln[0]))
    o = _attend(_b_attn, b_bound, q, k, v, pos, pos_col, shift_tables, b_consts, lam_init)
    xs = _out_proj(xs, o, sg, b_w_out[0].astype(BF16), name="b_out_proj")
    return xs[None]
```
